```python
import math
import jax, jax.numpy as jnp
from jax import lax
import numpy as np


D_MODEL = 1024
BATCH = 1
SEQ = 16384
DEPTH = 2

N_HEADS = 8
HEAD_DIM = 64
QK_WIDTH = 2 * N_HEADS * HEAD_DIM
V_WIDTH = N_HEADS * 2 * HEAD_DIM
Q_BLOCK = 128
ATTN_SCALE = HEAD_DIM ** -0.5
CONV_WIDTH = 31
D_FF = 3584
N_EXPERTS = 8
TOP_K = 2
PLE_DIM = 256
N_EVEN = (DEPTH + 1) // 2
N_ODD = DEPTH // 2
NORM_EPS = 1e-6
LN_EPS = 1e-5

kernel_name = 'hybrid_diffattn_conformer_moe_ple'


def rms_norm(x, g):
    xf = x.astype(jnp.float32)
    y = xf * lax.rsqrt(jnp.mean(xf * xf, axis=-1, keepdims=True) + NORM_EPS)
    return (y * g.astype(jnp.float32)).astype(x.dtype)


def layer_norm(x, g, b):
    xf = x.astype(jnp.float32)
    mu = jnp.mean(xf, axis=-1, keepdims=True)
    xc = xf - mu
    y = xc * lax.rsqrt(jnp.mean(xc * xc, axis=-1, keepdims=True) + LN_EPS)
    return (y * g.astype(jnp.float32) + b.astype(jnp.float32)).astype(x.dtype)


def diff_attention(h, w_qkv, w_o, g_q, g_k, lam_q1, lam_k1, lam_q2, lam_k2, g_sub, lambda_init):
    B, S, _ = h.shape
    qkv = h @ w_qkv
    q, k, v = jnp.split(qkv, [QK_WIDTH, 2 * QK_WIDTH], axis=-1)
    q = rms_norm(q.reshape(B, S, N_HEADS, 2, HEAD_DIM), g_q)
    k = rms_norm(k.reshape(B, S, N_HEADS, 2, HEAD_DIM), g_k)
    vf = v.reshape(B, S, N_HEADS, 2 * HEAD_DIM).astype(jnp.float32)
    f32 = jnp.float32
    lam = (jnp.exp(jnp.sum(lam_q1.astype(f32) * lam_k1.astype(f32)))
           - jnp.exp(jnp.sum(lam_q2.astype(f32) * lam_k2.astype(f32)))
           + lambda_init)
    n_blocks = S // Q_BLOCK
    q_blocks = q.reshape(B, n_blocks, Q_BLOCK, N_HEADS, 2, HEAD_DIM).transpose(1, 0, 2, 3, 4, 5)
    starts = jnp.arange(n_blocks, dtype=jnp.int32) * Q_BLOCK
    k_pos = jnp.arange(S, dtype=jnp.int32)

    def block(args):
        qi, start = args
        s = jnp.einsum('bqhcd,bkhcd->bhcqk', qi, k,
                       preferred_element_type=jnp.float32) * ATTN_SCALE
        q_pos = start + jnp.arange(Q_BLOCK, dtype=jnp.int32)
        causal = k_pos[None, :] <= q_pos[:, None]
        s = jnp.where(causal, s, -jnp.inf)
        a = jax.nn.softmax(s, axis=-1)
        a = a[:, :, 0] - lam * a[:, :, 1]
        return jnp.einsum('bhqk,bkhe->bqhe', a, vf)

    o = lax.map(block, (q_blocks, starts))
    o = o.transpose(1, 0, 2, 3, 4).reshape(B, S, N_HEADS, 2 * HEAD_DIM)
    o = rms_norm(o, g_sub) * (1.0 - lambda_init)
    return o.reshape(B, S, V_WIDTH).astype(h.dtype) @ w_o


def conformer_conv(h, w_pw1, b_pw1, w_dw, b_dw, ln_g, ln_b, w_pw2, b_pw2):
    u = h @ w_pw1 + b_pw1
    a, g = jnp.split(u, 2, axis=-1)
    u = a * jax.nn.sigmoid(g)
    u = lax.conv_general_dilated(
        u, w_dw[:, None, :].astype(u.dtype), window_strides=(1,),
        padding=[(CONV_WIDTH - 1, 0)],
        dimension_numbers=('NWC', 'WIO', 'NWC'),
        feature_group_count=D_MODEL) + b_dw
    u = jax.nn.silu(layer_norm(u, ln_g, ln_b))
    return u @ w_pw2 + b_pw2


def swiglu(h, w_gate, w_up, w_down):
    return (jax.nn.silu(h @ w_gate) * (h @ w_up)) @ w_down


def moe_swiglu(h, w_router, b_router, w_gate, w_up, w_down):
    logits = (h @ w_router).astype(jnp.float32) + b_router.astype(jnp.float32)
    top_vals, top_idx = lax.top_k(logits, TOP_K)
    top_w = jax.nn.softmax(top_vals, axis=-1)
    gates = jnp.sum(jax.nn.one_hot(top_idx, N_EXPERTS, dtype=jnp.float32)
                    * top_w[..., None], axis=-2).astype(h.dtype)
    out = jnp.zeros_like(h)
    for e in range(N_EXPERTS):
        out = out + gates[..., e:e + 1] * swiglu(h, w_gate[e], w_up[e], w_down[e])
    return out


def _normal(key, shape, scale):
    return jax.random.normal(key, shape, jnp.float32) * scale


def setup_inputs(seed: int = 0) -> dict:
    key = jax.random.key(seed)
    ks = jax.random.split(key, 34)
    D = D_MODEL
    return {
        'x': _normal(ks[0], (BATCH, SEQ, D), 1.0),
        'p': _normal(ks[1], (DEPTH, BATCH, SEQ, PLE_DIM), 1.0),
        'mix_norm_g': 1.0 + _normal(ks[2], (DEPTH, D), 0.02),
        'ffn_norm_g': 1.0 + _normal(ks[3], (DEPTH, D), 0.02),
        'attn_w_qkv': _normal(ks[4], (N_EVEN, D, 2 * QK_WIDTH + V_WIDTH), D ** -0.5),
        'attn_w_o': _normal(ks[5], (N_EVEN, V_WIDTH, D), V_WIDTH ** -0.5),
        'attn_g_q': 1.0 + _normal(ks[6], (N_EVEN, 2, HEAD_DIM), 0.02),
        'attn_g_k': 1.0 + _normal(ks[7], (N_EVEN, 2, HEAD_DIM), 0.02),
        'attn_lam_q1': _normal(ks[8], (N_EVEN, HEAD_DIM), 0.1),
        'attn_lam_k1': _normal(ks[9], (N_EVEN, HEAD_DIM), 0.1),
        'attn_lam_q2': _normal(ks[10], (N_EVEN, HEAD_DIM), 0.1),
        'attn_lam_k2': _normal(ks[11], (N_EVEN, HEAD_DIM), 0.1),
        'attn_g_sub': 1.0 + _normal(ks[12], (N_EVEN, 2 * HEAD_DIM), 0.02),
        'conv_w_pw1': _normal(ks[13], (N_ODD, D, 2 * D), D ** -0.5),
        'conv_b_pw1': _normal(ks[14], (N_ODD, 2 * D), 0.02),
        'conv_w_dw': _normal(ks[15], (N_ODD, CONV_WIDTH, D), CONV_WIDTH ** -0.5),
        'conv_b_dw': _normal(ks[16], (N_ODD, D), 0.02),
        'conv_ln_g': 1.0 + _normal(ks[17], (N_ODD, D), 0.02),
        'conv_ln_b': _normal(ks[18], (N_ODD, D), 0.02),
        'conv_w_pw2': _normal(ks[19], (N_ODD, D, D), D ** -0.5),
        'conv_b_pw2': _normal(ks[20], (N_ODD, D), 0.02),
        'ffn_w_gate': _normal(ks[21], (N_EVEN, D, D_FF), D ** -0.5),
        'ffn_w_up': _normal(ks[22], (N_EVEN, D, D_FF), D ** -0.5),
        'ffn_w_down': _normal(ks[23], (N_EVEN, D_FF, D), D_FF ** -0.5),
        'moe_w_router': _normal(ks[24], (N_ODD, D, N_EXPERTS), D ** -0.5),
        'moe_b_router': _normal(ks[25], (N_ODD, N_EXPERTS), 0.01),
        'moe_w_gate': _normal(ks[26], (N_ODD, N_EXPERTS, D, D_FF), D ** -0.5),
        'moe_w_up': _normal(ks[27], (N_ODD, N_EXPERTS, D, D_FF), D ** -0.5),
        'moe_w_down': _normal(ks[28], (N_ODD, N_EXPERTS, D_FF, D), D_FF ** -0.5),
        'ple_w_proj': _normal(ks[29], (DEPTH, PLE_DIM, D), PLE_DIM ** -0.5),
        'ple_norm_g': 1.0 + _normal(ks[30], (DEPTH, D), 0.02),
        'ple_gate_norm_g': 1.0 + _normal(ks[31], (DEPTH, D), 0.02),
        'ple_w_gate': _normal(ks[32], (DEPTH, D, D), D ** -0.5),
    }


def reference(x, p, mix_norm_g, ffn_norm_g, attn_w_qkv, attn_w_o, attn_g_q, attn_g_k,
              attn_lam_q1, attn_lam_k1, attn_lam_q2, attn_lam_k2, attn_g_sub,
              conv_w_pw1, conv_b_pw1, conv_w_dw, conv_b_dw, conv_ln_g, conv_ln_b,
              conv_w_pw2, conv_b_pw2, ffn_w_gate, ffn_w_up, ffn_w_down,
              moe_w_router, moe_b_router, moe_w_gate, moe_w_up, moe_w_down,
              ple_w_proj, ple_norm_g, ple_gate_norm_g, ple_w_gate):
    for i in range(DEPTH):
        j = i // 2
        h = rms_norm(x, mix_norm_g[i])
        if i % 2 == 0:
            lambda_init = 0.8 - 0.6 * math.exp(-0.3 * i)
            x = x + diff_attention(h, attn_w_qkv[j], attn_w_o[j], attn_g_q[j], attn_g_k[j],
                                   attn_lam_q1[j], attn_lam_k1[j], attn_lam_q2[j],
                                   attn_lam_k2[j], attn_g_sub[j], lambda_init)
        else:
            x = x + conformer_conv(h, conv_w_pw1[j], conv_b_pw1[j], conv_w_dw[j], conv_b_dw[j],
                                   conv_ln_g[j], conv_ln_b[j], conv_w_pw2[j], conv_b_pw2[j])
        h = rms_norm(x, ffn_norm_g[i])
        if i % 2 == 0:
            x = x + swiglu(h, ffn_w_gate[j], ffn_w_up[j], ffn_w_down[j])
        else:
            x = x + moe_swiglu(h, moe_w_router[j], moe_b_router[j],
                               moe_w_gate[j], moe_w_up[j], moe_w_down[j])
        e = rms_norm(p[i] @ ple_w_proj[i], ple_norm_g[i])
        gate = jax.nn.sigmoid(rms_norm(x, ple_gate_norm_g[i]) @ ple_w_gate[i])
        x = x + gate * e
    return x
```

```python
import functools
import math

import jax
import jax.numpy as jnp
from jax import lax
from jax.experimental import pallas as pl
from jax.experimental.pallas import tpu as pltpu

N_HEADS = 8
HEAD_DIM = 64
N_EXPERTS = 8
CONV_WIDTH = 31
NORM_EPS = 1e-6
LN_EPS = 1e-5
ATTN_SCALE = HEAD_DIM ** -0.5

BF16 = jnp.bfloat16
F32 = jnp.float32
NEG_BIG = -1e30
VMEM_LIMIT_BYTES = 56 * 1024 * 1024
HALO_ROWS = 32


def _params(*sem):
    return pltpu.CompilerParams(dimension_semantics=sem, vmem_limit_bytes=VMEM_LIMIT_BYTES)


def _rms(xf, g):
    return xf * lax.rsqrt(jnp.mean(xf * xf, axis=-1, keepdims=True) + NORM_EPS) * g


def _dot(a, b):
    return jnp.dot(a, b, preferred_element_type=F32)


def _dot_nt(a, b):
    return lax.dot_general(a, b, (((1,), (1,)), ((), ())), preferred_element_type=F32)


def _qkv_kernel(x_ref, g_ref, w_ref, gain_ref, bd_ref, o_ref, h_ref, *, n_qk_tiles):
    j = pl.program_id(1)

    @pl.when(j == 0)
    def _():
        h_ref[...] = _rms(x_ref[...], g_ref[...]).astype(BF16)

    y = _dot(h_ref[...], w_ref[...].astype(BF16))

    @pl.when(j < n_qk_tiles)
    def _():
        ss = _dot((y * y).astype(BF16), bd_ref[...])
        o_ref[...] = (y * lax.rsqrt(ss * (1.0 / HEAD_DIM) + NORM_EPS) * gain_ref[...]).astype(BF16)

    @pl.when(j >= n_qk_tiles)
    def _():
        o_ref[...] = y.astype(BF16)


def _qkv_proj(x, g, w, gain_row, tm, tn):
    s, d = x.shape
    n = w.shape[1]
    n_qk_tiles = (2 * n // 3) // tn
    lane = jnp.arange(tn) // HEAD_DIM
    bd = (lane[:, None] == lane[None, :]).astype(BF16)
    return pl.pallas_call(
        functools.partial(_qkv_kernel, n_qk_tiles=n_qk_tiles),
        grid=(s // tm, n // tn),
        in_specs=[
            pl.BlockSpec((tm, d), lambda i, j: (i, 0)),
            pl.BlockSpec((1, d), lambda i, j: (0, 0)),
            pl.BlockSpec((d, tn), lambda i, j: (0, j)),
            pl.BlockSpec((1, tn), lambda i, j: (0, j)),
            pl.BlockSpec((tn, tn), lambda i, j: (0, 0)),
        ],
        out_specs=pl.BlockSpec((tm, tn), lambda i, j: (i, j)),
        out_shape=jax.ShapeDtypeStruct((s, n), BF16),
        scratch_shapes=[pltpu.VMEM((tm, d), BF16)],
        compiler_params=_params("parallel", "arbitrary"),
        name="qkv_proj",
    )(x, g, w, gain_row, bd)


def _attn_kernel(q_ref, k_ref, vt_ref, lam_ref, gsub_ref, o_ref, acc_ref, *, bq, bk, lambda_init):
    i = pl.program_id(1)
    q = q_ref[...]
    lane = lax.broadcasted_iota(jnp.int32, q.shape, 1)
    q_maps = (jnp.where(lane < HEAD_DIM, q, 0), jnp.where(lane >= HEAD_DIM, q, 0))
    acc_ref[...] = jnp.zeros(acc_ref.shape, F32)

    def step(j, carry, masked):
        kb = k_ref[pl.ds(pl.multiple_of(j * bk, bk), bk), :]
        vb = vt_ref[:, pl.ds(pl.multiple_of(j * bk, bk), bk)]
        out = []
        for c in range(2):
            m_old, l_old = carry[2 * c], carry[2 * c + 1]
            s = _dot_nt(kb, q_maps[c])
            if masked:
                key = lax.broadcasted_iota(jnp.int32, s.shape, 0)
                qry = lax.broadcasted_iota(jnp.int32, s.shape, 1)
                s = jnp.where(key <= qry, s, NEG_BIG)
            m_new = jnp.maximum(m_old, jnp.max(s, axis=0, keepdims=True))
            alpha = jnp.exp(m_old - m_new)
            p = jnp.exp(s - m_new)
            l_new = alpha * l_old + jnp.sum(p, axis=0, keepdims=True)
            acc_ref[c] = alpha * acc_ref[c] + _dot(vb, p.astype(BF16))
            out += [m_new, l_new]
        return tuple(out)

    init = (jnp.full((1, bq), NEG_BIG, F32), jnp.zeros((1, bq), F32)) * 2
    assert bq == bk, "diagonal handling assumes square blocks"
    carry = lax.fori_loop(0, i, functools.partial(step, masked=False), init)
    carry = step(i, carry, masked=True)

    lam_vecs = lam_ref[...]
    lam = (jnp.exp(jnp.sum(lam_vecs[0:1] * lam_vecs[1:2], axis=-1, keepdims=True))
           - jnp.exp(jnp.sum(lam_vecs[2:3] * lam_vecs[3:4], axis=-1, keepdims=True))
           + lambda_init)
    o = acc_ref[0] / carry[1] - lam * (acc_ref[1] / carry[3])
    o = o * lax.rsqrt(jnp.mean(o * o, axis=0, keepdims=True) + NORM_EPS)
    o = o * (gsub_ref[...] * (1.0 - lambda_init))
    o_ref[...] = o.T.astype(BF16)


def _diff_attention(qk, vt, lam_vecs, gsub_col, lambda_init, bq):
    s = qk.shape[0]
    hd2 = 2 * HEAD_DIM
    return pl.pallas_call(
        functools.partial(_attn_kernel, bq=bq, bk=bq, lambda_init=lambda_init),
        grid=(N_HEADS, s // bq),
        in_specs=[
            pl.BlockSpec((bq, hd2), lambda h, i: (i, h)),
            pl.BlockSpec((s, hd2), lambda h, i: (0, N_HEADS + h)),
            pl.BlockSpec((hd2, s), lambda h, i: (h, 0)),
            pl.BlockSpec((4, HEAD_DIM), lambda h, i: (0, 0)),
            pl.BlockSpec((hd2, 1), lambda h, i: (0, 0)),
        ],
        out_specs=pl.BlockSpec((bq, hd2), lambda h, i: (i, h)),
        out_shape=jax.ShapeDtypeStruct((s, N_HEADS * hd2), BF16),
        scratch_shapes=[pltpu.VMEM((2, hd2, bq), F32)],
        compiler_params=_params("parallel", "arbitrary"),
        name="diff_attn",
    )(qk, qk, vt, lam_vecs, gsub_col)


def _proj_res_kernel(x_ref, a_ref, w_ref, o_ref, wb_ref):
    @pl.when(pl.program_id(0) == 0)
    def _():
        wb_ref[...] = w_ref[...].astype(BF16)

    o_ref[...] = x_ref[...] + _dot(a_ref[...], wb_ref[...])


def _proj_residual(x, a, w, tm):
    s, d = x.shape
    k = a.shape[1]
    return pl.pallas_call(
        _proj_res_kernel,
        grid=(s // tm,),
        in_specs=[
            pl.BlockSpec((tm, d), lambda i: (i, 0)),
            pl.BlockSpec((tm, k), lambda i: (i, 0)),
            pl.BlockSpec((k, d), lambda i: (0, 0)),
        ],
        out_specs=pl.BlockSpec((tm, d), lambda i: (i, 0)),
        out_shape=jax.ShapeDtypeStruct((s, d), F32),
        scratch_shapes=[pltpu.VMEM((k, d), BF16)],
        compiler_params=_params("arbitrary"),
        name="attn_out_proj",
    )(x, a, w)


def _swiglu_kernel(te_ref, nt_ref, x_ref, g_ref, wg_ref, wu_ref, wd_ref, o_ref, h_ref, acc_ref,
                   *, add_residual):
    t = pl.program_id(0)
    f = pl.program_id(1)

    @pl.when(t < nt_ref[0])
    def _():
        @pl.when(f == 0)
        def _():
            h_ref[...] = _rms(x_ref[...], g_ref[...]).astype(BF16)
            acc_ref[...] = jnp.zeros(acc_ref.shape, F32)

        h = h_ref[...]
        gate = _dot(h, wg_ref[0].astype(BF16))
        up = _dot(h, wu_ref[0].astype(BF16))
        act = (gate * jax.nn.sigmoid(gate) * up).astype(BF16)
        acc_ref[...] += _dot(act, wd_ref[0].astype(BF16))

        @pl.when(f == pl.num_programs(1) - 1)
        def _():
            if add_residual:
                o_ref[...] = x_ref[...] + acc_ref[...]
            else:
                o_ref[...] = acc_ref[...]

    @pl.when((t >= nt_ref[0]) & (f == 0))
    def _():
        o_ref[...] = jnp.zeros(o_ref.shape, F32)


def _grouped_swiglu(x, g, w_gate, w_up, w_down, tile_expert, n_tiles, tm, tf, add_residual):
    m, d = x.shape
    dff = w_gate.shape[2]
    nf = dff // tf
    max_tiles = tile_expert.shape[0]

    def row_map(t, f, te, nt):
        return (jnp.minimum(t, nt[0] - 1), 0)

    def _tf(t, f, nt):
        return jnp.where(t < nt[0], f, nf - 1)

    def _te(t, te, nt):
        return te[jnp.minimum(t, nt[0] - 1)]

    return pl.pallas_call(
        functools.partial(_swiglu_kernel, add_residual=add_residual),
        grid_spec=pltpu.PrefetchScalarGridSpec(
            num_scalar_prefetch=2,
            grid=(max_tiles, nf),
            in_specs=[
                pl.BlockSpec((tm, d), row_map),
                pl.BlockSpec((1, d), lambda t, f, te, nt: (0, 0)),
                pl.BlockSpec((1, d, tf), lambda t, f, te, nt: (_te(t, te, nt), 0, _tf(t, f, nt))),
                pl.BlockSpec((1, d, tf), lambda t, f, te, nt: (_te(t, te, nt), 0, _tf(t, f, nt))),
                pl.BlockSpec((1, tf, d), lambda t, f, te, nt: (_te(t, te, nt), _tf(t, f, nt), 0)),
            ],
            out_specs=pl.BlockSpec((tm, d), lambda t, f, te, nt: (t, 0)),
            scratch_shapes=[pltpu.VMEM((tm, d), BF16), pltpu.VMEM((tm, d), F32)],
        ),
        out_shape=jax.ShapeDtypeStruct((m, d), F32),
        compiler_params=_params("arbitrary", "arbitrary"),
        name="swiglu_residual" if add_residual else "expert_swiglu",
    )(tile_expert, n_tiles, x, g, w_gate, w_up, w_down)


def _ple_tail(x, p_ref, wp_ref, wg_ref, g_e_ref, g_gate_ref, o_ref):
    e = _rms(_dot(p_ref[...].astype(BF16), wp_ref[...]), g_e_ref[...])
    gate = jax.nn.sigmoid(_dot(_rms(x, g_gate_ref[...]).astype(BF16), wg_ref[...]))
    o_ref[...] = x + gate * e


def _ple_kernel(x_ref, p_ref, wp_ref, wg_ref, g_e_ref, g_gate_ref, o_ref, wpb_ref, wgb_ref):
    @pl.when(pl.program_id(0) == 0)
    def _():
        wpb_ref[...] = wp_ref[...].astype(BF16)
        wgb_ref[...] = wg_ref[...].astype(BF16)

    _ple_tail(x_ref[...], p_ref, wpb_ref, wgb_ref, g_e_ref, g_gate_ref, o_ref)


def _combine_ple_kernel(pos_ref, x_ref, w_ref, ys_ref, p_ref, wp_ref, wg_ref, g_e_ref, g_gate_ref,
                        o_ref, wpb_ref, wgb_ref, ybuf_ref, sem, *, tc):
    @pl.when(pl.program_id(0) == 0)
    def _():
        wpb_ref[...] = wp_ref[...].astype(BF16)
        wgb_ref[...] = wg_ref[...].astype(BF16)

    def row_copy(k, t):
        return pltpu.make_async_copy(ys_ref.at[pl.ds(pos_ref[k, t], 1), :],
                                     ybuf_ref.at[k, pl.ds(t, 1), :], sem)

    def issue(t, _):
        row_copy(0, t).start()
        row_copy(1, t).start()
        return 0

    lax.fori_loop(0, tc, issue, 0)
    for k in range(2):
        pltpu.make_async_copy(ys_ref.at[pl.ds(0, tc), :], ybuf_ref.at[k], sem).wait()
    w = w_ref[...]
    x = x_ref[...] + w[:, 0:1] * ybuf_ref[0] + w[:, 1:2] * ybuf_ref[1]
    _ple_tail(x, p_ref, wpb_ref, wgb_ref, g_e_ref, g_gate_ref, o_ref)


def _ple(x, p, wp, wg, g_e, g_gate, tm, routed=None):
    s, d = x.shape
    pd = p.shape[1]
    row = lambda i: (i, 0)
    fixed = lambda i: (0, 0)
    tail_specs = [
        pl.BlockSpec((tm, pd), row),
        pl.BlockSpec((pd, d), fixed),
        pl.BlockSpec((d, d), fixed),
        pl.BlockSpec((1, d), fixed),
        pl.BlockSpec((1, d), fixed),
    ]
    scratch = [pltpu.VMEM((pd, d), BF16), pltpu.VMEM((d, d), BF16)]
    if routed is None:
        return pl.pallas_call(
            _ple_kernel,
            grid=(s // tm,),
            in_specs=[pl.BlockSpec((tm, d), row)] + tail_specs,
            out_specs=pl.BlockSpec((tm, d), row),
            out_shape=jax.ShapeDtypeStruct((s, d), F32),
            scratch_shapes=scratch,
            compiler_params=_params("arbitrary"),
            name="ple",
        )(x, p, wp, wg, g_e, g_gate)
    pos, w12, ys = routed
    return pl.pallas_call(
        functools.partial(_combine_ple_kernel, tc=tm),
        grid=(s // tm,),
        in_specs=[
            pl.BlockSpec((2, tm), lambda i: (0, i), memory_space=pltpu.SMEM),
            pl.BlockSpec((tm, d), row),
            pl.BlockSpec((tm, 2), row),
            pl.BlockSpec(memory_space=pl.ANY),
        ] + tail_specs,
        out_specs=pl.BlockSpec((tm, d), row),
        out_shape=jax.ShapeDtypeStruct((s, d), F32),
        scratch_shapes=scratch + [pltpu.VMEM((2, tm, d), F32), pltpu.SemaphoreType.DMA(())],
        compiler_params=_params("arbitrary"),
        name="combine_ple",
    )(pos, x, w12, ys, p, wp, wg, g_e, g_gate)


def _glu_kernel(x_ref, g_ref, wa_ref, wg_ref, ba_ref, bg_ref, o_ref, h_ref):
    @pl.when(pl.program_id(1) == 0)
    def _():
        h_ref[...] = _rms(x_ref[...], g_ref[...]).astype(BF16)

    h = h_ref[...]
    a = _dot(h, wa_ref[...].astype(BF16)) + ba_ref[...]
    gate = _dot(h, wg_ref[...].astype(BF16)) + bg_ref[...]
    o_ref[...] = a * jax.nn.sigmoid(gate)


def _pointwise_glu(x, g, w, b, tm, tn):
    s, d = x.shape
    n = w.shape[1] // 2
    nj = n // tn
    return pl.pallas_call(
        _glu_kernel,
        grid=(s // tm, nj),
        in_specs=[
            pl.BlockSpec((tm, d), lambda i, j: (i, 0)),
            pl.BlockSpec((1, d), lambda i, j: (0, 0)),
            pl.BlockSpec((d, tn), lambda i, j: (0, j)),
            pl.BlockSpec((d, tn), lambda i, j: (0, j + nj)),
            pl.BlockSpec((1, tn), lambda i, j: (0, j)),
            pl.BlockSpec((1, tn), lambda i, j: (0, j + nj)),
        ],
        out_specs=pl.BlockSpec((tm, tn), lambda i, j: (i, j)),
        out_shape=jax.ShapeDtypeStruct((s, n), F32),
        scratch_shapes=[pltpu.VMEM((tm, d), BF16)],
        compiler_params=_params("parallel", "arbitrary"),
        name="conv_pw1_glu",
    )(x, g, w, w, b, b)


def _conv_kernel(x_ref, u_ref, halo_ref, wdw_ref, bdw_ref, lng_ref, lnb_ref, w2_ref, b2_ref,
                 o_ref, ext_ref, c_ref, w2b_ref, *, tm, rc):
    i = pl.program_id(0)

    @pl.when(i == 0)
    def _():
        w2b_ref[...] = w2_ref[...].astype(BF16)

    halo = halo_ref[...]
    ext_ref[0:HALO_ROWS, :] = jnp.where(i > 0, halo, jnp.zeros_like(halo))
    ext_ref[HALO_ROWS:, :] = u_ref[...]
    first = HALO_ROWS - (CONV_WIDTH - 1)

    def chunk(r, _):
        base = pl.multiple_of(r * rc, rc)
        acc = jnp.zeros((rc, u_ref.shape[1]), F32) + bdw_ref[...]
        win = ext_ref[pl.ds(base, rc + HALO_ROWS), :]
        for t in range(CONV_WIDTH):
            acc = acc + wdw_ref[t:t + 1, :] * win[first + t:first + t + rc, :]
        c_ref[pl.ds(base, rc), :] = acc
        return 0

    lax.fori_loop(0, tm // rc, chunk, 0)
    c = c_ref[...]
    mu = jnp.mean(c, axis=-1, keepdims=True)
    cc = c - mu
    y = cc * lax.rsqrt(jnp.mean(cc * cc, axis=-1, keepdims=True) + LN_EPS) * lng_ref[...] + lnb_ref[...]
    y = y * jax.nn.sigmoid(y)
    o_ref[...] = x_ref[...] + _dot(y.astype(BF16), w2b_ref[...]) + b2_ref[...]


def _conv_module_tail(x, u, w_dw, b_dw, ln_g, ln_b, w2, b2, tm):
    s, d = x.shape
    row = lambda i: (i, 0)
    fixed = lambda i: (0, 0)
    per = tm // HALO_ROWS
    return pl.pallas_call(
        functools.partial(_conv_kernel, tm=tm, rc=8),
        grid=(s // tm,),
        in_specs=[
            pl.BlockSpec((tm, d), row),
            pl.BlockSpec((tm, d), row),
            pl.BlockSpec((HALO_ROWS, d), lambda i: (jnp.maximum(i * per - 1, 0), 0)),
            pl.BlockSpec((CONV_WIDTH, d), fixed),
            pl.BlockSpec((1, d), fixed),
            pl.BlockSpec((1, d), fixed),
            pl.BlockSpec((1, d), fixed),
            pl.BlockSpec((d, d), fixed),
            pl.BlockSpec((1, d), fixed),
        ],
        out_specs=pl.BlockSpec((tm, d), row),
        out_shape=jax.ShapeDtypeStruct((s, d), F32),
        scratch_shapes=[pltpu.VMEM((tm + HALO_ROWS, d), F32), pltpu.VMEM((tm, d), F32),
                        pltpu.VMEM((d, d), BF16)],
        compiler_params=_params("arbitrary"),
        name="conv_ln_pw2",
    )(x, u, u, w_dw, b_dw, ln_g, ln_b, w2, b2)


def _router_kernel(x_ref, g_ref, wrt_ref, b_ref, info_ref, cnt_ref, tri_ref, carry_ref, *, tm):
    i = pl.program_id(0)

    @pl.when(i == 0)
    def _():
        a = lax.broadcasted_iota(jnp.int32, (tm, tm), 0)
        b = lax.broadcasted_iota(jnp.int32, (tm, tm), 1)
        tri_ref[...] = (a < b).astype(BF16)
        carry_ref[...] = jnp.zeros(carry_ref.shape, F32)

    h = _rms(x_ref[...], g_ref[...])
    logits = lax.dot_general(wrt_ref[...], h, (((1,), (1,)), ((), ())),
                             preferred_element_type=F32,
                             precision=lax.Precision.HIGHEST) + b_ref[...]
    eidx = lax.broadcasted_iota(jnp.int32, logits.shape, 0)
    m1 = jnp.max(logits, axis=0, keepdims=True)
    e1 = jnp.min(jnp.where(logits == m1, eidx, N_EXPERTS), axis=0, keepdims=True)
    rest = jnp.where(eidx == e1, -jnp.inf, logits)
    m2 = jnp.max(rest, axis=0, keepdims=True)
    e2 = jnp.min(jnp.where(rest == m2, eidx, N_EXPERTS), axis=0, keepdims=True)
    z = jnp.exp(m2 - m1)
    w1 = 1.0 / (1.0 + z)
    w2 = z / (1.0 + z)
    sel1 = eidx == e1
    sel2 = eidx == e2
    sel = (sel1 | sel2).astype(F32)
    rank = carry_ref[:, 0:1] + _dot(sel.astype(BF16), tri_ref[...])
    r1 = jnp.sum(jnp.where(sel1, rank, 0.0), axis=0, keepdims=True)
    r2 = jnp.sum(jnp.where(sel2, rank, 0.0), axis=0, keepdims=True)
    rows = (e1.astype(F32), e2.astype(F32), w1, w2, r1, r2)
    info = jnp.zeros(logits.shape, F32)
    for k, v in enumerate(rows):
        info = jnp.where(eidx == k, v, info)
    info_ref[...] = info
    carry_ref[...] = carry_ref[...] + jnp.sum(sel, axis=1, keepdims=True)
    cnt_ref[...] = carry_ref[...]


def _router(x, g, w_router_t, b_col, tm):
    s, d = x.shape
    return pl.pallas_call(
        functools.partial(_router_kernel, tm=tm),
        grid=(s // tm,),
        in_specs=[
            pl.BlockSpec((tm, d), lambda i: (i, 0)),
            pl.BlockSpec((1, d), lambda i: (0, 0)),
            pl.BlockSpec((N_EXPERTS, d), lambda i: (0, 0)),
            pl.BlockSpec((N_EXPERTS, 1), lambda i: (0, 0)),
        ],
        out_specs=[
            pl.BlockSpec((N_EXPERTS, tm), lambda i: (0, i)),
            pl.BlockSpec((N_EXPERTS, 128), lambda i: (0, 0)),
        ],
        out_shape=[jax.ShapeDtypeStruct((N_EXPERTS, s), F32),
                   jax.ShapeDtypeStruct((N_EXPERTS, 128), F32)],
        scratch_shapes=[pltpu.VMEM((tm, tm), BF16), pltpu.VMEM((N_EXPERTS, 128), F32)],
        compiler_params=_params("arbitrary"),
        name="router_top2",
    )(x, g, w_router_t, b_col)


def _dispatch_kernel(pos_ref, x_ref, xs_in_ref, xs_ref, sem, *, td):
    del xs_in_ref
    base = pl.program_id(0) * td

    def row_copy(k, t):
        return pltpu.make_async_copy(x_ref.at[pl.ds(base + t, 1), :],
                                     xs_ref.at[pl.ds(pos_ref[k, t], 1), :], sem)

    def issue(t, _):
        row_copy(0, t).start()
        row_copy(1, t).start()
        return 0

    lax.fori_loop(0, td, issue, 0)
    for _ in range(2):
        pltpu.make_async_copy(x_ref.at[pl.ds(0, td), :], xs_ref.at[pl.ds(0, td), :], sem).wait()


def _dispatch(x, pos, m_pad, td):
    s, d = x.shape
    return pl.pallas_call(
        functools.partial(_dispatch_kernel, td=td),
        grid=(s // td,),
        in_specs=[
            pl.BlockSpec((2, td), lambda i: (0, i), memory_space=pltpu.SMEM),
            pl.BlockSpec(memory_space=pl.ANY),
            pl.BlockSpec(memory_space=pl.ANY),
        ],
        out_specs=pl.BlockSpec(memory_space=pl.ANY),
        out_shape=jax.ShapeDtypeStruct((m_pad, d), F32),
        scratch_shapes=[pltpu.SemaphoreType.DMA(())],
        input_output_aliases={2: 0},
        compiler_params=_params("arbitrary"),
        name="moe_dispatch",
    )(pos, x, jnp.zeros((m_pad, d), F32))


def _tiles(s):
    t = lambda want: min(want, s)
    return dict(qkv=(t(1024), 512), attn=t(256), proj=t(512), ffn=(t(1024), 512), ple=t(512),
                glu=(t(1024), 512), conv=t(512), router=t(512), move=t(512), expert=(t(512), 512))


def _layer_even(x, p, tiles, mix_g, ffn_g, w_qkv, w_o, g_q, g_k, lam_vecs, g_sub, lambda_init,
                w_gate, w_up, w_down, ple_wp, ple_g, ple_gate_g, ple_wg):
    s, d = x.shape
    ones = jnp.ones((N_HEADS * 2 * HEAD_DIM,), F32)
    gain_row = jnp.concatenate([jnp.tile(g_q.reshape(-1), N_HEADS) * ATTN_SCALE,
                                jnp.tile(g_k.reshape(-1), N_HEADS), ones])[None, :]
    qkv = _qkv_proj(x, mix_g[None, :], w_qkv, gain_row, *tiles["qkv"])
    vt = qkv[:, 2 * d:].T
    o = _diff_attention(qkv, vt, lam_vecs, g_sub[:, None], lambda_init, tiles["attn"])
    x = _proj_residual(x, o, w_o, tiles["proj"])
    tm, tf = tiles["ffn"]
    n_tiles = s // tm
    x = _grouped_swiglu(x, ffn_g[None, :], w_gate[None], w_up[None], w_down[None],
                        jnp.zeros((n_tiles,), jnp.int32), jnp.full((1,), n_tiles, jnp.int32),
                        tm, tf, add_residual=True)
    return _ple(x, p, ple_wp, ple_wg, ple_g[None, :], ple_gate_g[None, :], tiles["ple"])


def _layer_odd(x, p, tiles, mix_g, ffn_g, w_pw1, b_pw1, w_dw, b_dw, ln_g, ln_b, w_pw2, b_pw2,
               w_router, b_router, w_gate, w_up, w_down, ple_wp, ple_g, ple_gate_g, ple_wg):
    s, d = x.shape
    u = _pointwise_glu(x, mix_g[None, :], w_pw1, b_pw1[None, :], *tiles["glu"])
    x = _conv_module_tail(x, u, w_dw, b_dw[None, :], ln_g[None, :], ln_b[None, :], w_pw2,
                          b_pw2[None, :], tiles["conv"])
    info, counts = _router(x, ffn_g[None, :], w_router.T, b_router[:, None], tiles["router"])
    tm, tf = tiles["expert"]
    e12 = info[0:2].astype(jnp.int32)
    w12 = info[2:4].T
    r12 = info[4:6].astype(jnp.int32)
    cnt = counts[:, 0].astype(jnp.int32)
    group_tiles = (cnt + tm - 1) // tm
    tile_end = jnp.cumsum(group_tiles)
    offsets = (tile_end - group_tiles) * tm
    pos = offsets[e12] + r12
    max_tiles = (2 * s) // tm + N_EXPERTS
    tile_expert = jnp.minimum(jnp.searchsorted(tile_end, jnp.arange(max_tiles), side="right"),
                              N_EXPERTS - 1).astype(jnp.int32)
    n_tiles = tile_end[-1:].astype(jnp.int32)
    xs = _dispatch(x, pos, max_tiles * tm, tiles["move"])
    ys = _grouped_swiglu(xs, ffn_g[None, :], w_gate, w_up, w_down, tile_expert, n_tiles,
                         tm, tf, add_residual=False)
    return _ple(x, p, ple_wp, ple_wg, ple_g[None, :], ple_gate_g[None, :], tiles["move"],
                routed=(pos, w12, ys))


def kernel(x, p, mix_norm_g, ffn_norm_g, attn_w_qkv, attn_w_o, attn_g_q, attn_g_k, attn_lam_q1, attn_lam_k1, attn_lam_q2, attn_lam_k2, attn_g_sub, conv_w_pw1, conv_b_pw1, conv_w_dw, conv_b_dw, conv_ln_g, conv_ln_b, conv_w_pw2, conv_b_pw2, ffn_w_gate, ffn_w_up, ffn_w_down, moe_w_router, moe_b_router, moe_w_gate, moe_w_up, moe_w_down, ple_w_proj, ple_norm_g, ple_gate_norm_g, ple_w_gate):
    depth, batch, s, _ = p.shape
    tiles = _tiles(s)
    outs = []
    for b in range(batch):
        xb = x[b]
        for i in range(depth):
            j = i // 2
            ple_args = (ple_w_proj[i], ple_norm_g[i], ple_gate_norm_g[i], ple_w_gate[i])
            if i % 2 == 0:
                lambda_init = 0.8 - 0.6 * math.exp(-0.3 * i)
                lam_vecs = jnp.stack([attn_lam_q1[j], attn_lam_k1[j], attn_lam_q2[j], attn_lam_k2[j]])
                xb = _layer_even(xb, p[i, b], tiles, mix_norm_g[i], ffn_norm_g[i], attn_w_qkv[j],
                                 attn_w_o[j], attn_g_q[j], attn_g_k[j], lam_vecs, attn_g_sub[j],
                                 lambda_init, ffn_w_gate[j], ffn_w_up[j], ffn_w_down[j], *ple_args)
            else:
                xb = _layer_odd(xb, p[i, b], tiles, mix_norm_g[i], ffn_norm_g[i], conv_w_pw1[j],
                                conv_b_pw1[j], conv_w_dw[j], conv_b_dw[j], conv_ln_g[j], conv_ln_b[j],
                                conv_w_pw2[j], conv_b_pw2[j], moe_w_router[j], moe_b_router[j],
                                moe_w_gate[j], moe_w_up[j], moe_w_down[j], *ple_args)
        outs.append(xb)
    return jnp.stack(outs)
```

```python
import functools
import math

import jax
import jax.numpy as jnp
from jax import lax
from jax.experimental import pallas as pl
from jax.experimental.pallas import tpu as pltpu

N_HEADS = 8
HEAD_DIM = 64
N_EXPERTS = 8
CONV_WIDTH = 31
NORM_EPS = 1e-6
LN_EPS = 1e-5
ATTN_SCALE = HEAD_DIM ** -0.5
LOG2_E = math.log2(math.e)

BF16 = jnp.bfloat16
F32 = jnp.float32
NEG_BIG = -1e30
VMEM_LIMIT_BYTES = 56 * 1024 * 1024
HALO_ROWS = 32


def _params(*sem):
    return pltpu.CompilerParams(dimension_semantics=sem, vmem_limit_bytes=VMEM_LIMIT_BYTES)


def _rms(xf, g):
    return xf * lax.rsqrt(jnp.mean(xf * xf, axis=-1, keepdims=True) + NORM_EPS) * g


def _dot(a, b):
    return jnp.dot(a, b, preferred_element_type=F32)


def _dot_nt(a, b):
    return lax.dot_general(a, b, (((1,), (1,)), ((), ())), preferred_element_type=F32)


def _qkv_kernel(x_ref, g_ref, w_ref, gain_ref, bd_ref, o_ref, h_ref, *, n_qk_tiles):
    j = pl.program_id(1)

    @pl.when(j == 0)
    def _():
        h_ref[...] = _rms(x_ref[...], g_ref[...]).astype(BF16)

    y = _dot(h_ref[...], w_ref[...].astype(BF16))

    @pl.when(j < n_qk_tiles)
    def _():
        ss = _dot((y * y).astype(BF16), bd_ref[...])
        o_ref[...] = (y * lax.rsqrt(ss * (1.0 / HEAD_DIM) + NORM_EPS) * gain_ref[...]).astype(BF16)

    @pl.when(j >= n_qk_tiles)
    def _():
        o_ref[...] = y.astype(BF16)


def _qkv_proj(x, g, w, gain_row, tm, tn):
    s, d = x.shape
    n = w.shape[1]
    n_qk_tiles = (2 * n // 3) // tn
    lane = jnp.arange(tn) // HEAD_DIM
    bd = (lane[:, None] == lane[None, :]).astype(BF16)
    return pl.pallas_call(
        functools.partial(_qkv_kernel, n_qk_tiles=n_qk_tiles),
        grid=(s // tm, n // tn),
        in_specs=[
            pl.BlockSpec((tm, d), lambda i, j: (i, 0)),
            pl.BlockSpec((1, d), lambda i, j: (0, 0)),
            pl.BlockSpec((d, tn), lambda i, j: (0, j)),
            pl.BlockSpec((1, tn), lambda i, j: (0, j)),
            pl.BlockSpec((tn, tn), lambda i, j: (0, 0)),
        ],
        out_specs=pl.BlockSpec((tm, tn), lambda i, j: (i, j)),
        out_shape=jax.ShapeDtypeStruct((s, n), BF16),
        scratch_shapes=[pltpu.VMEM((tm, d), BF16)],
        compiler_params=_params("parallel", "arbitrary"),
        name="qkv_proj",
    )(x, g, w, gain_row, bd)


def _attn_kernel(q_ref, k_ref, vt_ref, lam_ref, gsub_ref, o_ref, qq_ref, acc_ref, m_ref, l_ref,
                 sa_ref, sb_ref, *, bq, lambda_init):
    i = pl.program_id(1)
    q = q_ref[...]
    lane = lax.broadcasted_iota(jnp.int32, q.shape, 1)
    qq_ref[0:bq, :] = jnp.where(lane < HEAD_DIM, q, 0)
    qq_ref[bq:, :] = jnp.where(lane >= HEAD_DIM, q, 0)
    acc_ref[...] = jnp.zeros(acc_ref.shape, F32)
    m_ref[...] = jnp.full(m_ref.shape, NEG_BIG, F32)
    l_ref[...] = jnp.zeros(l_ref.shape, F32)

    def scores(j, s_ref):
        kb = k_ref[pl.ds(pl.multiple_of(j * bq, bq), bq), :]
        s_ref[...] = _dot_nt(kb, qq_ref[...])

    def softmax_pv(j, s_ref, masked):
        vb = vt_ref[:, pl.ds(pl.multiple_of(j * bq, bq), bq)]
        s = s_ref[...]
        if masked:
            key = lax.broadcasted_iota(jnp.int32, s.shape, 0)
            qry = lax.broadcasted_iota(jnp.int32, s.shape, 1)
            s = jnp.where(key <= jnp.where(qry >= bq, qry - bq, qry), s, NEG_BIG)
        m_old = m_ref[...]
        m_new = jnp.maximum(m_old, jnp.max(s, axis=0, keepdims=True))
        alpha = jnp.exp2(m_old - m_new)
        p = jnp.exp2(s - m_new)
        l_ref[...] = alpha * l_ref[...] + jnp.sum(p, axis=0, keepdims=True)
        m_ref[...] = m_new
        acc_ref[...] = alpha * acc_ref[...] + _dot(vb, p.astype(BF16))

    scores(0, sa_ref)

    def pair(jj, _):
        j = 2 * jj
        scores(j + 1, sb_ref)
        softmax_pv(j, sa_ref, masked=False)
        scores(j + 2, sa_ref)
        softmax_pv(j + 1, sb_ref, masked=False)
        return 0

    lax.fori_loop(0, i // 2, pair, 0)

    @pl.when(i % 2 == 1)
    def _():
        scores(i, sb_ref)
        softmax_pv(i - 1, sa_ref, masked=False)
        softmax_pv(i, sb_ref, masked=True)

    @pl.when(i % 2 == 0)
    def _():
        softmax_pv(i, sa_ref, masked=True)

    l = l_ref[...]
    lam_vecs = lam_ref[...]
    lam = (jnp.exp(jnp.sum(lam_vecs[0:1] * lam_vecs[1:2], axis=-1, keepdims=True))
           - jnp.exp(jnp.sum(lam_vecs[2:3] * lam_vecs[3:4], axis=-1, keepdims=True))
           + lambda_init)
    o = acc_ref[:, 0:bq] / l[:, 0:bq] - lam * (acc_ref[:, bq:] / l[:, bq:])
    o = o * lax.rsqrt(jnp.mean(o * o, axis=0, keepdims=True) + NORM_EPS)
    o = o * (gsub_ref[...] * (1.0 - lambda_init))
    o_ref[...] = o.T.astype(BF16)


def _diff_attention(qk, vt, lam_vecs, gsub_col, lambda_init, bq):
    s = qk.shape[0]
    hd2 = 2 * HEAD_DIM
    return pl.pallas_call(
        functools.partial(_attn_kernel, bq=bq, lambda_init=lambda_init),
        grid=(N_HEADS, s // bq),
        in_specs=[
            pl.BlockSpec((bq, hd2), lambda h, i: (i, h)),
            pl.BlockSpec((s, hd2), lambda h, i: (0, N_HEADS + h)),
            pl.BlockSpec((hd2, s), lambda h, i: (h, 0)),
            pl.BlockSpec((4, HEAD_DIM), lambda h, i: (0, 0)),
            pl.BlockSpec((hd2, 1), lambda h, i: (0, 0)),
        ],
        out_specs=pl.BlockSpec((bq, hd2), lambda h, i: (i, h)),
        out_shape=jax.ShapeDtypeStruct((s, N_HEADS * hd2), BF16),
        scratch_shapes=[pltpu.VMEM((2 * bq, hd2), BF16), pltpu.VMEM((hd2, 2 * bq), F32),
                        pltpu.VMEM((1, 2 * bq), F32), pltpu.VMEM((1, 2 * bq), F32),
                        pltpu.VMEM((bq, 2 * bq), F32), pltpu.VMEM((bq, 2 * bq), F32)],
        compiler_params=_params("parallel", "arbitrary"),
        name="diff_attn",
    )(qk, qk, vt, lam_vecs, gsub_col)


def _proj_res_kernel(x_ref, a_ref, w_ref, o_ref, wb_ref):
    @pl.when(pl.program_id(0) == 0)
    def _():
        wb_ref[...] = w_ref[...].astype(BF16)

    o_ref[...] = x_ref[...] + _dot(a_ref[...], wb_ref[...])


def _proj_residual(x, a, w, tm):
    s, d = x.shape
    k = a.shape[1]
    return pl.pallas_call(
        _proj_res_kernel,
        grid=(s // tm,),
        in_specs=[
            pl.BlockSpec((tm, d), lambda i: (i, 0)),
            pl.BlockSpec((tm, k), lambda i: (i, 0)),
            pl.BlockSpec((k, d), lambda i: (0, 0)),
        ],
        out_specs=pl.BlockSpec((tm, d), lambda i: (i, 0)),
        out_shape=jax.ShapeDtypeStruct((s, d), F32),
        scratch_shapes=[pltpu.VMEM((k, d), BF16)],
        compiler_params=_params("arbitrary"),
        name="attn_out_proj",
    )(x, a, w)


def _swiglu_kernel(te_ref, nt_ref, x_ref, g_ref, wg_ref, wu_ref, wd_ref, o_ref, h_ref, acc_ref,
                   *, add_residual):
    t = pl.program_id(0)
    f = pl.program_id(1)

    @pl.when(t < nt_ref[0])
    def _():
        @pl.when(f == 0)
        def _():
            h_ref[...] = _rms(x_ref[...], g_ref[...]).astype(BF16)
            acc_ref[...] = jnp.zeros(acc_ref.shape, F32)

        h = h_ref[...]
        gate = _dot(h, wg_ref[0].astype(BF16))
        up = _dot(h, wu_ref[0].astype(BF16))
        act = (gate * jax.nn.sigmoid(gate) * up).astype(BF16)
        acc_ref[...] += _dot(act, wd_ref[0].astype(BF16))

        @pl.when(f == pl.num_programs(1) - 1)
        def _():
            if add_residual:
                o_ref[...] = x_ref[...] + acc_ref[...]
            else:
                o_ref[...] = acc_ref[...]

    @pl.when((t >= nt_ref[0]) & (f == 0))
    def _():
        o_ref[...] = jnp.zeros(o_ref.shape, F32)


def _grouped_swiglu(x, g, w_gate, w_up, w_down, tile_expert, n_tiles, tm, tf, add_residual):
    m, d = x.shape
    dff = w_gate.shape[2]
    nf = dff // tf
    max_tiles = tile_expert.shape[0]

    def row_map(t, f, te, nt):
        return (jnp.minimum(t, nt[0] - 1), 0)

    def _tf(t, f, nt):
        return jnp.where(t < nt[0], f, nf - 1)

    def _te(t, te, nt):
        return te[jnp.minimum(t, nt[0] - 1)]

    return pl.pallas_call(
        functools.partial(_swiglu_kernel, add_residual=add_residual),
        grid_spec=pltpu.PrefetchScalarGridSpec(
            num_scalar_prefetch=2,
            grid=(max_tiles, nf),
            in_specs=[
                pl.BlockSpec((tm, d), row_map),
                pl.BlockSpec((1, d), lambda t, f, te, nt: (0, 0)),
                pl.BlockSpec((1, d, tf), lambda t, f, te, nt: (_te(t, te, nt), 0, _tf(t, f, nt))),
                pl.BlockSpec((1, d, tf), lambda t, f, te, nt: (_te(t, te, nt), 0, _tf(t, f, nt))),
                pl.BlockSpec((1, tf, d), lambda t, f, te, nt: (_te(t, te, nt), _tf(t, f, nt), 0)),
            ],
            out_specs=pl.BlockSpec((tm, d), lambda t, f, te, nt: (t, 0)),
            scratch_shapes=[pltpu.VMEM((tm, d), BF16), pltpu.VMEM((tm, d), F32)],
        ),
        out_shape=jax.ShapeDtypeStruct((m, d), F32),
        compiler_params=_params("arbitrary", "arbitrary"),
        name="swiglu_residual" if add_residual else "expert_swiglu",
    )(tile_expert, n_tiles, x, g, w_gate, w_up, w_down)


def _ple_tail(x, p_ref, wp_ref, wg_ref, g_e_ref, g_gate_ref, o_ref):
    e = _rms(_dot(p_ref[...].astype(BF16), wp_ref[...]), g_e_ref[...])
    gate = jax.nn.sigmoid(_dot(_rms(x, g_gate_ref[...]).astype(BF16), wg_ref[...]))
    o_ref[...] = x + gate * e


def _ple_kernel(x_ref, p_ref, wp_ref, wg_ref, g_e_ref, g_gate_ref, o_ref, wpb_ref, wgb_ref):
    @pl.when(pl.program_id(0) == 0)
    def _():
        wpb_ref[...] = wp_ref[...].astype(BF16)
        wgb_ref[...] = wg_ref[...].astype(BF16)

    _ple_tail(x_ref[...], p_ref, wpb_ref, wgb_ref, g_e_ref, g_gate_ref, o_ref)


def _combine_ple_kernel(pos_ref, x_ref, w_ref, ys_ref, p_ref, wp_ref, wg_ref, g_e_ref, g_gate_ref,
                        o_ref, wpb_ref, wgb_ref, ybuf_ref, sem, *, tc):
    @pl.when(pl.program_id(0) == 0)
    def _():
        wpb_ref[...] = wp_ref[...].astype(BF16)
        wgb_ref[...] = wg_ref[...].astype(BF16)

    def row_copy(k, t):
        return pltpu.make_async_copy(ys_ref.at[pl.ds(pos_ref[k, t], 1), :],
                                     ybuf_ref.at[k, pl.ds(t, 1), :], sem)

    def issue(t, _):
        row_copy(0, t).start()
        row_copy(1, t).start()
        return 0

    lax.fori_loop(0, tc, issue, 0)
    for k in range(2):
        pltpu.make_async_copy(ys_ref.at[pl.ds(0, tc), :], ybuf_ref.at[k], sem).wait()
    w = w_ref[...]
    x = x_ref[...] + w[:, 0:1] * ybuf_ref[0] + w[:, 1:2] * ybuf_ref[1]
    _ple_tail(x, p_ref, wpb_ref, wgb_ref, g_e_ref, g_gate_ref, o_ref)


def _ple(x, p, wp, wg, g_e, g_gate, tm, routed=None):
    s, d = x.shape
    pd = p.shape[1]
    row = lambda i: (i, 0)
    fixed = lambda i: (0, 0)
    tail_specs = [
        pl.BlockSpec((tm, pd), row),
        pl.BlockSpec((pd, d), fixed),
        pl.BlockSpec((d, d), fixed),
        pl.BlockSpec((1, d), fixed),
        pl.BlockSpec((1, d), fixed),
    ]
    scratch = [pltpu.VMEM((pd, d), BF16), pltpu.VMEM((d, d), BF16)]
    if routed is None:
        return pl.pallas_call(
            _ple_kernel,
            grid=(s // tm,),
            in_specs=[pl.BlockSpec((tm, d), row)] + tail_specs,
            out_specs=pl.BlockSpec((tm, d), row),
            out_shape=jax.ShapeDtypeStruct((s, d), F32),
            scratch_shapes=scratch,
            compiler_params=_params("arbitrary"),
            name="ple",
        )(x, p, wp, wg, g_e, g_gate)
    pos, w12, ys = routed
    return pl.pallas_call(
        functools.partial(_combine_ple_kernel, tc=tm),
        grid=(s // tm,),
        in_specs=[
            pl.BlockSpec((2, tm), lambda i: (0, i), memory_space=pltpu.SMEM),
            pl.BlockSpec((tm, d), row),
            pl.BlockSpec((tm, 2), row),
            pl.BlockSpec(memory_space=pl.ANY),
        ] + tail_specs,
        out_specs=pl.BlockSpec((tm, d), row),
        out_shape=jax.ShapeDtypeStruct((s, d), F32),
        scratch_shapes=scratch + [pltpu.VMEM((2, tm, d), F32), pltpu.SemaphoreType.DMA(())],
        compiler_params=_params("arbitrary"),
        name="combine_ple",
    )(pos, x, w12, ys, p, wp, wg, g_e, g_gate)


def _glu_kernel(x_ref, g_ref, wa_ref, wg_ref, ba_ref, bg_ref, o_ref, h_ref):
    @pl.when(pl.program_id(1) == 0)
    def _():
        h_ref[...] = _rms(x_ref[...], g_ref[...]).astype(BF16)

    h = h_ref[...]
    a = _dot(h, wa_ref[...].astype(BF16)) + ba_ref[...]
    gate = _dot(h, wg_ref[...].astype(BF16)) + bg_ref[...]
    o_ref[...] = a * jax.nn.sigmoid(gate)


def _pointwise_glu(x, g, w, b, tm, tn):
    s, d = x.shape
    n = w.shape[1] // 2
    nj = n // tn
    return pl.pallas_call(
        _glu_kernel,
        grid=(s // tm, nj),
        in_specs=[
            pl.BlockSpec((tm, d), lambda i, j: (i, 0)),
            pl.BlockSpec((1, d), lambda i, j: (0, 0)),
            pl.BlockSpec((d, tn), lambda i, j: (0, j)),
            pl.BlockSpec((d, tn), lambda i, j: (0, j + nj)),
            pl.BlockSpec((1, tn), lambda i, j: (0, j)),
            pl.BlockSpec((1, tn), lambda i, j: (0, j + nj)),
        ],
        out_specs=pl.BlockSpec((tm, tn), lambda i, j: (i, j)),
        out_shape=jax.ShapeDtypeStruct((s, n), F32),
        scratch_shapes=[pltpu.VMEM((tm, d), BF16)],
        compiler_params=_params("parallel", "arbitrary"),
        name="conv_pw1_glu",
    )(x, g, w, w, b, b)


def _conv_kernel(x_ref, u_ref, halo_ref, wdw_ref, bdw_ref, lng_ref, lnb_ref, w2_ref, b2_ref,
                 o_ref, ext_ref, c_ref, w2b_ref, *, tm, rc):
    i = pl.program_id(0)

    @pl.when(i == 0)
    def _():
        w2b_ref[...] = w2_ref[...].astype(BF16)

    halo = halo_ref[...]
    ext_ref[0:HALO_ROWS, :] = jnp.where(i > 0, halo, jnp.zeros_like(halo))
    ext_ref[HALO_ROWS:, :] = u_ref[...]
    first = HALO_ROWS - (CONV_WIDTH - 1)

    def chunk(r, _):
        base = pl.multiple_of(r * rc, rc)
        acc = jnp.zeros((rc, u_ref.shape[1]), F32) + bdw_ref[...]
        win = ext_ref[pl.ds(base, rc + HALO_ROWS), :]
        for t in range(CONV_WIDTH):
            acc = acc + wdw_ref[t:t + 1, :] * win[first + t:first + t + rc, :]
        c_ref[pl.ds(base, rc), :] = acc
        return 0

    lax.fori_loop(0, tm // rc, chunk, 0)
    c = c_ref[...]
    mu = jnp.mean(c, axis=-1, keepdims=True)
    cc = c - mu
    y = cc * lax.rsqrt(jnp.mean(cc * cc, axis=-1, keepdims=True) + LN_EPS) * lng_ref[...] + lnb_ref[...]
    y = y * jax.nn.sigmoid(y)
    o_ref[...] = x_ref[...] + _dot(y.astype(BF16), w2b_ref[...]) + b2_ref[...]


def _conv_module_tail(x, u, w_dw, b_dw, ln_g, ln_b, w2, b2, tm):
    s, d = x.shape
    row = lambda i: (i, 0)
    fixed = lambda i: (0, 0)
    per = tm // HALO_ROWS
    return pl.pallas_call(
        functools.partial(_conv_kernel, tm=tm, rc=8),
        grid=(s // tm,),
        in_specs=[
            pl.BlockSpec((tm, d), row),
            pl.BlockSpec((tm, d), row),
            pl.BlockSpec((HALO_ROWS, d), lambda i: (jnp.maximum(i * per - 1, 0), 0)),
            pl.BlockSpec((CONV_WIDTH, d), fixed),
            pl.BlockSpec((1, d), fixed),
            pl.BlockSpec((1, d), fixed),
            pl.BlockSpec((1, d), fixed),
            pl.BlockSpec((d, d), fixed),
            pl.BlockSpec((1, d), fixed),
        ],
        out_specs=pl.BlockSpec((tm, d), row),
        out_shape=jax.ShapeDtypeStruct((s, d), F32),
        scratch_shapes=[pltpu.VMEM((tm + HALO_ROWS, d), F32), pltpu.VMEM((tm, d), F32),
                        pltpu.VMEM((d, d), BF16)],
        compiler_params=_params("arbitrary"),
        name="conv_ln_pw2",
    )(x, u, u, w_dw, b_dw, ln_g, ln_b, w2, b2)


def _router_kernel(x_ref, g_ref, wrt_ref, b_ref, info_ref, cnt_ref, tri_ref, carry_ref, *, tm):
    i = pl.program_id(0)

    @pl.when(i == 0)
    def _():
        a = lax.broadcasted_iota(jnp.int32, (tm, tm), 0)
        b = lax.broadcasted_iota(jnp.int32, (tm, tm), 1)
        tri_ref[...] = (a < b).astype(BF16)
        carry_ref[...] = jnp.zeros(carry_ref.shape, F32)

    h = _rms(x_ref[...], g_ref[...])
    logits = lax.dot_general(wrt_ref[...], h, (((1,), (1,)), ((), ())),
                             preferred_element_type=F32,
                             precision=lax.Precision.HIGHEST) + b_ref[...]
    eidx = lax.broadcasted_iota(jnp.int32, logits.shape, 0)
    m1 = jnp.max(logits, axis=0, keepdims=True)
    e1 = jnp.min(jnp.where(logits == m1, eidx, N_EXPERTS), axis=0, keepdims=True)
    rest = jnp.where(eidx == e1, -jnp.inf, logits)
    m2 = jnp.max(rest, axis=0, keepdims=True)
    e2 = jnp.min(jnp.where(rest == m2, eidx, N_EXPERTS), axis=0, keepdims=True)
    z = jnp.exp(m2 - m1)
    w1 = 1.0 / (1.0 + z)
    w2 = z / (1.0 + z)
    sel1 = eidx == e1
    sel2 = eidx == e2
    sel = (sel1 | sel2).astype(F32)
    rank = carry_ref[:, 0:1] + _dot(sel.astype(BF16), tri_ref[...])
    r1 = jnp.sum(jnp.where(sel1, rank, 0.0), axis=0, keepdims=True)
    r2 = jnp.sum(jnp.where(sel2, rank, 0.0), axis=0, keepdims=True)
    rows = (e1.astype(F32), e2.astype(F32), w1, w2, r1, r2)
    info = jnp.zeros(logits.shape, F32)
    for k, v in enumerate(rows):
        info = jnp.where(eidx == k, v, info)
    info_ref[...] = info
    carry_ref[...] = carry_ref[...] + jnp.sum(sel, axis=1, keepdims=True)
    cnt_ref[...] = carry_ref[...]


def _router(x, g, w_router_t, b_col, tm):
    s, d = x.shape
    return pl.pallas_call(
        functools.partial(_router_kernel, tm=tm),
        grid=(s // tm,),
        in_specs=[
            pl.BlockSpec((tm, d), lambda i: (i, 0)),
            pl.BlockSpec((1, d), lambda i: (0, 0)),
            pl.BlockSpec((N_EXPERTS, d), lambda i: (0, 0)),
            pl.BlockSpec((N_EXPERTS, 1), lambda i: (0, 0)),
        ],
        out_specs=[
            pl.BlockSpec((N_EXPERTS, tm), lambda i: (0, i)),
            pl.BlockSpec((N_EXPERTS, 128), lambda i: (0, 0)),
        ],
        out_shape=[jax.ShapeDtypeStruct((N_EXPERTS, s), F32),
                   jax.ShapeDtypeStruct((N_EXPERTS, 128), F32)],
        scratch_shapes=[pltpu.VMEM((tm, tm), BF16), pltpu.VMEM((N_EXPERTS, 128), F32)],
        compiler_params=_params("arbitrary"),
        name="router_top2",
    )(x, g, w_router_t, b_col)


def _dispatch_kernel(pos_ref, x_ref, xs_in_ref, xs_ref, sem, *, td):
    del xs_in_ref

    def row_copy(k, t):
        return pltpu.make_async_copy(x_ref.at[pl.ds(t, 1), :],
                                     xs_ref.at[pl.ds(pos_ref[k, t], 1), :], sem)

    def issue(t, _):
        row_copy(0, t).start()
        row_copy(1, t).start()
        return 0

    lax.fori_loop(0, td, issue, 0)
    for _ in range(2):
        pltpu.make_async_copy(x_ref, xs_ref.at[pl.ds(0, td), :], sem).wait()


def _dispatch(x, pos, m_pad, td):
    s, d = x.shape
    return pl.pallas_call(
        functools.partial(_dispatch_kernel, td=td),
        grid=(s // td,),
        in_specs=[
            pl.BlockSpec((2, td), lambda i: (0, i), memory_space=pltpu.SMEM),
            pl.BlockSpec((td, d), lambda i: (i, 0)),
            pl.BlockSpec(memory_space=pl.ANY),
        ],
        out_specs=pl.BlockSpec(memory_space=pl.ANY),
        out_shape=jax.ShapeDtypeStruct((m_pad, d), F32),
        scratch_shapes=[pltpu.SemaphoreType.DMA(())],
        input_output_aliases={2: 0},
        compiler_params=_params("arbitrary"),
        name="moe_dispatch",
    )(pos, x, jnp.zeros((m_pad, d), F32))


def _tiles(s):
    t = lambda want: min(want, s)
    return dict(qkv=(t(1024), 512), attn=t(512), proj=t(512), ffn=(t(1024), 512), ple=t(512),
                glu=(t(1024), 512), conv=t(512), router=t(512), move=t(512), expert=(t(512), 512))


def _layer_even(x, p, tiles, mix_g, ffn_g, w_qkv, w_o, g_q, g_k, lam_vecs, g_sub, lambda_init,
                w_gate, w_up, w_down, ple_wp, ple_g, ple_gate_g, ple_wg):
    s, d = x.shape
    ones = jnp.ones((N_HEADS * 2 * HEAD_DIM,), F32)
    gain_row = jnp.concatenate([jnp.tile(g_q.reshape(-1), N_HEADS) * (ATTN_SCALE * LOG2_E),
                                jnp.tile(g_k.reshape(-1), N_HEADS), ones])[None, :]
    qkv = _qkv_proj(x, mix_g[None, :], w_qkv, gain_row, *tiles["qkv"])
    vt = qkv[:, 2 * d:].T
    o = _diff_attention(qkv, vt, lam_vecs, g_sub[:, None], lambda_init, tiles["attn"])
    x = _proj_residual(x, o, w_o, tiles["proj"])
    tm, tf = tiles["ffn"]
    n_tiles = s // tm
    x = _grouped_swiglu(x, ffn_g[None, :], w_gate[None], w_up[None], w_down[None],
                        jnp.zeros((n_tiles,), jnp.int32), jnp.full((1,), n_tiles, jnp.int32),
                        tm, tf, add_residual=True)
    return _ple(x, p, ple_wp, ple_wg, ple_g[None, :], ple_gate_g[None, :], tiles["ple"])


def _layer_odd(x, p, tiles, mix_g, ffn_g, w_pw1, b_pw1, w_dw, b_dw, ln_g, ln_b, w_pw2, b_pw2,
               w_router, b_router, w_gate, w_up, w_down, ple_wp, ple_g, ple_gate_g, ple_wg):
    s, d = x.shape
    u = _pointwise_glu(x, mix_g[None, :], w_pw1, b_pw1[None, :], *tiles["glu"])
    x = _conv_module_tail(x, u, w_dw, b_dw[None, :], ln_g[None, :], ln_b[None, :], w_pw2,
                          b_pw2[None, :], tiles["conv"])
    info, counts = _router(x, ffn_g[None, :], w_router.T, b_router[:, None], tiles["router"])
    tm, tf = tiles["expert"]
    e12 = info[0:2].astype(jnp.int32)
    w12 = info[2:4].T
    r12 = info[4:6].astype(jnp.int32)
    cnt = counts[:, 0].astype(jnp.int32)
    group_tiles = (cnt + tm - 1) // tm
    tile_end = jnp.cumsum(group_tiles)
    offsets = (tile_end - group_tiles) * tm
    expert_ids = jnp.arange(N_EXPERTS, dtype=jnp.int32)
    onehot = e12[:, :, None] == expert_ids[None, None, :]
    pos = jnp.sum(jnp.where(onehot, offsets[None, None, :], 0), axis=-1) + r12
    max_tiles = (2 * s) // tm + N_EXPERTS
    tile_ids = jnp.arange(max_tiles, dtype=jnp.int32)
    tile_expert = jnp.minimum(jnp.sum((tile_end[None, :] <= tile_ids[:, None]).astype(jnp.int32), axis=1),
                              N_EXPERTS - 1)
    n_tiles = tile_end[-1:].astype(jnp.int32)
    xs = _dispatch(x, pos, max_tiles * tm, tiles["move"])
    ys = _grouped_swiglu(xs, ffn_g[None, :], w_gate, w_up, w_down, tile_expert, n_tiles,
                         tm, tf, add_residual=False)
    return _ple(x, p, ple_wp, ple_wg, ple_g[None, :], ple_gate_g[None, :], tiles["move"],
                routed=(pos, w12, ys))


def kernel(x, p, mix_norm_g, ffn_norm_g, attn_w_qkv, attn_w_o, attn_g_q, attn_g_k, attn_lam_q1, attn_lam_k1, attn_lam_q2, attn_lam_k2, attn_g_sub, conv_w_pw1, conv_b_pw1, conv_w_dw, conv_b_dw, conv_ln_g, conv_ln_b, conv_w_pw2, conv_b_pw2, ffn_w_gate, ffn_w_up, ffn_w_down, moe_w_router, moe_b_router, moe_w_gate, moe_w_up, moe_w_down, ple_w_proj, ple_norm_g, ple_gate_norm_g, ple_w_gate):
    depth, batch, s, _ = p.shape
    tiles = _tiles(s)
    outs = []
    for b in range(batch):
        xb = x[b]
        for i in range(depth):
            j = i // 2
            ple_args = (ple_w_proj[i], ple_norm_g[i], ple_gate_norm_g[i], ple_w_gate[i])
            if i % 2 == 0:
                lambda_init = 0.8 - 0.6 * math.exp(-0.3 * i)
                lam_vecs = jnp.stack([attn_lam_q1[j], attn_lam_k1[j], attn_lam_q2[j], attn_lam_k2[j]])
                xb = _layer_even(xb, p[i, b], tiles, mix_norm_g[i], ffn_norm_g[i], attn_w_qkv[j],
                                 attn_w_o[j], attn_g_q[j], attn_g_k[j], lam_vecs, attn_g_sub[j],
                                 lambda_init, ffn_w_gate[j], ffn_w_up[j], ffn_w_down[j], *ple_args)
            else:
                xb = _layer_odd(xb, p[i, b], tiles, mix_norm_g[i], ffn_norm_g[i], conv_w_pw1[j],
                                conv_b_pw1[j], conv_w_dw[j], conv_b_dw[j], conv_ln_g[j], conv_ln_b[j],
                                conv_w_pw2[j], conv_b_pw2[j], moe_w_router[j], moe_b_router[j],
                                moe_w_gate[j], moe_w_up[j], moe_w_down[j], *ple_args)
        outs.append(xb)
    return jnp.stack(outs)
```

```python
import functools
import math

import jax
import jax.numpy as jnp
from jax import lax
from jax.experimental import pallas as pl
from jax.experimental.pallas import tpu as pltpu

N_HEADS = 8
HEAD_DIM = 64
N_EXPERTS = 8
CONV_WIDTH = 31
NORM_EPS = 1e-6
LN_EPS = 1e-5
ATTN_SCALE = HEAD_DIM ** -0.5
LOG2_E = math.log2(math.e)

BF16 = jnp.bfloat16
F32 = jnp.float32
NEG_BIG = -1e30
VMEM_LIMIT_BYTES = 56 * 1024 * 1024
SUBLANES, LANES = 8, 128
ATTN_UNROLL = 4
HALO_ROWS = 32


def _params(*sem):
    return pltpu.CompilerParams(dimension_semantics=sem, vmem_limit_bytes=VMEM_LIMIT_BYTES)


def _rms(xf, g):
    return xf * lax.rsqrt(jnp.mean(xf * xf, axis=-1, keepdims=True) + NORM_EPS) * g


def _dot(a, b):
    return jnp.dot(a, b, preferred_element_type=F32)


def _dot_nt(a, b):
    return lax.dot_general(a, b, (((1,), (1,)), ((), ())), preferred_element_type=F32)


def _qkv_kernel(x_ref, g_ref, w_ref, gain_ref, bd_ref, o_ref, h_ref, *, n_qk_tiles):
    j = pl.program_id(1)

    @pl.when(j == 0)
    def _():
        h_ref[...] = _rms(x_ref[...], g_ref[...]).astype(BF16)

    y = _dot(h_ref[...], w_ref[...].astype(BF16))

    @pl.when(j < n_qk_tiles)
    def _():
        ss = _dot((y * y).astype(BF16), bd_ref[...])
        o_ref[...] = (y * lax.rsqrt(ss * (1.0 / HEAD_DIM) + NORM_EPS) * gain_ref[...]).astype(BF16)

    @pl.when(j >= n_qk_tiles)
    def _():
        o_ref[...] = y.astype(BF16)


def _qkv_proj(x, g, w, gain_row, tm, tn):
    s, d = x.shape
    n = w.shape[1]
    n_qk_tiles = (2 * n // 3) // tn
    lane = jnp.arange(tn) // HEAD_DIM
    bd = (lane[:, None] == lane[None, :]).astype(BF16)
    return pl.pallas_call(
        functools.partial(_qkv_kernel, n_qk_tiles=n_qk_tiles),
        grid=(s // tm, n // tn),
        in_specs=[
            pl.BlockSpec((tm, d), lambda i, j: (i, 0)),
            pl.BlockSpec((1, d), lambda i, j: (0, 0)),
            pl.BlockSpec((d, tn), lambda i, j: (0, j)),
            pl.BlockSpec((1, tn), lambda i, j: (0, j)),
            pl.BlockSpec((tn, tn), lambda i, j: (0, 0)),
        ],
        out_specs=pl.BlockSpec((tm, tn), lambda i, j: (i, j)),
        out_shape=jax.ShapeDtypeStruct((s, n), BF16),
        scratch_shapes=[pltpu.VMEM((tm, d), BF16)],
        compiler_params=_params("parallel", "arbitrary"),
        name="qkv_proj",
    )(x, g, w, gain_row, bd)


def _attn_kernel(q_ref, k_ref, vt_ref, lam_ref, gsub_ref, o_ref, qq_ref, acc_ref, m_ref, l_ref,
                 sa_ref, sb_ref, *, bq, lambda_init, unroll):
    i = pl.program_id(1)
    q = q_ref[...]
    lane = lax.broadcasted_iota(jnp.int32, q.shape, 1)
    qq_ref[0:bq, :] = jnp.where(lane < HEAD_DIM, q, 0)
    qq_ref[bq:, :] = jnp.where(lane >= HEAD_DIM, q, 0)
    acc_ref[...] = jnp.zeros(acc_ref.shape, F32)
    m_ref[...] = jnp.full(m_ref.shape, NEG_BIG, F32)
    l_ref[...] = jnp.zeros(l_ref.shape, F32)

    def scores(j, s_ref):
        kb = k_ref[pl.ds(pl.multiple_of(j * bq, bq), bq), :]
        s_ref[...] = _dot_nt(kb, qq_ref[...])

    def softmax_pv(j, s_ref, masked):
        vb = vt_ref[:, pl.ds(pl.multiple_of(j * bq, bq), bq)]
        s = s_ref[...]
        if masked:
            key = lax.broadcasted_iota(jnp.int32, s.shape, 0)
            qry = lax.broadcasted_iota(jnp.int32, s.shape, 1)
            s = jnp.where(key <= jnp.where(qry >= bq, qry - bq, qry), s, NEG_BIG)
        m_old = m_ref[...]
        m_new = jnp.maximum(m_old, jnp.max(s, axis=0, keepdims=True))
        alpha = jnp.exp2(m_old - m_new)
        p = jnp.exp2(s - m_new)
        l_ref[...] = alpha * l_ref[...] + jnp.sum(p, axis=0, keepdims=True)
        m_ref[...] = m_new
        acc_ref[...] = alpha * acc_ref[...] + _dot(vb, p.astype(BF16))

    bufs = (sa_ref, sb_ref)
    scores(0, sa_ref)

    def group(g, _):
        for t in range(unroll):
            j = unroll * g + t
            scores(j + 1, bufs[(t + 1) % 2])
            softmax_pv(j, bufs[t % 2], masked=False)
        return 0

    lax.fori_loop(0, i // unroll, group, 0)
    j0 = (i // unroll) * unroll

    for r in range(unroll):
        @pl.when(i - j0 == r)
        def _(r=r):
            for t in range(r + 1):
                if t < r:
                    scores(j0 + t + 1, bufs[(t + 1) % 2])
                softmax_pv(j0 + t, bufs[t % 2], masked=(t == r))

    l = l_ref[...]
    lam_vecs = lam_ref[...]
    lam = (jnp.exp(jnp.sum(lam_vecs[0:1] * lam_vecs[1:2], axis=-1, keepdims=True))
           - jnp.exp(jnp.sum(lam_vecs[2:3] * lam_vecs[3:4], axis=-1, keepdims=True))
           + lambda_init)
    o = acc_ref[:, 0:bq] / l[:, 0:bq] - lam * (acc_ref[:, bq:] / l[:, bq:])
    o = o * lax.rsqrt(jnp.mean(o * o, axis=0, keepdims=True) + NORM_EPS)
    o = o * (gsub_ref[...] * (1.0 - lambda_init))
    o_ref[...] = o.T.astype(BF16)


def _diff_attention(qk, vt, lam_vecs, gsub_col, lambda_init, bq):
    s = qk.shape[0]
    hd2 = 2 * HEAD_DIM
    return pl.pallas_call(
        functools.partial(_attn_kernel, bq=bq, lambda_init=lambda_init, unroll=ATTN_UNROLL),
        grid=(N_HEADS, s // bq),
        in_specs=[
            pl.BlockSpec((bq, hd2), lambda h, i: (i, h)),
            pl.BlockSpec((s, hd2), lambda h, i: (0, N_HEADS + h)),
            pl.BlockSpec((hd2, s), lambda h, i: (h, 0)),
            pl.BlockSpec((4, HEAD_DIM), lambda h, i: (0, 0)),
            pl.BlockSpec((hd2, 1), lambda h, i: (0, 0)),
        ],
        out_specs=pl.BlockSpec((bq, hd2), lambda h, i: (i, h)),
        out_shape=jax.ShapeDtypeStruct((s, N_HEADS * hd2), BF16),
        scratch_shapes=[pltpu.VMEM((2 * bq, hd2), BF16), pltpu.VMEM((hd2, 2 * bq), F32),
                        pltpu.VMEM((1, 2 * bq), F32), pltpu.VMEM((1, 2 * bq), F32),
                        pltpu.VMEM((bq, 2 * bq), F32), pltpu.VMEM((bq, 2 * bq), F32)],
        compiler_params=_params("parallel", "arbitrary"),
        name="diff_attn",
    )(qk, qk, vt, lam_vecs, gsub_col)


def _proj_res_kernel(x_ref, a_ref, w_ref, o_ref, wb_ref):
    @pl.when(pl.program_id(0) == 0)
    def _():
        wb_ref[...] = w_ref[...].astype(BF16)

    o_ref[...] = x_ref[...] + _dot(a_ref[...], wb_ref[...])


def _proj_residual(x, a, w, tm):
    s, d = x.shape
    k = a.shape[1]
    return pl.pallas_call(
        _proj_res_kernel,
        grid=(s // tm,),
        in_specs=[
            pl.BlockSpec((tm, d), lambda i: (i, 0)),
            pl.BlockSpec((tm, k), lambda i: (i, 0)),
            pl.BlockSpec((k, d), lambda i: (0, 0)),
        ],
        out_specs=pl.BlockSpec((tm, d), lambda i: (i, 0)),
        out_shape=jax.ShapeDtypeStruct((s, d), F32),
        scratch_shapes=[pltpu.VMEM((k, d), BF16)],
        compiler_params=_params("arbitrary"),
        name="attn_out_proj",
    )(x, a, w)


def _swiglu_kernel(te_ref, nt_ref, x_ref, g_ref, wg_ref, wu_ref, wd_ref, o_ref, h_ref, acc_ref,
                   *, add_residual):
    t = pl.program_id(0)
    f = pl.program_id(1)

    @pl.when(t < nt_ref[0])
    def _():
        @pl.when(f == 0)
        def _():
            h_ref[...] = _rms(x_ref[...], g_ref[...]).astype(BF16)
            acc_ref[...] = jnp.zeros(acc_ref.shape, F32)

        h = h_ref[...]
        gate = _dot(h, wg_ref[0].astype(BF16))
        up = _dot(h, wu_ref[0].astype(BF16))
        act = (gate * jax.nn.sigmoid(gate) * up).astype(BF16)
        acc_ref[...] += _dot(act, wd_ref[0].astype(BF16))

        @pl.when(f == pl.num_programs(1) - 1)
        def _():
            if add_residual:
                o_ref[...] = x_ref[...] + acc_ref[...]
            else:
                o_ref[...] = acc_ref[...]

    @pl.when((t >= nt_ref[0]) & (f == 0))
    def _():
        o_ref[...] = jnp.zeros(o_ref.shape, F32)


def _grouped_swiglu(x, g, w_gate, w_up, w_down, tile_expert, n_tiles, tm, tf, add_residual):
    m, d = x.shape
    dff = w_gate.shape[2]
    nf = dff // tf
    max_tiles = tile_expert.shape[0]

    def _last(nt):
        return jnp.maximum(nt[0] - 1, 0)

    def row_map(t, f, te, nt):
        return (jnp.minimum(t, _last(nt)), 0)

    def _tf(t, f, nt):
        return jnp.where(t < nt[0], f, nf - 1)

    def _te(t, te, nt):
        return te[jnp.minimum(t, _last(nt))]

    return pl.pallas_call(
        functools.partial(_swiglu_kernel, add_residual=add_residual),
        grid_spec=pltpu.PrefetchScalarGridSpec(
            num_scalar_prefetch=2,
            grid=(max_tiles, nf),
            in_specs=[
                pl.BlockSpec((tm, d), row_map),
                pl.BlockSpec((1, d), lambda t, f, te, nt: (0, 0)),
                pl.BlockSpec((1, d, tf), lambda t, f, te, nt: (_te(t, te, nt), 0, _tf(t, f, nt))),
                pl.BlockSpec((1, d, tf), lambda t, f, te, nt: (_te(t, te, nt), 0, _tf(t, f, nt))),
                pl.BlockSpec((1, tf, d), lambda t, f, te, nt: (_te(t, te, nt), _tf(t, f, nt), 0)),
            ],
            out_specs=pl.BlockSpec((tm, d), lambda t, f, te, nt: (t, 0)),
            scratch_shapes=[pltpu.VMEM((tm, d), BF16), pltpu.VMEM((tm, d), F32)],
        ),
        out_shape=jax.ShapeDtypeStruct((m, d), F32),
        compiler_params=_params("arbitrary", "arbitrary"),
        name="swiglu_residual" if add_residual else "expert_swiglu",
    )(tile_expert, n_tiles, x, g, w_gate, w_up, w_down)


def _ple_tail(x, p_ref, wp_ref, wg_ref, g_e_ref, g_gate_ref, o_ref):
    e = _rms(_dot(p_ref[...].astype(BF16), wp_ref[...]), g_e_ref[...])
    gate = jax.nn.sigmoid(_dot(_rms(x, g_gate_ref[...]).astype(BF16), wg_ref[...]))
    o_ref[...] = x + gate * e


def _ple_kernel(x_ref, p_ref, wp_ref, wg_ref, g_e_ref, g_gate_ref, o_ref, wpb_ref, wgb_ref):
    @pl.when(pl.program_id(0) == 0)
    def _():
        wpb_ref[...] = wp_ref[...].astype(BF16)
        wgb_ref[...] = wg_ref[...].astype(BF16)

    _ple_tail(x_ref[...], p_ref, wpb_ref, wgb_ref, g_e_ref, g_gate_ref, o_ref)


def _combine_ple_kernel(pos_ref, x_ref, w_ref, ys_ref, p_ref, wp_ref, wg_ref, g_e_ref, g_gate_ref,
                        o_ref, wpb_ref, wgb_ref, ybuf_ref, sem, *, tc):
    @pl.when(pl.program_id(0) == 0)
    def _():
        wpb_ref[...] = wp_ref[...].astype(BF16)
        wgb_ref[...] = wg_ref[...].astype(BF16)

    def row_copy(k, t):
        return pltpu.make_async_copy(ys_ref.at[pl.ds(pos_ref[k, t], 1), :],
                                     ybuf_ref.at[k, pl.ds(t, 1), :], sem)

    def issue(t, _):
        row_copy(0, t).start(priority=0)
        row_copy(1, t).start(priority=1)
        return 0

    lax.fori_loop(0, tc, issue, 0)
    for k in range(2):
        pltpu.make_async_copy(ys_ref.at[pl.ds(0, tc), :], ybuf_ref.at[k], sem).wait()
    w = w_ref[...]
    x = x_ref[...] + w[:, 0:1] * ybuf_ref[0] + w[:, 1:2] * ybuf_ref[1]
    _ple_tail(x, p_ref, wpb_ref, wgb_ref, g_e_ref, g_gate_ref, o_ref)


def _ple(x, p, wp, wg, g_e, g_gate, tm, routed=None):
    s, d = x.shape
    pd = p.shape[1]
    row = lambda i: (i, 0)
    fixed = lambda i: (0, 0)
    tail_specs = [
        pl.BlockSpec((tm, pd), row),
        pl.BlockSpec((pd, d), fixed),
        pl.BlockSpec((d, d), fixed),
        pl.BlockSpec((1, d), fixed),
        pl.BlockSpec((1, d), fixed),
    ]
    scratch = [pltpu.VMEM((pd, d), BF16), pltpu.VMEM((d, d), BF16)]
    if routed is None:
        return pl.pallas_call(
            _ple_kernel,
            grid=(s // tm,),
            in_specs=[pl.BlockSpec((tm, d), row)] + tail_specs,
            out_specs=pl.BlockSpec((tm, d), row),
            out_shape=jax.ShapeDtypeStruct((s, d), F32),
            scratch_shapes=scratch,
            compiler_params=_params("arbitrary"),
            name="ple",
        )(x, p, wp, wg, g_e, g_gate)
    pos, w12, ys = routed
    return pl.pallas_call(
        functools.partial(_combine_ple_kernel, tc=tm),
        grid=(s // tm,),
        in_specs=[
            pl.BlockSpec((2, tm), lambda i: (0, i), memory_space=pltpu.SMEM),
            pl.BlockSpec((tm, d), row),
            pl.BlockSpec((tm, 2), row),
            pl.BlockSpec(memory_space=pl.ANY),
        ] + tail_specs,
        out_specs=pl.BlockSpec((tm, d), row),
        out_shape=jax.ShapeDtypeStruct((s, d), F32),
        scratch_shapes=scratch + [pltpu.VMEM((2, tm, d), F32), pltpu.SemaphoreType.DMA(())],
        compiler_params=_params("arbitrary"),
        name="combine_ple",
    )(pos, x, w12, ys, p, wp, wg, g_e, g_gate)


def _glu_kernel(x_ref, g_ref, wa_ref, wg_ref, ba_ref, bg_ref, o_ref, h_ref):
    @pl.when(pl.program_id(1) == 0)
    def _():
        h_ref[...] = _rms(x_ref[...], g_ref[...]).astype(BF16)

    h = h_ref[...]
    a = _dot(h, wa_ref[...].astype(BF16)) + ba_ref[...]
    gate = _dot(h, wg_ref[...].astype(BF16)) + bg_ref[...]
    o_ref[...] = a * jax.nn.sigmoid(gate)


def _pointwise_glu(x, g, w, b, tm, tn):
    s, d = x.shape
    n = w.shape[1] // 2
    nj = n // tn
    return pl.pallas_call(
        _glu_kernel,
        grid=(s // tm, nj),
        in_specs=[
            pl.BlockSpec((tm, d), lambda i, j: (i, 0)),
            pl.BlockSpec((1, d), lambda i, j: (0, 0)),
            pl.BlockSpec((d, tn), lambda i, j: (0, j)),
            pl.BlockSpec((d, tn), lambda i, j: (0, j + nj)),
            pl.BlockSpec((1, tn), lambda i, j: (0, j)),
            pl.BlockSpec((1, tn), lambda i, j: (0, j + nj)),
        ],
        out_specs=pl.BlockSpec((tm, tn), lambda i, j: (i, j)),
        out_shape=jax.ShapeDtypeStruct((s, n), F32),
        scratch_shapes=[pltpu.VMEM((tm, d), BF16)],
        compiler_params=_params("parallel", "arbitrary"),
        name="conv_pw1_glu",
    )(x, g, w, w, b, b)


def _conv_kernel(x_ref, u_ref, halo_ref, wdw_ref, bdw_ref, lng_ref, lnb_ref, w2_ref, b2_ref,
                 o_ref, ext_ref, c_ref, w2b_ref, *, tm, rc):
    i = pl.program_id(0)

    @pl.when(i == 0)
    def _():
        w2b_ref[...] = w2_ref[...].astype(BF16)

    halo = halo_ref[...]
    ext_ref[0:HALO_ROWS, :] = jnp.where(i > 0, halo, jnp.zeros_like(halo))
    ext_ref[HALO_ROWS:, :] = u_ref[...]
    first = HALO_ROWS - (CONV_WIDTH - 1)

    win_rows = rc + HALO_ROWS

    def chunk(r, _):
        base = pl.multiple_of(r * rc, rc)
        for lo in range(0, u_ref.shape[1], LANES):
            cols = slice(lo, lo + LANES)
            win = ext_ref[pl.ds(base, win_rows), cols]
            acc = jnp.zeros((rc, LANES), F32) + bdw_ref[:, cols]
            for b in range(SUBLANES):
                wb = win if b == 0 else pltpu.roll(win, shift=win_rows - b, axis=0)
                for a in range(win_rows // SUBLANES):
                    t = SUBLANES * a + b - first
                    if 0 <= t < CONV_WIDTH:
                        acc = acc + wdw_ref[t:t + 1, cols] * wb[SUBLANES * a:SUBLANES * a + rc, :]
            c_ref[pl.ds(base, rc), cols] = acc
        return 0

    lax.fori_loop(0, tm // rc, chunk, 0)
    c = c_ref[...]
    mu = jnp.mean(c, axis=-1, keepdims=True)
    cc = c - mu
    y = cc * lax.rsqrt(jnp.mean(cc * cc, axis=-1, keepdims=True) + LN_EPS) * lng_ref[...] + lnb_ref[...]
    y = y * jax.nn.sigmoid(y)
    o_ref[...] = x_ref[...] + _dot(y.astype(BF16), w2b_ref[...]) + b2_ref[...]


def _conv_module_tail(x, u, w_dw, b_dw, ln_g, ln_b, w2, b2, tm):
    s, d = x.shape
    row = lambda i: (i, 0)
    fixed = lambda i: (0, 0)
    per = tm // HALO_ROWS
    return pl.pallas_call(
        functools.partial(_conv_kernel, tm=tm, rc=min(64, tm)),
        grid=(s // tm,),
        in_specs=[
            pl.BlockSpec((tm, d), row),
            pl.BlockSpec((tm, d), row),
            pl.BlockSpec((HALO_ROWS, d), lambda i: (jnp.maximum(i * per - 1, 0), 0)),
            pl.BlockSpec((CONV_WIDTH, d), fixed),
            pl.BlockSpec((1, d), fixed),
            pl.BlockSpec((1, d), fixed),
            pl.BlockSpec((1, d), fixed),
            pl.BlockSpec((d, d), fixed),
            pl.BlockSpec((1, d), fixed),
        ],
        out_specs=pl.BlockSpec((tm, d), row),
        out_shape=jax.ShapeDtypeStruct((s, d), F32),
        scratch_shapes=[pltpu.VMEM((tm + HALO_ROWS, d), F32), pltpu.VMEM((tm, d), F32),
                        pltpu.VMEM((d, d), BF16)],
        compiler_params=_params("arbitrary"),
        name="conv_ln_pw2",
    )(x, u, u, w_dw, b_dw, ln_g, ln_b, w2, b2)


def _router_kernel(x_ref, g_ref, wrt_ref, b_ref, info_ref, cnt_ref, tri_ref, carry_ref, *, tm):
    i = pl.program_id(0)

    @pl.when(i == 0)
    def _():
        a = lax.broadcasted_iota(jnp.int32, (tm, tm), 0)
        b = lax.broadcasted_iota(jnp.int32, (tm, tm), 1)
        tri_ref[...] = (a < b).astype(BF16)
        carry_ref[...] = jnp.zeros(carry_ref.shape, F32)

    h = _rms(x_ref[...], g_ref[...])
    logits = lax.dot_general(wrt_ref[...], h, (((1,), (1,)), ((), ())),
                             preferred_element_type=F32,
                             precision=lax.Precision.HIGHEST) + b_ref[...]
    eidx = lax.broadcasted_iota(jnp.int32, logits.shape, 0)
    m1 = jnp.max(logits, axis=0, keepdims=True)
    e1 = jnp.min(jnp.where(logits == m1, eidx, N_EXPERTS), axis=0, keepdims=True)
    rest = jnp.where(eidx == e1, -jnp.inf, logits)
    m2 = jnp.max(rest, axis=0, keepdims=True)
    e2 = jnp.min(jnp.where(rest == m2, eidx, N_EXPERTS), axis=0, keepdims=True)
    z = jnp.exp(m2 - m1)
    w1 = 1.0 / (1.0 + z)
    w2 = z / (1.0 + z)
    sel1 = eidx == e1
    sel2 = eidx == e2
    sel = (sel1 | sel2).astype(F32)
    rank = carry_ref[:, 0:1] + _dot(sel.astype(BF16), tri_ref[...])
    r1 = jnp.sum(jnp.where(sel1, rank, 0.0), axis=0, keepdims=True)
    r2 = jnp.sum(jnp.where(sel2, rank, 0.0), axis=0, keepdims=True)
    rows = (e1.astype(F32), e2.astype(F32), w1, w2, r1, r2)
    info = jnp.zeros(logits.shape, F32)
    for k, v in enumerate(rows):
        info = jnp.where(eidx == k, v, info)
    info_ref[...] = info
    carry_ref[...] = carry_ref[...] + jnp.sum(sel, axis=1, keepdims=True)
    cnt_ref[...] = carry_ref[...]


def _router(x, g, w_router_t, b_col, tm):
    s, d = x.shape
    return pl.pallas_call(
        functools.partial(_router_kernel, tm=tm),
        grid=(s // tm,),
        in_specs=[
            pl.BlockSpec((tm, d), lambda i: (i, 0)),
            pl.BlockSpec((1, d), lambda i: (0, 0)),
            pl.BlockSpec((N_EXPERTS, d), lambda i: (0, 0)),
            pl.BlockSpec((N_EXPERTS, 1), lambda i: (0, 0)),
        ],
        out_specs=[
            pl.BlockSpec((N_EXPERTS, tm), lambda i: (0, i)),
            pl.BlockSpec((N_EXPERTS, 128), lambda i: (0, 0)),
        ],
        out_shape=[jax.ShapeDtypeStruct((N_EXPERTS, s), F32),
                   jax.ShapeDtypeStruct((N_EXPERTS, 128), F32)],
        scratch_shapes=[pltpu.VMEM((tm, tm), BF16), pltpu.VMEM((N_EXPERTS, 128), F32)],
        compiler_params=_params("arbitrary"),
        name="router_top2",
    )(x, g, w_router_t, b_col)


def _dispatch_kernel(pos_ref, x_ref, xs_in_ref, xs_ref, sem, *, td):
    del xs_in_ref

    def row_copy(k, t):
        return pltpu.make_async_copy(x_ref.at[pl.ds(t, 1), :],
                                     xs_ref.at[pl.ds(pos_ref[k, t], 1), :], sem)

    def issue(t, _):
        row_copy(0, t).start(priority=0)
        row_copy(1, t).start(priority=1)
        return 0

    lax.fori_loop(0, td, issue, 0)
    for _ in range(2):
        pltpu.make_async_copy(x_ref, xs_ref.at[pl.ds(0, td), :], sem).wait()


def _dispatch(x, pos, m_pad, td):
    s, d = x.shape
    return pl.pallas_call(
        functools.partial(_dispatch_kernel, td=td),
        grid=(s // td,),
        in_specs=[
            pl.BlockSpec((2, td), lambda i: (0, i), memory_space=pltpu.SMEM),
            pl.BlockSpec((td, d), lambda i: (i, 0)),
            pl.BlockSpec(memory_space=pl.ANY),
        ],
        out_specs=pl.BlockSpec(memory_space=pl.ANY),
        out_shape=jax.ShapeDtypeStruct((m_pad, d), F32),
        scratch_shapes=[pltpu.SemaphoreType.DMA(())],
        input_output_aliases={2: 0},
        compiler_params=_params("arbitrary"),
        name="moe_dispatch",
    )(pos, x, jnp.zeros((m_pad, d), F32))


def _tiles(s):
    t = lambda want: min(want, s)
    return dict(qkv=(t(1024), 512), attn=t(512), proj=t(512), ffn=(t(1024), 512), ple=t(512),
                glu=(t(1024), 512), conv=t(512), router=t(512), move=t(512), expert=(t(1024), 512))


def _layer_even(x, p, tiles, mix_g, ffn_g, w_qkv, w_o, g_q, g_k, lam_vecs, g_sub, lambda_init,
                w_gate, w_up, w_down, ple_wp, ple_g, ple_gate_g, ple_wg):
    s, d = x.shape
    ones = jnp.ones((N_HEADS * 2 * HEAD_DIM,), F32)
    gain_row = jnp.concatenate([jnp.tile(g_q.reshape(-1), N_HEADS) * (ATTN_SCALE * LOG2_E),
                                jnp.tile(g_k.reshape(-1), N_HEADS), ones])[None, :]
    qkv = _qkv_proj(x, mix_g[None, :], w_qkv, gain_row, *tiles["qkv"])
    vt = qkv[:, 2 * d:].T
    o = _diff_attention(qkv, vt, lam_vecs, g_sub[:, None], lambda_init, tiles["attn"])
    x = _proj_residual(x, o, w_o, tiles["proj"])
    tm, tf = tiles["ffn"]
    n_tiles = s // tm
    x = _grouped_swiglu(x, ffn_g[None, :], w_gate[None], w_up[None], w_down[None],
                        jnp.zeros((n_tiles,), jnp.int32), jnp.full((1,), n_tiles, jnp.int32),
                        tm, tf, add_residual=True)
    return _ple(x, p, ple_wp, ple_wg, ple_g[None, :], ple_gate_g[None, :], tiles["ple"])


def _layer_odd(x, p, tiles, mix_g, ffn_g, w_pw1, b_pw1, w_dw, b_dw, ln_g, ln_b, w_pw2, b_pw2,
               w_router, b_router, w_gate, w_up, w_down, ple_wp, ple_g, ple_gate_g, ple_wg):
    s, d = x.shape
    u = _pointwise_glu(x, mix_g[None, :], w_pw1, b_pw1[None, :], *tiles["glu"])
    x = _conv_module_tail(x, u, w_dw, b_dw[None, :], ln_g[None, :], ln_b[None, :], w_pw2,
                          b_pw2[None, :], tiles["conv"])
    info, counts = _router(x, ffn_g[None, :], w_router.T, b_router[:, None], tiles["router"])
    tm, tf = tiles["expert"]
    e12 = info[0:2].astype(jnp.int32)
    w12 = info[2:4].T
    r12 = info[4:6].astype(jnp.int32)
    cnt = counts[:, 0].astype(jnp.int32)
    group_tiles = (cnt + tm - 1) // tm
    tile_end = jnp.cumsum(group_tiles)
    offsets = (tile_end - group_tiles) * tm
    expert_ids = jnp.arange(N_EXPERTS, dtype=jnp.int32)
    onehot = e12[:, :, None] == expert_ids[None, None, :]
    pos = jnp.sum(jnp.where(onehot, offsets[None, None, :], 0), axis=-1) + r12
    max_tiles = (2 * s) // tm + N_EXPERTS
    tile_ids = jnp.arange(max_tiles, dtype=jnp.int32)
    tile_expert = jnp.minimum(jnp.sum((tile_end[None, :] <= tile_ids[:, None]).astype(jnp.int32), axis=1),
                              N_EXPERTS - 1)
    n_tiles = tile_end[-1:].astype(jnp.int32)
    xs = _dispatch(x, pos, max_tiles * tm, tiles["move"])
    ys = _grouped_swiglu(xs, ffn_g[None, :], w_gate, w_up, w_down, tile_expert, n_tiles,
                         tm, tf, add_residual=False)
    return _ple(x, p, ple_wp, ple_wg, ple_g[None, :], ple_gate_g[None, :], tiles["move"],
                routed=(pos, w12, ys))


def kernel(x, p, mix_norm_g, ffn_norm_g, attn_w_qkv, attn_w_o, attn_g_q, attn_g_k, attn_lam_q1, attn_lam_k1, attn_lam_q2, attn_lam_k2, attn_g_sub, conv_w_pw1, conv_b_pw1, conv_w_dw, conv_b_dw, conv_ln_g, conv_ln_b, conv_w_pw2, conv_b_pw2, ffn_w_gate, ffn_w_up, ffn_w_down, moe_w_router, moe_b_router, moe_w_gate, moe_w_up, moe_w_down, ple_w_proj, ple_norm_g, ple_gate_norm_g, ple_w_gate):
    depth, batch, s, _ = p.shape
    tiles = _tiles(s)
    outs = []
    for b in range(batch):
        xb = x[b]
        for i in range(depth):
            j = i // 2
            ple_args = (ple_w_proj[i], ple_norm_g[i], ple_gate_norm_g[i], ple_w_gate[i])
            if i % 2 == 0:
                lambda_init = 0.8 - 0.6 * math.exp(-0.3 * i)
                lam_vecs = jnp.stack([attn_lam_q1[j], attn_lam_k1[j], attn_lam_q2[j], attn_lam_k2[j]])
                xb = _layer_even(xb, p[i, b], tiles, mix_norm_g[i], ffn_norm_g[i], attn_w_qkv[j],
                                 attn_w_o[j], attn_g_q[j], attn_g_k[j], lam_vecs, attn_g_sub[j],
                                 lambda_init, ffn_w_gate[j], ffn_w_up[j], ffn_w_down[j], *ple_args)
            else:
                xb = _layer_odd(xb, p[i, b], tiles, mix_norm_g[i], ffn_norm_g[i], conv_w_pw1[j],
                                conv_b_pw1[j], conv_w_dw[j], conv_b_dw[j], conv_ln_g[j], conv_ln_b[j],
                                conv_w_pw2[j], conv_b_pw2[j], moe_w_router[j], moe_b_router[j],
                                moe_w_gate[j], moe_w_up[j], moe_w_down[j], *ple_args)
        outs.append(xb)
    return jnp.stack(outs)
```

```python
import functools
import math

import jax
import jax.numpy as jnp
from jax import lax
from jax.experimental import pallas as pl
from jax.experimental.pallas import tpu as pltpu

N_HEADS = 8
HEAD_DIM = 64
N_EXPERTS = 8
CONV_WIDTH = 31
NORM_EPS = 1e-6
LN_EPS = 1e-5
ATTN_SCALE = HEAD_DIM ** -0.5
LOG2_E = math.log2(math.e)

BF16 = jnp.bfloat16
F32 = jnp.float32
NEG_BIG = -1e30
VMEM_LIMIT_BYTES = 56 * 1024 * 1024
SUBLANES, LANES = 8, 128
ATTN_UNROLL = 4
BOUND_SLACK = 1.01
MAX_FIXED_SHIFT = 40.0
HALO_ROWS = 32


def _params(*sem):
    return pltpu.CompilerParams(dimension_semantics=sem, vmem_limit_bytes=VMEM_LIMIT_BYTES)


def _rms(xf, g):
    return xf * lax.rsqrt(jnp.mean(xf * xf, axis=-1, keepdims=True) + NORM_EPS) * g


def _dot(a, b):
    return jnp.dot(a, b, preferred_element_type=F32)


def _dot_nt(a, b):
    return lax.dot_general(a, b, (((1,), (1,)), ((), ())), preferred_element_type=F32)


def _qkv_kernel(x_ref, g_ref, w_ref, gain_ref, bd_ref, o_ref, h_ref, *, n_qk_tiles):
    j = pl.program_id(1)

    @pl.when(j == 0)
    def _():
        h_ref[...] = _rms(x_ref[...], g_ref[...]).astype(BF16)

    y = _dot(h_ref[...], w_ref[...].astype(BF16))

    @pl.when(j < n_qk_tiles)
    def _():
        ss = _dot((y * y).astype(BF16), bd_ref[...])
        o_ref[...] = (y * lax.rsqrt(ss * (1.0 / HEAD_DIM) + NORM_EPS) * gain_ref[...]).astype(BF16)

    @pl.when(j >= n_qk_tiles)
    def _():
        o_ref[...] = y.astype(BF16)


def _qkv_proj(x, g, w, gain_row, tm, tn):
    s, d = x.shape
    n = w.shape[1]
    n_qk_tiles = (2 * n // 3) // tn
    lane = jnp.arange(tn) // HEAD_DIM
    bd = (lane[:, None] == lane[None, :]).astype(BF16)
    return pl.pallas_call(
        functools.partial(_qkv_kernel, n_qk_tiles=n_qk_tiles),
        grid=(s // tm, n // tn),
        in_specs=[
            pl.BlockSpec((tm, d), lambda i, j: (i, 0)),
            pl.BlockSpec((1, d), lambda i, j: (0, 0)),
            pl.BlockSpec((d, tn), lambda i, j: (0, j)),
            pl.BlockSpec((1, tn), lambda i, j: (0, j)),
            pl.BlockSpec((tn, tn), lambda i, j: (0, 0)),
        ],
        out_specs=pl.BlockSpec((tm, tn), lambda i, j: (i, j)),
        out_shape=jax.ShapeDtypeStruct((s, n), BF16),
        scratch_shapes=[pltpu.VMEM((tm, d), BF16)],
        compiler_params=_params("parallel", "arbitrary"),
        name="qkv_proj",
    )(x, g, w, gain_row, bd)


def _attn_kernel(q_ref, k_ref, vt_ref, knorm_ref, lam_ref, gsub_ref, o_ref, qq_ref, acc_ref, m_ref,
                 l_ref, sa_ref, sb_ref, *, bq, lambda_init, unroll, online):
    i = pl.program_id(1)
    q = q_ref[...]
    lane = lax.broadcasted_iota(jnp.int32, q.shape, 1)
    qq_ref[0:bq, :] = jnp.where(lane < HEAD_DIM, q, 0)
    qq_ref[bq:, :] = jnp.where(lane >= HEAD_DIM, q, 0)
    acc_ref[...] = jnp.zeros(acc_ref.shape, F32)
    l_ref[...] = jnp.zeros(l_ref.shape, F32)
    if online:
        m_ref[...] = jnp.full(m_ref.shape, NEG_BIG, F32)
    else:
        qq = qq_ref[...].astype(F32)
        q_norm2 = _dot_nt(jnp.ones((SUBLANES, 2 * HEAD_DIM), BF16), (qq * qq).astype(BF16))[0:1]
        m_ref[...] = jnp.sqrt(q_norm2) * knorm_ref[0] * BOUND_SLACK

    def score_tile(j):
        kb = k_ref[pl.ds(pl.multiple_of(j * bq, bq), bq), :]
        return _dot_nt(kb, qq_ref[...])

    def scores(j, s_ref):
        if online:
            s_ref[...] = score_tile(j)

    def softmax_pv(j, s_ref, masked):
        vb = vt_ref[:, pl.ds(pl.multiple_of(j * bq, bq), bq)]
        s = s_ref[...] if online else score_tile(j)
        if masked:
            key = lax.broadcasted_iota(jnp.int32, s.shape, 0)
            qry = lax.broadcasted_iota(jnp.int32, s.shape, 1)
            s = jnp.where(key <= jnp.where(qry >= bq, qry - bq, qry), s, NEG_BIG)
        if online:
            m_old = m_ref[...]
            m_new = jnp.maximum(m_old, jnp.max(s, axis=0, keepdims=True))
            alpha = jnp.exp2(m_old - m_new)
            p = jnp.exp2(s - m_new)
            l_ref[...] = alpha * l_ref[...] + jnp.sum(p, axis=0, keepdims=True)
            m_ref[...] = m_new
            acc_ref[...] = alpha * acc_ref[...] + _dot(vb, p.astype(BF16))
        else:
            p = jnp.exp2(s - m_ref[...])
            l_ref[...] += jnp.sum(p, axis=0, keepdims=True)
            acc_ref[...] += _dot(vb, p.astype(BF16))

    bufs = (sa_ref, sb_ref)
    scores(0, sa_ref)

    def group(g, _):
        for t in range(unroll):
            j = unroll * g + t
            scores(j + 1, bufs[(t + 1) % 2])
            softmax_pv(j, bufs[t % 2], masked=False)
        return 0

    lax.fori_loop(0, i // unroll, group, 0)
    j0 = (i // unroll) * unroll

    for r in range(unroll):
        @pl.when(i - j0 == r)
        def _(r=r):
            for t in range(r + 1):
                if t < r:
                    scores(j0 + t + 1, bufs[(t + 1) % 2])
                softmax_pv(j0 + t, bufs[t % 2], masked=(t == r))

    l = l_ref[...]
    lam_vecs = lam_ref[...]
    lam = (jnp.exp(jnp.sum(lam_vecs[0:1] * lam_vecs[1:2], axis=-1, keepdims=True))
           - jnp.exp(jnp.sum(lam_vecs[2:3] * lam_vecs[3:4], axis=-1, keepdims=True))
           + lambda_init)
    o = acc_ref[:, 0:bq] / l[:, 0:bq] - lam * (acc_ref[:, bq:] / l[:, bq:])
    o = o * lax.rsqrt(jnp.mean(o * o, axis=0, keepdims=True) + NORM_EPS)
    o = o * (gsub_ref[...] * (1.0 - lambda_init))
    o_ref[...] = o.T.astype(BF16)


def _diff_attention(qk, vt, lam_vecs, gsub_col, lambda_init, bq):
    s = qk.shape[0]
    hd2 = 2 * HEAD_DIM
    sq = jnp.square(qk[:, :2 * N_HEADS * hd2].astype(F32)).reshape(s, 2, N_HEADS, 2, HEAD_DIM)
    norms = jnp.sqrt(jnp.max(jnp.sum(sq, axis=-1), axis=0))
    fits = jnp.all(norms[0] * norms[1] * BOUND_SLACK <= MAX_FIXED_SHIFT)
    knorm = jnp.repeat(norms[1], bq, axis=1)[:, None, :]

    def call(online):
        return _diff_attention_call(qk, vt, knorm, lam_vecs, gsub_col, lambda_init, bq, online)

    return lax.cond(fits, lambda: call(False), lambda: call(True))


def _diff_attention_call(qk, vt, knorm, lam_vecs, gsub_col, lambda_init, bq, online):
    s = qk.shape[0]
    hd2 = 2 * HEAD_DIM
    return pl.pallas_call(
        functools.partial(_attn_kernel, bq=bq, lambda_init=lambda_init, unroll=ATTN_UNROLL,
                          online=online),
        grid=(N_HEADS, s // bq),
        in_specs=[
            pl.BlockSpec((bq, hd2), lambda h, i: (i, h)),
            pl.BlockSpec((s, hd2), lambda h, i: (0, N_HEADS + h)),
            pl.BlockSpec((hd2, s), lambda h, i: (h, 0)),
            pl.BlockSpec((1, 1, 2 * bq), lambda h, i: (h, 0, 0)),
            pl.BlockSpec((4, HEAD_DIM), lambda h, i: (0, 0)),
            pl.BlockSpec((hd2, 1), lambda h, i: (0, 0)),
        ],
        out_specs=pl.BlockSpec((bq, hd2), lambda h, i: (i, h)),
        out_shape=jax.ShapeDtypeStruct((s, N_HEADS * hd2), BF16),
        scratch_shapes=[pltpu.VMEM((2 * bq, hd2), BF16), pltpu.VMEM((hd2, 2 * bq), F32),
                        pltpu.VMEM((1, 2 * bq), F32), pltpu.VMEM((1, 2 * bq), F32),
                        pltpu.VMEM((bq, 2 * bq), F32), pltpu.VMEM((bq, 2 * bq), F32)],
        compiler_params=_params("parallel", "arbitrary"),
        name="diff_attn_online" if online else "diff_attn",
    )(qk, qk, vt, knorm, lam_vecs, gsub_col)


def _proj_res_kernel(x_ref, a_ref, w_ref, o_ref, wb_ref):
    @pl.when(pl.program_id(0) == 0)
    def _():
        wb_ref[...] = w_ref[...].astype(BF16)

    o_ref[...] = x_ref[...] + _dot(a_ref[...], wb_ref[...])


def _proj_residual(x, a, w, tm):
    s, d = x.shape
    k = a.shape[1]
    return pl.pallas_call(
        _proj_res_kernel,
        grid=(s // tm,),
        in_specs=[
            pl.BlockSpec((tm, d), lambda i: (i, 0)),
            pl.BlockSpec((tm, k), lambda i: (i, 0)),
            pl.BlockSpec((k, d), lambda i: (0, 0)),
        ],
        out_specs=pl.BlockSpec((tm, d), lambda i: (i, 0)),
        out_shape=jax.ShapeDtypeStruct((s, d), F32),
        scratch_shapes=[pltpu.VMEM((k, d), BF16)],
        compiler_params=_params("arbitrary"),
        name="attn_out_proj",
    )(x, a, w)


def _swiglu_kernel(te_ref, nt_ref, x_ref, g_ref, wg_ref, wu_ref, wd_ref, o_ref, h_ref, acc_ref,
                   *, add_residual):
    t = pl.program_id(0)
    f = pl.program_id(1)

    @pl.when(t < nt_ref[0])
    def _():
        @pl.when(f == 0)
        def _():
            h_ref[...] = _rms(x_ref[...], g_ref[...]).astype(BF16)
            acc_ref[...] = jnp.zeros(acc_ref.shape, F32)

        h = h_ref[...]
        gate = _dot(h, wg_ref[0].astype(BF16))
        up = _dot(h, wu_ref[0].astype(BF16))
        act = (gate * jax.nn.sigmoid(gate) * up).astype(BF16)
        acc_ref[...] += _dot(act, wd_ref[0].astype(BF16))

        @pl.when(f == pl.num_programs(1) - 1)
        def _():
            if add_residual:
                o_ref[...] = x_ref[...] + acc_ref[...]
            else:
                o_ref[...] = acc_ref[...]

    @pl.when((t >= nt_ref[0]) & (f == 0))
    def _():
        o_ref[...] = jnp.zeros(o_ref.shape, F32)


def _grouped_swiglu(x, g, w_gate, w_up, w_down, tile_expert, n_tiles, tm, tf, add_residual):
    m, d = x.shape
    dff = w_gate.shape[2]
    nf = dff // tf
    max_tiles = tile_expert.shape[0]

    def _last(nt):
        return jnp.maximum(nt[0] - 1, 0)

    def row_map(t, f, te, nt):
        return (jnp.minimum(t, _last(nt)), 0)

    def _tf(t, f, nt):
        return jnp.where(t < nt[0], f, nf - 1)

    def _te(t, te, nt):
        return te[jnp.minimum(t, _last(nt))]

    return pl.pallas_call(
        functools.partial(_swiglu_kernel, add_residual=add_residual),
        grid_spec=pltpu.PrefetchScalarGridSpec(
            num_scalar_prefetch=2,
            grid=(max_tiles, nf),
            in_specs=[
                pl.BlockSpec((tm, d), row_map),
                pl.BlockSpec((1, d), lambda t, f, te, nt: (0, 0)),
                pl.BlockSpec((1, d, tf), lambda t, f, te, nt: (_te(t, te, nt), 0, _tf(t, f, nt))),
                pl.BlockSpec((1, d, tf), lambda t, f, te, nt: (_te(t, te, nt), 0, _tf(t, f, nt))),
                pl.BlockSpec((1, tf, d), lambda t, f, te, nt: (_te(t, te, nt), _tf(t, f, nt), 0)),
            ],
            out_specs=pl.BlockSpec((tm, d), lambda t, f, te, nt: (t, 0)),
            scratch_shapes=[pltpu.VMEM((tm, d), BF16), pltpu.VMEM((tm, d), F32)],
        ),
        out_shape=jax.ShapeDtypeStruct((m, d), F32),
        compiler_params=_params("arbitrary", "arbitrary"),
        name="swiglu_residual" if add_residual else "expert_swiglu",
    )(tile_expert, n_tiles, x, g, w_gate, w_up, w_down)


def _ple_tail(x, p_ref, wp_ref, wg_ref, g_e_ref, g_gate_ref, o_ref):
    e = _rms(_dot(p_ref[...].astype(BF16), wp_ref[...]), g_e_ref[...])
    gate = jax.nn.sigmoid(_dot(_rms(x, g_gate_ref[...]).astype(BF16), wg_ref[...]))
    o_ref[...] = x + gate * e


def _ple_kernel(x_ref, p_ref, wp_ref, wg_ref, g_e_ref, g_gate_ref, o_ref, wpb_ref, wgb_ref):
    @pl.when(pl.program_id(0) == 0)
    def _():
        wpb_ref[...] = wp_ref[...].astype(BF16)
        wgb_ref[...] = wg_ref[...].astype(BF16)

    _ple_tail(x_ref[...], p_ref, wpb_ref, wgb_ref, g_e_ref, g_gate_ref, o_ref)


def _combine_ple_kernel(pos_ref, x_ref, w_ref, ys_ref, p_ref, wp_ref, wg_ref, g_e_ref, g_gate_ref,
                        o_ref, wpb_ref, wgb_ref, ybuf_ref, sem, *, tc):
    @pl.when(pl.program_id(0) == 0)
    def _():
        wpb_ref[...] = wp_ref[...].astype(BF16)
        wgb_ref[...] = wg_ref[...].astype(BF16)

    def row_copy(k, t):
        return pltpu.make_async_copy(ys_ref.at[pl.ds(pos_ref[k, t], 1), :],
                                     ybuf_ref.at[k, pl.ds(t, 1), :], sem)

    def issue(t, _):
        row_copy(0, t).start(priority=0)
        row_copy(1, t).start(priority=1)
        return 0

    lax.fori_loop(0, tc, issue, 0, unroll=8)
    for k in range(2):
        pltpu.make_async_copy(ys_ref.at[pl.ds(0, tc), :], ybuf_ref.at[k], sem).wait()
    w = w_ref[...]
    x = x_ref[...] + w[:, 0:1] * ybuf_ref[0] + w[:, 1:2] * ybuf_ref[1]
    _ple_tail(x, p_ref, wpb_ref, wgb_ref, g_e_ref, g_gate_ref, o_ref)


def _ple(x, p, wp, wg, g_e, g_gate, tm, routed=None):
    s, d = x.shape
    pd = p.shape[1]
    row = lambda i: (i, 0)
    fixed = lambda i: (0, 0)
    tail_specs = [
        pl.BlockSpec((tm, pd), row),
        pl.BlockSpec((pd, d), fixed),
        pl.BlockSpec((d, d), fixed),
        pl.BlockSpec((1, d), fixed),
        pl.BlockSpec((1, d), fixed),
    ]
    scratch = [pltpu.VMEM((pd, d), BF16), pltpu.VMEM((d, d), BF16)]
    if routed is None:
        return pl.pallas_call(
            _ple_kernel,
            grid=(s // tm,),
            in_specs=[pl.BlockSpec((tm, d), row)] + tail_specs,
            out_specs=pl.BlockSpec((tm, d), row),
            out_shape=jax.ShapeDtypeStruct((s, d), F32),
            scratch_shapes=scratch,
            compiler_params=_params("arbitrary"),
            name="ple",
        )(x, p, wp, wg, g_e, g_gate)
    pos, w12, ys = routed
    return pl.pallas_call(
        functools.partial(_combine_ple_kernel, tc=tm),
        grid=(s // tm,),
        in_specs=[
            pl.BlockSpec((2, tm), lambda i: (0, i), memory_space=pltpu.SMEM),
            pl.BlockSpec((tm, d), row),
            pl.BlockSpec((tm, 2), row),
            pl.BlockSpec(memory_space=pl.ANY),
        ] + tail_specs,
        out_specs=pl.BlockSpec((tm, d), row),
        out_shape=jax.ShapeDtypeStruct((s, d), F32),
        scratch_shapes=scratch + [pltpu.VMEM((2, tm, d), F32), pltpu.SemaphoreType.DMA(())],
        compiler_params=_params("arbitrary"),
        name="combine_ple",
    )(pos, x, w12, ys, p, wp, wg, g_e, g_gate)


def _glu_kernel(x_ref, g_ref, wa_ref, wg_ref, ba_ref, bg_ref, o_ref, h_ref):
    @pl.when(pl.program_id(1) == 0)
    def _():
        h_ref[...] = _rms(x_ref[...], g_ref[...]).astype(BF16)

    h = h_ref[...]
    a = _dot(h, wa_ref[...].astype(BF16)) + ba_ref[...]
    gate = _dot(h, wg_ref[...].astype(BF16)) + bg_ref[...]
    o_ref[...] = a * jax.nn.sigmoid(gate)


def _pointwise_glu(x, g, w, b, tm, tn):
    s, d = x.shape
    n = w.shape[1] // 2
    nj = n // tn
    return pl.pallas_call(
        _glu_kernel,
        grid=(s // tm, nj),
        in_specs=[
            pl.BlockSpec((tm, d), lambda i, j: (i, 0)),
            pl.BlockSpec((1, d), lambda i, j: (0, 0)),
            pl.BlockSpec((d, tn), lambda i, j: (0, j)),
            pl.BlockSpec((d, tn), lambda i, j: (0, j + nj)),
            pl.BlockSpec((1, tn), lambda i, j: (0, j)),
            pl.BlockSpec((1, tn), lambda i, j: (0, j + nj)),
        ],
        out_specs=pl.BlockSpec((tm, tn), lambda i, j: (i, j)),
        out_shape=jax.ShapeDtypeStruct((s, n), F32),
        scratch_shapes=[pltpu.VMEM((tm, d), BF16)],
        compiler_params=_params("parallel", "arbitrary"),
        name="conv_pw1_glu",
    )(x, g, w, w, b, b)


def _conv_kernel(x_ref, u_ref, halo_ref, wdw_ref, bdw_ref, lng_ref, lnb_ref, w2_ref, b2_ref,
                 o_ref, ext_ref, c_ref, w2b_ref, *, tm, rc):
    i = pl.program_id(0)

    @pl.when(i == 0)
    def _():
        w2b_ref[...] = w2_ref[...].astype(BF16)

    halo = halo_ref[...]
    ext_ref[0:HALO_ROWS, :] = jnp.where(i > 0, halo, jnp.zeros_like(halo))
    ext_ref[HALO_ROWS:, :] = u_ref[...]
    first = HALO_ROWS - (CONV_WIDTH - 1)

    win_rows = rc + HALO_ROWS

    def chunk(r, _):
        base = pl.multiple_of(r * rc, rc)
        for lo in range(0, u_ref.shape[1], LANES):
            cols = slice(lo, lo + LANES)
            win = ext_ref[pl.ds(base, win_rows), cols]
            acc = jnp.zeros((rc, LANES), F32) + bdw_ref[:, cols]
            for b in range(SUBLANES):
                wb = win if b == 0 else pltpu.roll(win, shift=win_rows - b, axis=0)
                for a in range(win_rows // SUBLANES):
                    t = SUBLANES * a + b - first
                    if 0 <= t < CONV_WIDTH:
                        acc = acc + wdw_ref[t:t + 1, cols] * wb[SUBLANES * a:SUBLANES * a + rc, :]
            c_ref[pl.ds(base, rc), cols] = acc
        return 0

    lax.fori_loop(0, tm // rc, chunk, 0)
    c = c_ref[...]
    mu = jnp.mean(c, axis=-1, keepdims=True)
    cc = c - mu
    y = cc * lax.rsqrt(jnp.mean(cc * cc, axis=-1, keepdims=True) + LN_EPS) * lng_ref[...] + lnb_ref[...]
    y = y * jax.nn.sigmoid(y)
    o_ref[...] = x_ref[...] + _dot(y.astype(BF16), w2b_ref[...]) + b2_ref[...]


def _conv_module_tail(x, u, w_dw, b_dw, ln_g, ln_b, w2, b2, tm):
    s, d = x.shape
    row = lambda i: (i, 0)
    fixed = lambda i: (0, 0)
    per = tm // HALO_ROWS
    return pl.pallas_call(
        functools.partial(_conv_kernel, tm=tm, rc=min(64, tm)),
        grid=(s // tm,),
        in_specs=[
            pl.BlockSpec((tm, d), row),
            pl.BlockSpec((tm, d), row),
            pl.BlockSpec((HALO_ROWS, d), lambda i: (jnp.maximum(i * per - 1, 0), 0)),
            pl.BlockSpec((CONV_WIDTH, d), fixed),
            pl.BlockSpec((1, d), fixed),
            pl.BlockSpec((1, d), fixed),
            pl.BlockSpec((1, d), fixed),
            pl.BlockSpec((d, d), fixed),
            pl.BlockSpec((1, d), fixed),
        ],
        out_specs=pl.BlockSpec((tm, d), row),
        out_shape=jax.ShapeDtypeStruct((s, d), F32),
        scratch_shapes=[pltpu.VMEM((tm + HALO_ROWS, d), F32), pltpu.VMEM((tm, d), F32),
                        pltpu.VMEM((d, d), BF16)],
        compiler_params=_params("arbitrary"),
        name="conv_ln_pw2",
    )(x, u, u, w_dw, b_dw, ln_g, ln_b, w2, b2)


def _router_kernel(x_ref, g_ref, wrt_ref, b_ref, info_ref, cnt_ref, tri_ref, carry_ref, *, tm):
    i = pl.program_id(0)

    @pl.when(i == 0)
    def _():
        a = lax.broadcasted_iota(jnp.int32, (tm, tm), 0)
        b = lax.broadcasted_iota(jnp.int32, (tm, tm), 1)
        tri_ref[...] = (a < b).astype(BF16)
        carry_ref[...] = jnp.zeros(carry_ref.shape, F32)

    h = _rms(x_ref[...], g_ref[...])
    logits = lax.dot_general(wrt_ref[...], h, (((1,), (1,)), ((), ())),
                             preferred_element_type=F32,
                             precision=lax.Precision.HIGHEST) + b_ref[...]
    eidx = lax.broadcasted_iota(jnp.int32, logits.shape, 0)
    m1 = jnp.max(logits, axis=0, keepdims=True)
    e1 = jnp.min(jnp.where(logits == m1, eidx, N_EXPERTS), axis=0, keepdims=True)
    rest = jnp.where(eidx == e1, -jnp.inf, logits)
    m2 = jnp.max(rest, axis=0, keepdims=True)
    e2 = jnp.min(jnp.where(rest == m2, eidx, N_EXPERTS), axis=0, keepdims=True)
    z = jnp.exp(m2 - m1)
    w1 = 1.0 / (1.0 + z)
    w2 = z / (1.0 + z)
    sel1 = eidx == e1
    sel2 = eidx == e2
    sel = (sel1 | sel2).astype(F32)
    rank = carry_ref[:, 0:1] + _dot(sel.astype(BF16), tri_ref[...])
    r1 = jnp.sum(jnp.where(sel1, rank, 0.0), axis=0, keepdims=True)
    r2 = jnp.sum(jnp.where(sel2, rank, 0.0), axis=0, keepdims=True)
    rows = (e1.astype(F32), e2.astype(F32), w1, w2, r1, r2)
    info = jnp.zeros(logits.shape, F32)
    for k, v in enumerate(rows):
        info = jnp.where(eidx == k, v, info)
    info_ref[...] = info
    carry_ref[...] = carry_ref[...] + jnp.sum(sel, axis=1, keepdims=True)
    cnt_ref[...] = carry_ref[...]


def _router(x, g, w_router_t, b_col, tm):
    s, d = x.shape
    return pl.pallas_call(
        functools.partial(_router_kernel, tm=tm),
        grid=(s // tm,),
        in_specs=[
            pl.BlockSpec((tm, d), lambda i: (i, 0)),
            pl.BlockSpec((1, d), lambda i: (0, 0)),
            pl.BlockSpec((N_EXPERTS, d), lambda i: (0, 0)),
            pl.BlockSpec((N_EXPERTS, 1), lambda i: (0, 0)),
        ],
        out_specs=[
            pl.BlockSpec((N_EXPERTS, tm), lambda i: (0, i)),
            pl.BlockSpec((N_EXPERTS, 128), lambda i: (0, 0)),
        ],
        out_shape=[jax.ShapeDtypeStruct((N_EXPERTS, s), F32),
                   jax.ShapeDtypeStruct((N_EXPERTS, 128), F32)],
        scratch_shapes=[pltpu.VMEM((tm, tm), BF16), pltpu.VMEM((N_EXPERTS, 128), F32)],
        compiler_params=_params("arbitrary"),
        name="router_top2",
    )(x, g, w_router_t, b_col)


def _dispatch_kernel(pos_ref, x_ref, xs_in_ref, xs_ref, sem, *, td):
    del xs_in_ref

    def row_copy(k, t):
        return pltpu.make_async_copy(x_ref.at[pl.ds(t, 1), :],
                                     xs_ref.at[pl.ds(pos_ref[k, t], 1), :], sem)

    def issue(t, _):
        row_copy(0, t).start(priority=0)
        row_copy(1, t).start(priority=1)
        return 0

    lax.fori_loop(0, td, issue, 0, unroll=8)
    for _ in range(2):
        pltpu.make_async_copy(x_ref, xs_ref.at[pl.ds(0, td), :], sem).wait()


def _dispatch(x, pos, m_pad, td):
    s, d = x.shape
    return pl.pallas_call(
        functools.partial(_dispatch_kernel, td=td),
        grid=(s // td,),
        in_specs=[
            pl.BlockSpec((2, td), lambda i: (0, i), memory_space=pltpu.SMEM),
            pl.BlockSpec((td, d), lambda i: (i, 0)),
            pl.BlockSpec(memory_space=pl.ANY),
        ],
        out_specs=pl.BlockSpec(memory_space=pl.ANY),
        out_shape=jax.ShapeDtypeStruct((m_pad, d), F32),
        scratch_shapes=[pltpu.SemaphoreType.DMA(())],
        input_output_aliases={2: 0},
        compiler_params=_params("arbitrary"),
        name="moe_dispatch",
    )(pos, x, jnp.zeros((m_pad, d), F32))


def _tiles(s):
    t = lambda want: min(want, s)
    return dict(qkv=(t(1024), 512), attn=t(512), proj=t(512), ffn=(t(1024), 512), ple=t(512),
                glu=(t(1024), 512), conv=t(512), router=t(512), move=t(512), expert=(t(1024), 512))


def _layer_even(x, p, tiles, mix_g, ffn_g, w_qkv, w_o, g_q, g_k, lam_vecs, g_sub, lambda_init,
                w_gate, w_up, w_down, ple_wp, ple_g, ple_gate_g, ple_wg):
    s, d = x.shape
    ones = jnp.ones((N_HEADS * 2 * HEAD_DIM,), F32)
    gain_row = jnp.concatenate([jnp.tile(g_q.reshape(-1), N_HEADS) * (ATTN_SCALE * LOG2_E),
                                jnp.tile(g_k.reshape(-1), N_HEADS), ones])[None, :]
    qkv = _qkv_proj(x, mix_g[None, :], w_qkv, gain_row, *tiles["qkv"])
    vt = qkv[:, 2 * d:].T
    o = _diff_attention(qkv, vt, lam_vecs, g_sub[:, None], lambda_init, tiles["attn"])
    x = _proj_residual(x, o, w_o, tiles["proj"])
    tm, tf = tiles["ffn"]
    n_tiles = s // tm
    x = _grouped_swiglu(x, ffn_g[None, :], w_gate[None], w_up[None], w_down[None],
                        jnp.zeros((n_tiles,), jnp.int32), jnp.full((1,), n_tiles, jnp.int32),
                        tm, tf, add_residual=True)
    return _ple(x, p, ple_wp, ple_wg, ple_g[None, :], ple_gate_g[None, :], tiles["ple"])


def _layer_odd(x, p, tiles, mix_g, ffn_g, w_pw1, b_pw1, w_dw, b_dw, ln_g, ln_b, w_pw2, b_pw2,
               w_router, b_router, w_gate, w_up, w_down, ple_wp, ple_g, ple_gate_g, ple_wg):
    s, d = x.shape
    u = _pointwise_glu(x, mix_g[None, :], w_pw1, b_pw1[None, :], *tiles["glu"])
    x = _conv_module_tail(x, u, w_dw, b_dw[None, :], ln_g[None, :], ln_b[None, :], w_pw2,
                          b_pw2[None, :], tiles["conv"])
    info, counts = _router(x, ffn_g[None, :], w_router.T, b_router[:, None], tiles["router"])
    tm, tf = tiles["expert"]
    e12 = info[0:2].astype(jnp.int32)
    w12 = info[2:4].T
    r12 = info[4:6].astype(jnp.int32)
    cnt = counts[:, 0].astype(jnp.int32)
    group_tiles = (cnt + tm - 1) // tm
    tile_end = jnp.cumsum(group_tiles)
    offsets = (tile_end - group_tiles) * tm
    expert_ids = jnp.arange(N_EXPERTS, dtype=jnp.int32)
    onehot = e12[:, :, None] == expert_ids[None, None, :]
    pos = jnp.sum(jnp.where(onehot, offsets[None, None, :], 0), axis=-1) + r12
    max_tiles = (2 * s) // tm + N_EXPERTS
    tile_ids = jnp.arange(max_tiles, dtype=jnp.int32)
    tile_expert = jnp.minimum(jnp.sum((tile_end[None, :] <= tile_ids[:, None]).astype(jnp.int32), axis=1),
                              N_EXPERTS - 1)
    n_tiles = tile_end[-1:].astype(jnp.int32)
    xs = _dispatch(x, pos, max_tiles * tm, tiles["move"])
    ys = _grouped_swiglu(xs, ffn_g[None, :], w_gate, w_up, w_down, tile_expert, n_tiles,
                         tm, tf, add_residual=False)
    return _ple(x, p, ple_wp, ple_wg, ple_g[None, :], ple_gate_g[None, :], tiles["move"],
                routed=(pos, w12, ys))


def kernel(x, p, mix_norm_g, ffn_norm_g, attn_w_qkv, attn_w_o, attn_g_q, attn_g_k, attn_lam_q1, attn_lam_k1, attn_lam_q2, attn_lam_k2, attn_g_sub, conv_w_pw1, conv_b_pw1, conv_w_dw, conv_b_dw, conv_ln_g, conv_ln_b, conv_w_pw2, conv_b_pw2, ffn_w_gate, ffn_w_up, ffn_w_down, moe_w_router, moe_b_router, moe_w_gate, moe_w_up, moe_w_down, ple_w_proj, ple_norm_g, ple_gate_norm_g, ple_w_gate):
    depth, batch, s, _ = p.shape
    tiles = _tiles(s)
    outs = []
    for b in range(batch):
        xb = x[b]
        for i in range(depth):
            j = i // 2
            ple_args = (ple_w_proj[i], ple_norm_g[i], ple_gate_norm_g[i], ple_w_gate[i])
            if i % 2 == 0:
                lambda_init = 0.8 - 0.6 * math.exp(-0.3 * i)
                lam_vecs = jnp.stack([attn_lam_q1[j], attn_lam_k1[j], attn_lam_q2[j], attn_lam_k2[j]])
                xb = _layer_even(xb, p[i, b], tiles, mix_norm_g[i], ffn_norm_g[i], attn_w_qkv[j],
                                 attn_w_o[j], attn_g_q[j], attn_g_k[j], lam_vecs, attn_g_sub[j],
                                 lambda_init, ffn_w_gate[j], ffn_w_up[j], ffn_w_down[j], *ple_args)
            else:
                xb = _layer_odd(xb, p[i, b], tiles, mix_norm_g[i], ffn_norm_g[i], conv_w_pw1[j],
                                conv_b_pw1[j], conv_w_dw[j], conv_b_dw[j], conv_ln_g[j], conv_ln_b[j],
                                conv_w_pw2[j], conv_b_pw2[j], moe_w_router[j], moe_b_router[j],
                                moe_w_gate[j], moe_w_up[j], moe_w_down[j], *ple_args)
        outs.append(xb)
    return jnp.stack(outs)
```

```python
import functools
import math

import jax
import jax.numpy as jnp
from jax import lax
from jax.experimental import pallas as pl
from jax.experimental.pallas import tpu as pltpu

N_HEADS = 8
HEAD_DIM = 64
N_EXPERTS = 8
CONV_WIDTH = 31
NORM_EPS = 1e-6
LN_EPS = 1e-5
ATTN_SCALE = HEAD_DIM ** -0.5
LOG2_E = math.log2(math.e)

BF16 = jnp.bfloat16
F32 = jnp.float32
NEG_BIG = -1e30
VMEM_LIMIT_BYTES = 56 * 1024 * 1024
SUBLANES, LANES = 8, 128
ATTN_UNROLL = 4
BOUND_SLACK = 1.01
MAX_FIXED_SHIFT = 40.0
HALO_ROWS = 32


def _params(*sem):
    return pltpu.CompilerParams(dimension_semantics=sem, vmem_limit_bytes=VMEM_LIMIT_BYTES)


def _rms(xf, g):
    return xf * lax.rsqrt(jnp.mean(xf * xf, axis=-1, keepdims=True) + NORM_EPS) * g


def _dot(a, b):
    return jnp.dot(a, b, preferred_element_type=F32)


def _dot_nt(a, b):
    return lax.dot_general(a, b, (((1,), (1,)), ((), ())), preferred_element_type=F32)


def _qkv_kernel(x_ref, g_ref, w_ref, gain_ref, bd_ref, qk_ref, vt_ref, nrm_ref, h_ref, *, n_qk_tiles):
    j = pl.program_id(1)

    @pl.when(j == 0)
    def _():
        h_ref[...] = _rms(x_ref[...], g_ref[...]).astype(BF16)

    y = _dot(h_ref[...], w_ref[...].astype(BF16))

    @pl.when(j < n_qk_tiles)
    def _():
        ss = _dot((y * y).astype(BF16), bd_ref[...])
        yn = (y * lax.rsqrt(ss * (1.0 / HEAD_DIM) + NORM_EPS) * gain_ref[...]).astype(BF16)
        qk_ref[...] = yn
        ynf = yn.astype(F32)
        row_norm2 = _dot((ynf * ynf).astype(BF16), bd_ref[...])
        nrm_ref[...] = jnp.broadcast_to(jnp.max(row_norm2, axis=0, keepdims=True), nrm_ref.shape)

    @pl.when(j >= n_qk_tiles)
    def _():
        vt_ref[...] = y.T.astype(BF16)
        nrm_ref[...] = jnp.zeros(nrm_ref.shape, F32)


def _qkv_proj(x, g, w, gain_row, tm, tn):
    s, d = x.shape
    n = w.shape[1]
    n_qk = 2 * n // 3
    n_qk_tiles = n_qk // tn
    lane = jnp.arange(tn) // HEAD_DIM
    bd = (lane[:, None] == lane[None, :]).astype(BF16)
    return pl.pallas_call(
        functools.partial(_qkv_kernel, n_qk_tiles=n_qk_tiles),
        grid=(s // tm, n // tn),
        in_specs=[
            pl.BlockSpec((tm, d), lambda i, j: (i, 0)),
            pl.BlockSpec((1, d), lambda i, j: (0, 0)),
            pl.BlockSpec((d, tn), lambda i, j: (0, j)),
            pl.BlockSpec((1, tn), lambda i, j: (0, j)),
            pl.BlockSpec((tn, tn), lambda i, j: (0, 0)),
        ],
        out_specs=[
            pl.BlockSpec((tm, tn), lambda i, j: (i, jnp.minimum(j, n_qk_tiles - 1))),
            pl.BlockSpec((tn, tm), lambda i, j: (jnp.maximum(j - n_qk_tiles, 0), i)),
            pl.BlockSpec((SUBLANES, tn), lambda i, j: (i, j)),
        ],
        out_shape=[jax.ShapeDtypeStruct((s, n_qk), BF16),
                   jax.ShapeDtypeStruct((n - n_qk, s), BF16),
                   jax.ShapeDtypeStruct((s // tm * SUBLANES, n), F32)],
        scratch_shapes=[pltpu.VMEM((tm, d), BF16)],
        compiler_params=_params("parallel", "arbitrary"),
        name="qkv_proj",
    )(x, g, w, gain_row, bd)


def _attn_kernel(q_ref, k_ref, vt_ref, knorm_ref, lam_ref, gsub_ref, o_ref, qq_ref, acc_ref, m_ref,
                 l_ref, sa_ref, sb_ref, *, bq, lambda_init, unroll, online):
    i = pl.program_id(1)
    q = q_ref[...]
    lane = lax.broadcasted_iota(jnp.int32, q.shape, 1)
    qq_ref[0:bq, :] = jnp.where(lane < HEAD_DIM, q, 0)
    qq_ref[bq:, :] = jnp.where(lane >= HEAD_DIM, q, 0)
    acc_ref[...] = jnp.zeros(acc_ref.shape, F32)
    l_ref[...] = jnp.zeros(l_ref.shape, F32)
    if online:
        m_ref[...] = jnp.full(m_ref.shape, NEG_BIG, F32)
    else:
        qq = qq_ref[...].astype(F32)
        q_norm2 = _dot_nt(jnp.ones((SUBLANES, 2 * HEAD_DIM), BF16), (qq * qq).astype(BF16))[0:1]
        m_ref[...] = jnp.sqrt(q_norm2) * knorm_ref[0] * BOUND_SLACK

    def score_tile(j):
        kb = k_ref[pl.ds(pl.multiple_of(j * bq, bq), bq), :]
        return _dot_nt(kb, qq_ref[...])

    def scores(j, s_ref):
        if online:
            s_ref[...] = score_tile(j)

    def softmax_pv(j, s_ref, masked):
        vb = vt_ref[:, pl.ds(pl.multiple_of(j * bq, bq), bq)]
        s = s_ref[...] if online else score_tile(j)
        if masked:
            key = lax.broadcasted_iota(jnp.int32, s.shape, 0)
            qry = lax.broadcasted_iota(jnp.int32, s.shape, 1)
            s = jnp.where(key <= jnp.where(qry >= bq, qry - bq, qry), s, NEG_BIG)
        if online:
            m_old = m_ref[...]
            m_new = jnp.maximum(m_old, jnp.max(s, axis=0, keepdims=True))
            alpha = jnp.exp2(m_old - m_new)
            p = jnp.exp2(s - m_new)
            l_ref[...] = alpha * l_ref[...] + jnp.sum(p, axis=0, keepdims=True)
            m_ref[...] = m_new
            acc_ref[...] = alpha * acc_ref[...] + _dot(vb, p.astype(BF16))
        else:
            p = jnp.exp2(s - m_ref[...])
            l_ref[...] += jnp.sum(p, axis=0, keepdims=True)
            acc_ref[...] += _dot(vb, p.astype(BF16))

    bufs = (sa_ref, sb_ref)
    scores(0, sa_ref)

    def group(g, _):
        for t in range(unroll):
            j = unroll * g + t
            scores(j + 1, bufs[(t + 1) % 2])
            softmax_pv(j, bufs[t % 2], masked=False)
        return 0

    lax.fori_loop(0, i // unroll, group, 0)
    j0 = (i // unroll) * unroll

    for r in range(unroll):
        @pl.when(i - j0 == r)
        def _(r=r):
            for t in range(r + 1):
                if t < r:
                    scores(j0 + t + 1, bufs[(t + 1) % 2])
                softmax_pv(j0 + t, bufs[t % 2], masked=(t == r))

    l = l_ref[...]
    lam_vecs = lam_ref[...]
    lam = (jnp.exp(jnp.sum(lam_vecs[0:1] * lam_vecs[1:2], axis=-1, keepdims=True))
           - jnp.exp(jnp.sum(lam_vecs[2:3] * lam_vecs[3:4], axis=-1, keepdims=True))
           + lambda_init)
    o = acc_ref[:, 0:bq] / l[:, 0:bq] - lam * (acc_ref[:, bq:] / l[:, bq:])
    o = o * lax.rsqrt(jnp.mean(o * o, axis=0, keepdims=True) + NORM_EPS)
    o = o * (gsub_ref[...] * (1.0 - lambda_init))
    o_ref[...] = o.T.astype(BF16)


def _diff_attention(qk, vt, norm2_tiles, lam_vecs, gsub_col, lambda_init, bq):
    norm2 = jnp.max(norm2_tiles[:, :qk.shape[1]], axis=0)[::HEAD_DIM]
    norms = jnp.sqrt(norm2).reshape(2, N_HEADS, 2)
    fits = jnp.all(norms[0] * norms[1] * BOUND_SLACK <= MAX_FIXED_SHIFT)
    knorm = jnp.repeat(norms[1], bq, axis=1)[:, None, :]

    def call(online):
        return _diff_attention_call(qk, vt, knorm, lam_vecs, gsub_col, lambda_init, bq, online)

    return lax.cond(fits, lambda: call(False), lambda: call(True))


def _diff_attention_call(qk, vt, knorm, lam_vecs, gsub_col, lambda_init, bq, online):
    s = qk.shape[0]
    hd2 = 2 * HEAD_DIM
    return pl.pallas_call(
        functools.partial(_attn_kernel, bq=bq, lambda_init=lambda_init, unroll=ATTN_UNROLL,
                          online=online),
        grid=(N_HEADS, s // bq),
        in_specs=[
            pl.BlockSpec((bq, hd2), lambda h, i: (i, h)),
            pl.BlockSpec((s, hd2), lambda h, i: (0, N_HEADS + h)),
            pl.BlockSpec((hd2, s), lambda h, i: (h, 0)),
            pl.BlockSpec((1, 1, 2 * bq), lambda h, i: (h, 0, 0)),
            pl.BlockSpec((4, HEAD_DIM), lambda h, i: (0, 0)),
            pl.BlockSpec((hd2, 1), lambda h, i: (0, 0)),
        ],
        out_specs=pl.BlockSpec((bq, hd2), lambda h, i: (i, h)),
        out_shape=jax.ShapeDtypeStruct((s, N_HEADS * hd2), BF16),
        scratch_shapes=[pltpu.VMEM((2 * bq, hd2), BF16), pltpu.VMEM((hd2, 2 * bq), F32),
                        pltpu.VMEM((1, 2 * bq), F32), pltpu.VMEM((1, 2 * bq), F32),
                        pltpu.VMEM((bq, 2 * bq), F32), pltpu.VMEM((bq, 2 * bq), F32)],
        compiler_params=_params("parallel", "arbitrary"),
        name="diff_attn_online" if online else "diff_attn",
    )(qk, qk, vt, knorm, lam_vecs, gsub_col)


def _proj_res_kernel(x_ref, a_ref, w_ref, o_ref, wb_ref):
    @pl.when(pl.program_id(0) == 0)
    def _():
        wb_ref[...] = w_ref[...].astype(BF16)

    o_ref[...] = x_ref[...] + _dot(a_ref[...], wb_ref[...])


def _proj_residual(x, a, w, tm):
    s, d = x.shape
    k = a.shape[1]
    return pl.pallas_call(
        _proj_res_kernel,
        grid=(s // tm,),
        in_specs=[
            pl.BlockSpec((tm, d), lambda i: (i, 0)),
            pl.BlockSpec((tm, k), lambda i: (i, 0)),
            pl.BlockSpec((k, d), lambda i: (0, 0)),
        ],
        out_specs=pl.BlockSpec((tm, d), lambda i: (i, 0)),
        out_shape=jax.ShapeDtypeStruct((s, d), F32),
        scratch_shapes=[pltpu.VMEM((k, d), BF16)],
        compiler_params=_params("arbitrary"),
        name="attn_out_proj",
    )(x, a, w)


def _swiglu_kernel(te_ref, nt_ref, x_ref, g_ref, wg_ref, wu_ref, wd_ref, o_ref, h_ref, acc_ref,
                   *, add_residual):
    t = pl.program_id(0)
    f = pl.program_id(1)

    @pl.when(t < nt_ref[0])
    def _():
        @pl.when(f == 0)
        def _():
            h_ref[...] = _rms(x_ref[...], g_ref[...]).astype(BF16)
            acc_ref[...] = jnp.zeros(acc_ref.shape, F32)

        h = h_ref[...]
        gate = _dot(h, wg_ref[0].astype(BF16))
        up = _dot(h, wu_ref[0].astype(BF16))
        act = (gate * jax.nn.sigmoid(gate) * up).astype(BF16)
        acc_ref[...] += _dot(act, wd_ref[0].astype(BF16))

        @pl.when(f == pl.num_programs(1) - 1)
        def _():
            if add_residual:
                o_ref[...] = x_ref[...] + acc_ref[...]
            else:
                o_ref[...] = acc_ref[...]

    @pl.when((t >= nt_ref[0]) & (f == 0))
    def _():
        o_ref[...] = jnp.zeros(o_ref.shape, F32)


def _grouped_swiglu(x, g, w_gate, w_up, w_down, tile_expert, n_tiles, tm, tf, add_residual):
    m, d = x.shape
    dff = w_gate.shape[2]
    nf = dff // tf
    max_tiles = tile_expert.shape[0]

    def _last(nt):
        return jnp.maximum(nt[0] - 1, 0)

    def row_map(t, f, te, nt):
        return (jnp.minimum(t, _last(nt)), 0)

    def _tf(t, f, nt):
        return jnp.where(t < nt[0], f, nf - 1)

    def _te(t, te, nt):
        return te[jnp.minimum(t, _last(nt))]

    return pl.pallas_call(
        functools.partial(_swiglu_kernel, add_residual=add_residual),
        grid_spec=pltpu.PrefetchScalarGridSpec(
            num_scalar_prefetch=2,
            grid=(max_tiles, nf),
            in_specs=[
                pl.BlockSpec((tm, d), row_map),
                pl.BlockSpec((1, d), lambda t, f, te, nt: (0, 0)),
                pl.BlockSpec((1, d, tf), lambda t, f, te, nt: (_te(t, te, nt), 0, _tf(t, f, nt))),
                pl.BlockSpec((1, d, tf), lambda t, f, te, nt: (_te(t, te, nt), 0, _tf(t, f, nt))),
                pl.BlockSpec((1, tf, d), lambda t, f, te, nt: (_te(t, te, nt), _tf(t, f, nt), 0)),
            ],
            out_specs=pl.BlockSpec((tm, d), lambda t, f, te, nt: (t, 0)),
            scratch_shapes=[pltpu.VMEM((tm, d), BF16), pltpu.VMEM((tm, d), F32)],
        ),
        out_shape=jax.ShapeDtypeStruct((m, d), F32),
        compiler_params=_params("arbitrary", "arbitrary"),
        name="swiglu_residual" if add_residual else "expert_swiglu",
    )(tile_expert, n_tiles, x, g, w_gate, w_up, w_down)


def _ple_tail(x, p_ref, wp_ref, wg_ref, g_e_ref, g_gate_ref, o_ref):
    e = _rms(_dot(p_ref[...].astype(BF16), wp_ref[...]), g_e_ref[...])
    gate = jax.nn.sigmoid(_dot(_rms(x, g_gate_ref[...]).astype(BF16), wg_ref[...]))
    o_ref[...] = x + gate * e


def _ple_kernel(x_ref, p_ref, wp_ref, wg_ref, g_e_ref, g_gate_ref, o_ref, wpb_ref, wgb_ref):
    @pl.when(pl.program_id(0) == 0)
    def _():
        wpb_ref[...] = wp_ref[...].astype(BF16)
        wgb_ref[...] = wg_ref[...].astype(BF16)

    _ple_tail(x_ref[...], p_ref, wpb_ref, wgb_ref, g_e_ref, g_gate_ref, o_ref)


def _combine_ple_kernel(pos_ref, x_ref, w_ref, ys_ref, p_ref, wp_ref, wg_ref, g_e_ref, g_gate_ref,
                        o_ref, wpb_ref, wgb_ref, ybuf_ref, sem, *, tc):
    @pl.when(pl.program_id(0) == 0)
    def _():
        wpb_ref[...] = wp_ref[...].astype(BF16)
        wgb_ref[...] = wg_ref[...].astype(BF16)

    def row_copy(k, t):
        return pltpu.make_async_copy(ys_ref.at[pl.ds(pos_ref[k, t], 1), :],
                                     ybuf_ref.at[k, pl.ds(t, 1), :], sem)

    def issue(t, _):
        row_copy(0, t).start(priority=0)
        row_copy(1, t).start(priority=1)
        return 0

    lax.fori_loop(0, tc, issue, 0, unroll=8)
    for k in range(2):
        pltpu.make_async_copy(ys_ref.at[pl.ds(0, tc), :], ybuf_ref.at[k], sem).wait()
    w = w_ref[...]
    x = x_ref[...] + w[:, 0:1] * ybuf_ref[0] + w[:, 1:2] * ybuf_ref[1]
    _ple_tail(x, p_ref, wpb_ref, wgb_ref, g_e_ref, g_gate_ref, o_ref)


def _ple(x, p, wp, wg, g_e, g_gate, tm, routed=None):
    s, d = x.shape
    pd = p.shape[1]
    row = lambda i: (i, 0)
    fixed = lambda i: (0, 0)
    tail_specs = [
        pl.BlockSpec((tm, pd), row),
        pl.BlockSpec((pd, d), fixed),
        pl.BlockSpec((d, d), fixed),
        pl.BlockSpec((1, d), fixed),
        pl.BlockSpec((1, d), fixed),
    ]
    scratch = [pltpu.VMEM((pd, d), BF16), pltpu.VMEM((d, d), BF16)]
    if routed is None:
        return pl.pallas_call(
            _ple_kernel,
            grid=(s // tm,),
            in_specs=[pl.BlockSpec((tm, d), row)] + tail_specs,
            out_specs=pl.BlockSpec((tm, d), row),
            out_shape=jax.ShapeDtypeStruct((s, d), F32),
            scratch_shapes=scratch,
            compiler_params=_params("arbitrary"),
            name="ple",
        )(x, p, wp, wg, g_e, g_gate)
    pos, w12, ys = routed
    return pl.pallas_call(
        functools.partial(_combine_ple_kernel, tc=tm),
        grid=(s // tm,),
        in_specs=[
            pl.BlockSpec((2, tm), lambda i: (0, i), memory_space=pltpu.SMEM),
            pl.BlockSpec((tm, d), row),
            pl.BlockSpec((tm, 2), row),
            pl.BlockSpec(memory_space=pl.ANY),
        ] + tail_specs,
        out_specs=pl.BlockSpec((tm, d), row),
        out_shape=jax.ShapeDtypeStruct((s, d), F32),
        scratch_shapes=scratch + [pltpu.VMEM((2, tm, d), F32), pltpu.SemaphoreType.DMA(())],
        compiler_params=_params("arbitrary"),
        name="combine_ple",
    )(pos, x, w12, ys, p, wp, wg, g_e, g_gate)


def _glu_kernel(x_ref, g_ref, wa_ref, wg_ref, ba_ref, bg_ref, o_ref, h_ref):
    @pl.when(pl.program_id(1) == 0)
    def _():
        h_ref[...] = _rms(x_ref[...], g_ref[...]).astype(BF16)

    h = h_ref[...]
    a = _dot(h, wa_ref[...].astype(BF16)) + ba_ref[...]
    gate = _dot(h, wg_ref[...].astype(BF16)) + bg_ref[...]
    o_ref[...] = a * jax.nn.sigmoid(gate)


def _pointwise_glu(x, g, w, b, tm, tn):
    s, d = x.shape
    n = w.shape[1] // 2
    nj = n // tn
    return pl.pallas_call(
        _glu_kernel,
        grid=(s // tm, nj),
        in_specs=[
            pl.BlockSpec((tm, d), lambda i, j: (i, 0)),
            pl.BlockSpec((1, d), lambda i, j: (0, 0)),
            pl.BlockSpec((d, tn), lambda i, j: (0, j)),
            pl.BlockSpec((d, tn), lambda i, j: (0, j + nj)),
            pl.BlockSpec((1, tn), lambda i, j: (0, j)),
            pl.BlockSpec((1, tn), lambda i, j: (0, j + nj)),
        ],
        out_specs=pl.BlockSpec((tm, tn), lambda i, j: (i, j)),
        out_shape=jax.ShapeDtypeStruct((s, n), F32),
        scratch_shapes=[pltpu.VMEM((tm, d), BF16)],
        compiler_params=_params("parallel", "arbitrary"),
        name="conv_pw1_glu",
    )(x, g, w, w, b, b)


def _conv_kernel(x_ref, u_ref, halo_ref, wdw_ref, bdw_ref, lng_ref, lnb_ref, w2_ref, b2_ref,
                 o_ref, ext_ref, c_ref, w2b_ref, *, tm, rc):
    i = pl.program_id(0)

    @pl.when(i == 0)
    def _():
        w2b_ref[...] = w2_ref[...].astype(BF16)

    halo = halo_ref[...]
    ext_ref[0:HALO_ROWS, :] = jnp.where(i > 0, halo, jnp.zeros_like(halo))
    ext_ref[HALO_ROWS:, :] = u_ref[...]
    first = HALO_ROWS - (CONV_WIDTH - 1)

    win_rows = rc + HALO_ROWS

    def chunk(r, _):
        base = pl.multiple_of(r * rc, rc)
        for lo in range(0, u_ref.shape[1], LANES):
            cols = slice(lo, lo + LANES)
            win = ext_ref[pl.ds(base, win_rows), cols]
            acc = jnp.zeros((rc, LANES), F32) + bdw_ref[:, cols]
            for b in range(SUBLANES):
                wb = win if b == 0 else pltpu.roll(win, shift=win_rows - b, axis=0)
                for a in range(win_rows // SUBLANES):
                    t = SUBLANES * a + b - first
                    if 0 <= t < CONV_WIDTH:
                        acc = acc + wdw_ref[t:t + 1, cols] * wb[SUBLANES * a:SUBLANES * a + rc, :]
            c_ref[pl.ds(base, rc), cols] = acc
        return 0

    lax.fori_loop(0, tm // rc, chunk, 0)
    c = c_ref[...]
    mu = jnp.mean(c, axis=-1, keepdims=True)
    cc = c - mu
    y = cc * lax.rsqrt(jnp.mean(cc * cc, axis=-1, keepdims=True) + LN_EPS) * lng_ref[...] + lnb_ref[...]
    y = y * jax.nn.sigmoid(y)
    o_ref[...] = x_ref[...] + _dot(y.astype(BF16), w2b_ref[...]) + b2_ref[...]


def _conv_module_tail(x, u, w_dw, b_dw, ln_g, ln_b, w2, b2, tm):
    s, d = x.shape
    row = lambda i: (i, 0)
    fixed = lambda i: (0, 0)
    per = tm // HALO_ROWS
    return pl.pallas_call(
        functools.partial(_conv_kernel, tm=tm, rc=min(64, tm)),
        grid=(s // tm,),
        in_specs=[
            pl.BlockSpec((tm, d), row),
            pl.BlockSpec((tm, d), row),
            pl.BlockSpec((HALO_ROWS, d), lambda i: (jnp.maximum(i * per - 1, 0), 0)),
            pl.BlockSpec((CONV_WIDTH, d), fixed),
            pl.BlockSpec((1, d), fixed),
            pl.BlockSpec((1, d), fixed),
            pl.BlockSpec((1, d), fixed),
            pl.BlockSpec((d, d), fixed),
            pl.BlockSpec((1, d), fixed),
        ],
        out_specs=pl.BlockSpec((tm, d), row),
        out_shape=jax.ShapeDtypeStruct((s, d), F32),
        scratch_shapes=[pltpu.VMEM((tm + HALO_ROWS, d), F32), pltpu.VMEM((tm, d), F32),
                        pltpu.VMEM((d, d), BF16)],
        compiler_params=_params("arbitrary"),
        name="conv_ln_pw2",
    )(x, u, u, w_dw, b_dw, ln_g, ln_b, w2, b2)


def _router_kernel(x_ref, g_ref, wrt_ref, b_ref, info_ref, cnt_ref, tri_ref, carry_ref, *, tm):
    i = pl.program_id(0)

    @pl.when(i == 0)
    def _():
        a = lax.broadcasted_iota(jnp.int32, (tm, tm), 0)
        b = lax.broadcasted_iota(jnp.int32, (tm, tm), 1)
        tri_ref[...] = (a < b).astype(BF16)
        carry_ref[...] = jnp.zeros(carry_ref.shape, F32)

    h = _rms(x_ref[...], g_ref[...])
    logits = lax.dot_general(wrt_ref[...], h, (((1,), (1,)), ((), ())),
                             preferred_element_type=F32,
                             precision=lax.Precision.HIGHEST) + b_ref[...]
    eidx = lax.broadcasted_iota(jnp.int32, logits.shape, 0)
    m1 = jnp.max(logits, axis=0, keepdims=True)
    e1 = jnp.min(jnp.where(logits == m1, eidx, N_EXPERTS), axis=0, keepdims=True)
    rest = jnp.where(eidx == e1, -jnp.inf, logits)
    m2 = jnp.max(rest, axis=0, keepdims=True)
    e2 = jnp.min(jnp.where(rest == m2, eidx, N_EXPERTS), axis=0, keepdims=True)
    z = jnp.exp(m2 - m1)
    w1 = 1.0 / (1.0 + z)
    w2 = z / (1.0 + z)
    sel1 = eidx == e1
    sel2 = eidx == e2
    sel = (sel1 | sel2).astype(F32)
    rank = carry_ref[:, 0:1] + _dot(sel.astype(BF16), tri_ref[...])
    r1 = jnp.sum(jnp.where(sel1, rank, 0.0), axis=0, keepdims=True)
    r2 = jnp.sum(jnp.where(sel2, rank, 0.0), axis=0, keepdims=True)
    rows = (e1.astype(F32), e2.astype(F32), w1, w2, r1, r2)
    info = jnp.zeros(logits.shape, F32)
    for k, v in enumerate(rows):
        info = jnp.where(eidx == k, v, info)
    info_ref[...] = info
    carry_ref[...] = carry_ref[...] + jnp.sum(sel, axis=1, keepdims=True)
    cnt_ref[...] = carry_ref[...]


def _router(x, g, w_router_t, b_col, tm):
    s, d = x.shape
    return pl.pallas_call(
        functools.partial(_router_kernel, tm=tm),
        grid=(s // tm,),
        in_specs=[
            pl.BlockSpec((tm, d), lambda i: (i, 0)),
            pl.BlockSpec((1, d), lambda i: (0, 0)),
            pl.BlockSpec((N_EXPERTS, d), lambda i: (0, 0)),
            pl.BlockSpec((N_EXPERTS, 1), lambda i: (0, 0)),
        ],
        out_specs=[
            pl.BlockSpec((N_EXPERTS, tm), lambda i: (0, i)),
            pl.BlockSpec((N_EXPERTS, 128), lambda i: (0, 0)),
        ],
        out_shape=[jax.ShapeDtypeStruct((N_EXPERTS, s), F32),
                   jax.ShapeDtypeStruct((N_EXPERTS, 128), F32)],
        scratch_shapes=[pltpu.VMEM((tm, tm), BF16), pltpu.VMEM((N_EXPERTS, 128), F32)],
        compiler_params=_params("arbitrary"),
        name="router_top2",
    )(x, g, w_router_t, b_col)


def _dispatch_kernel(pos_ref, x_ref, xs_in_ref, xs_ref, sem, *, td):
    del xs_in_ref

    def row_copy(k, t):
        return pltpu.make_async_copy(x_ref.at[pl.ds(t, 1), :],
                                     xs_ref.at[pl.ds(pos_ref[k, t], 1), :], sem)

    def issue(t, _):
        row_copy(0, t).start(priority=0)
        row_copy(1, t).start(priority=1)
        return 0

    lax.fori_loop(0, td, issue, 0, unroll=8)
    for _ in range(2):
        pltpu.make_async_copy(x_ref, xs_ref.at[pl.ds(0, td), :], sem).wait()


def _dispatch(x, pos, m_pad, td):
    s, d = x.shape
    return pl.pallas_call(
        functools.partial(_dispatch_kernel, td=td),
        grid=(s // td,),
        in_specs=[
            pl.BlockSpec((2, td), lambda i: (0, i), memory_space=pltpu.SMEM),
            pl.BlockSpec((td, d), lambda i: (i, 0)),
            pl.BlockSpec(memory_space=pl.ANY),
        ],
        out_specs=pl.BlockSpec(memory_space=pl.ANY),
        out_shape=jax.ShapeDtypeStruct((m_pad, d), F32),
        scratch_shapes=[pltpu.SemaphoreType.DMA(())],
        input_output_aliases={2: 0},
        compiler_params=_params("arbitrary"),
        name="moe_dispatch",
    )(pos, x, jnp.zeros((m_pad, d), F32))


def _tiles(s):
    t = lambda want: min(want, s)
    return dict(qkv=(t(1024), 512), attn=t(512), proj=t(512), ffn=(t(1024), 512), ple=t(512),
                glu=(t(1024), 512), conv=t(512), router=t(512), move=t(512), expert=(t(1024), 512))


def _layer_even(x, p, tiles, mix_g, ffn_g, w_qkv, w_o, g_q, g_k, lam_vecs, g_sub, lambda_init,
                w_gate, w_up, w_down, ple_wp, ple_g, ple_gate_g, ple_wg):
    s, d = x.shape
    ones = jnp.ones((N_HEADS * 2 * HEAD_DIM,), F32)
    gain_row = jnp.concatenate([jnp.tile(g_q.reshape(-1), N_HEADS) * (ATTN_SCALE * LOG2_E),
                                jnp.tile(g_k.reshape(-1), N_HEADS), ones])[None, :]
    qk, vt, norm2_tiles = _qkv_proj(x, mix_g[None, :], w_qkv, gain_row, *tiles["qkv"])
    o = _diff_attention(qk, vt, norm2_tiles, lam_vecs, g_sub[:, None], lambda_init, tiles["attn"])
    x = _proj_residual(x, o, w_o, tiles["proj"])
    tm, tf = tiles["ffn"]
    n_tiles = s // tm
    x = _grouped_swiglu(x, ffn_g[None, :], w_gate[None], w_up[None], w_down[None],
                        jnp.zeros((n_tiles,), jnp.int32), jnp.full((1,), n_tiles, jnp.int32),
                        tm, tf, add_residual=True)
    return _ple(x, p, ple_wp, ple_wg, ple_g[None, :], ple_gate_g[None, :], tiles["ple"])


def _layer_odd(x, p, tiles, mix_g, ffn_g, w_pw1, b_pw1, w_dw, b_dw, ln_g, ln_b, w_pw2, b_pw2,
               w_router, b_router, w_gate, w_up, w_down, ple_wp, ple_g, ple_gate_g, ple_wg):
    s, d = x.shape
    u = _pointwise_glu(x, mix_g[None, :], w_pw1, b_pw1[None, :], *tiles["glu"])
    x = _conv_module_tail(x, u, w_dw, b_dw[None, :], ln_g[None, :], ln_b[None, :], w_pw2,
                          b_pw2[None, :], tiles["conv"])
    info, counts = _router(x, ffn_g[None, :], w_router.T, b_router[:, None], tiles["router"])
    tm, tf = tiles["expert"]
    e12 = info[0:2].astype(jnp.int32)
    w12 = info[2:4].T
    r12 = info[4:6].astype(jnp.int32)
    cnt = counts[:, 0].astype(jnp.int32)
    group_tiles = (cnt + tm - 1) // tm
    tile_end = jnp.cumsum(group_tiles)
    offsets = (tile_end - group_tiles) * tm
    expert_ids = jnp.arange(N_EXPERTS, dtype=jnp.int32)
    onehot = e12[:, :, None] == expert_ids[None, None, :]
    pos = jnp.sum(jnp.where(onehot, offsets[None, None, :], 0), axis=-1) + r12
    max_tiles = (2 * s) // tm + N_EXPERTS
    tile_ids = jnp.arange(max_tiles, dtype=jnp.int32)
    tile_expert = jnp.minimum(jnp.sum((tile_end[None, :] <= tile_ids[:, None]).astype(jnp.int32), axis=1),
                              N_EXPERTS - 1)
    n_tiles = tile_end[-1:].astype(jnp.int32)
    xs = _dispatch(x, pos, max_tiles * tm, tiles["move"])
    ys = _grouped_swiglu(xs, ffn_g[None, :], w_gate, w_up, w_down, tile_expert, n_tiles,
                         tm, tf, add_residual=False)
    return _ple(x, p, ple_wp, ple_wg, ple_g[None, :], ple_gate_g[None, :], tiles["move"],
                routed=(pos, w12, ys))


def kernel(x, p, mix_norm_g, ffn_norm_g, attn_w_qkv, attn_w_o, attn_g_q, attn_g_k, attn_lam_q1, attn_lam_k1, attn_lam_q2, attn_lam_k2, attn_g_sub, conv_w_pw1, conv_b_pw1, conv_w_dw, conv_b_dw, conv_ln_g, conv_ln_b, conv_w_pw2, conv_b_pw2, ffn_w_gate, ffn_w_up, ffn_w_down, moe_w_router, moe_b_router, moe_w_gate, moe_w_up, moe_w_down, ple_w_proj, ple_norm_g, ple_gate_norm_g, ple_w_gate):
    depth, batch, s, _ = p.shape
    tiles = _tiles(s)
    outs = []
    for b in range(batch):
        xb = x[b]
        for i in range(depth):
            j = i // 2
            ple_args = (ple_w_proj[i], ple_norm_g[i], ple_gate_norm_g[i], ple_w_gate[i])
            if i % 2 == 0:
                lambda_init = 0.8 - 0.6 * math.exp(-0.3 * i)
                lam_vecs = jnp.stack([attn_lam_q1[j], attn_lam_k1[j], attn_lam_q2[j], attn_lam_k2[j]])
                xb = _layer_even(xb, p[i, b], tiles, mix_norm_g[i], ffn_norm_g[i], attn_w_qkv[j],
                                 attn_w_o[j], attn_g_q[j], attn_g_k[j], lam_vecs, attn_g_sub[j],
                                 lambda_init, ffn_w_gate[j], ffn_w_up[j], ffn_w_down[j], *ple_args)
            else:
                xb = _layer_odd(xb, p[i, b], tiles, mix_norm_g[i], ffn_norm_g[i], conv_w_pw1[j],
                                conv_b_pw1[j], conv_w_dw[j], conv_b_dw[j], conv_ln_g[j], conv_ln_b[j],
                                conv_w_pw2[j], conv_b_pw2[j], moe_w_router[j], moe_b_router[j],
                                moe_w_gate[j], moe_w_up[j], moe_w_down[j], *ple_args)
        outs.append(xb)
    return jnp.stack(outs)
```

```python
import functools
import math

import jax
import jax.numpy as jnp
from jax import lax
from jax.experimental import pallas as pl
from jax.experimental.pallas import tpu as pltpu

N_HEADS = 8
HEAD_DIM = 64
N_EXPERTS = 8
CONV_WIDTH = 31
NORM_EPS = 1e-6
LN_EPS = 1e-5
ATTN_SCALE = HEAD_DIM ** -0.5
LOG2_E = math.log2(math.e)

BF16 = jnp.bfloat16
F32 = jnp.float32
NEG_BIG = -1e30
VMEM_LIMIT_BYTES = 56 * 1024 * 1024
SUBLANES, LANES = 8, 128
ATTN_UNROLL = 4
BOUND_SLACK = 1.01
MAX_FIXED_SHIFT = 40.0
HALO_ROWS = 32


def _params(*sem):
    return pltpu.CompilerParams(dimension_semantics=sem, vmem_limit_bytes=VMEM_LIMIT_BYTES)


def _rms(xf, g):
    return xf * lax.rsqrt(jnp.mean(xf * xf, axis=-1, keepdims=True) + NORM_EPS) * g


def _dot(a, b):
    return jnp.dot(a, b, preferred_element_type=F32)


def _dot_nt(a, b):
    return lax.dot_general(a, b, (((1,), (1,)), ((), ())), preferred_element_type=F32)


def _qkv_kernel(x_ref, g_ref, w_ref, gain_ref, bd_ref, qk_ref, vt_ref, h_ref, *, n_qk_tiles):
    j = pl.program_id(1)

    @pl.when(j == 0)
    def _():
        h_ref[...] = _rms(x_ref[...], g_ref[...]).astype(BF16)

    y = _dot(h_ref[...], w_ref[...].astype(BF16))

    @pl.when(j < n_qk_tiles)
    def _():
        ss = _dot((y * y).astype(BF16), bd_ref[...])
        qk_ref[...] = (y * lax.rsqrt(ss * (1.0 / HEAD_DIM) + NORM_EPS) * gain_ref[...]).astype(BF16)

    @pl.when(j >= n_qk_tiles)
    def _():
        vt_ref[...] = y.T.astype(BF16)


def _qkv_proj(x, g, w, gain_row, tm, tn):
    s, d = x.shape
    n = w.shape[1]
    n_qk = 2 * n // 3
    n_qk_tiles = n_qk // tn
    lane = jnp.arange(tn) // HEAD_DIM
    bd = (lane[:, None] == lane[None, :]).astype(BF16)
    return pl.pallas_call(
        functools.partial(_qkv_kernel, n_qk_tiles=n_qk_tiles),
        grid=(s // tm, n // tn),
        in_specs=[
            pl.BlockSpec((tm, d), lambda i, j: (i, 0)),
            pl.BlockSpec((1, d), lambda i, j: (0, 0)),
            pl.BlockSpec((d, tn), lambda i, j: (0, j)),
            pl.BlockSpec((1, tn), lambda i, j: (0, jnp.minimum(j, n_qk_tiles - 1))),
            pl.BlockSpec((tn, tn), lambda i, j: (0, 0)),
        ],
        out_specs=[
            pl.BlockSpec((tm, tn), lambda i, j: (i, jnp.minimum(j, n_qk_tiles - 1))),
            pl.BlockSpec((tn, tm), lambda i, j: (jnp.maximum(j - n_qk_tiles, 0), i)),
        ],
        out_shape=[jax.ShapeDtypeStruct((s, n_qk), BF16),
                   jax.ShapeDtypeStruct((n - n_qk, s), BF16)],
        scratch_shapes=[pltpu.VMEM((tm, d), BF16)],
        compiler_params=_params("parallel", "arbitrary"),
        name="qkv_proj",
    )(x, g, w, gain_row, bd)


def _attn_kernel(q_ref, k_ref, vt_ref, knorm_ref, lam_ref, gsub_ref, o_ref, qq_ref, acc_ref, m_ref,
                 l_ref, sa_ref, sb_ref, *, bq, lambda_init, unroll, online):
    i = pl.program_id(1)
    q = q_ref[...]
    lane = lax.broadcasted_iota(jnp.int32, q.shape, 1)
    qq_ref[0:bq, :] = jnp.where(lane < HEAD_DIM, q, 0)
    qq_ref[bq:, :] = jnp.where(lane >= HEAD_DIM, q, 0)
    acc_ref[...] = jnp.zeros(acc_ref.shape, F32)
    l_ref[...] = jnp.zeros(l_ref.shape, F32)
    if online:
        m_ref[...] = jnp.full(m_ref.shape, NEG_BIG, F32)
    else:
        qq = qq_ref[...].astype(F32)
        q_norm2 = _dot_nt(jnp.ones((SUBLANES, 2 * HEAD_DIM), BF16), (qq * qq).astype(BF16))[0:1]
        m_ref[...] = jnp.sqrt(q_norm2) * knorm_ref[0] * BOUND_SLACK

    def score_tile(j):
        kb = k_ref[pl.ds(pl.multiple_of(j * bq, bq), bq), :]
        return _dot_nt(kb, qq_ref[...])

    def scores(j, s_ref):
        if online:
            s_ref[...] = score_tile(j)

    def softmax_pv(j, s_ref, masked):
        vb = vt_ref[:, pl.ds(pl.multiple_of(j * bq, bq), bq)]
        s = s_ref[...] if online else score_tile(j)
        if masked:
            key = lax.broadcasted_iota(jnp.int32, s.shape, 0)
            qry = lax.broadcasted_iota(jnp.int32, s.shape, 1)
            s = jnp.where(key <= jnp.where(qry >= bq, qry - bq, qry), s, NEG_BIG)
        if online:
            m_old = m_ref[...]
            m_new = jnp.maximum(m_old, jnp.max(s, axis=0, keepdims=True))
            alpha = jnp.exp2(m_old - m_new)
            p = jnp.exp2(s - m_new)
            l_ref[...] = alpha * l_ref[...] + jnp.sum(p, axis=0, keepdims=True)
            m_ref[...] = m_new
            acc_ref[...] = alpha * acc_ref[...] + _dot(vb, p.astype(BF16))
        else:
            p = jnp.exp2(s - m_ref[...])
            l_ref[...] += jnp.sum(p, axis=0, keepdims=True)
            acc_ref[...] += _dot(vb, p.astype(BF16))

    bufs = (sa_ref, sb_ref)
    scores(0, sa_ref)

    def group(g, _):
        for t in range(unroll):
            j = unroll * g + t
            scores(j + 1, bufs[(t + 1) % 2])
            softmax_pv(j, bufs[t % 2], masked=False)
        return 0

    lax.fori_loop(0, i // unroll, group, 0)
    j0 = (i // unroll) * unroll

    for r in range(unroll):
        @pl.when(i - j0 == r)
        def _(r=r):
            for t in range(r + 1):
                if t < r:
                    scores(j0 + t + 1, bufs[(t + 1) % 2])
                softmax_pv(j0 + t, bufs[t % 2], masked=(t == r))

    l = l_ref[...]
    lam_vecs = lam_ref[...]
    lam = (jnp.exp(jnp.sum(lam_vecs[0:1] * lam_vecs[1:2], axis=-1, keepdims=True))
           - jnp.exp(jnp.sum(lam_vecs[2:3] * lam_vecs[3:4], axis=-1, keepdims=True))
           + lambda_init)
    o = acc_ref[:, 0:bq] / l[:, 0:bq] - lam * (acc_ref[:, bq:] / l[:, bq:])
    o = o * lax.rsqrt(jnp.mean(o * o, axis=0, keepdims=True) + NORM_EPS)
    o = o * (gsub_ref[...] * (1.0 - lambda_init))
    o_ref[...] = o.T.astype(BF16)


def _diff_attention(qk, vt, q_gain, k_gain, lam_vecs, gsub_col, lambda_init, bq):
    unit = math.sqrt(HEAD_DIM)
    q_norm = unit * jnp.max(jnp.abs(q_gain), axis=-1)
    k_norm = unit * jnp.max(jnp.abs(k_gain), axis=-1)
    fits = jnp.all(q_norm * k_norm * BOUND_SLACK <= MAX_FIXED_SHIFT)
    knorm = jnp.repeat(k_norm, bq)[None, None, :]

    def call(online):
        return _diff_attention_call(qk, vt, knorm, lam_vecs, gsub_col, lambda_init, bq, online)

    return lax.cond(fits, lambda: call(False), lambda: call(True))


def _diff_attention_call(qk, vt, knorm, lam_vecs, gsub_col, lambda_init, bq, online):
    s = qk.shape[0]
    hd2 = 2 * HEAD_DIM
    return pl.pallas_call(
        functools.partial(_attn_kernel, bq=bq, lambda_init=lambda_init, unroll=ATTN_UNROLL,
                          online=online),
        grid=(N_HEADS, s // bq),
        in_specs=[
            pl.BlockSpec((bq, hd2), lambda h, i: (i, h)),
            pl.BlockSpec((s, hd2), lambda h, i: (0, N_HEADS + h)),
            pl.BlockSpec((hd2, s), lambda h, i: (h, 0)),
            pl.BlockSpec((1, 1, 2 * bq), lambda h, i: (0, 0, 0)),
            pl.BlockSpec((4, HEAD_DIM), lambda h, i: (0, 0)),
            pl.BlockSpec((hd2, 1), lambda h, i: (0, 0)),
        ],
        out_specs=pl.BlockSpec((bq, hd2), lambda h, i: (i, h)),
        out_shape=jax.ShapeDtypeStruct((s, N_HEADS * hd2), BF16),
        scratch_shapes=[pltpu.VMEM((2 * bq, hd2), BF16), pltpu.VMEM((hd2, 2 * bq), F32),
                        pltpu.VMEM((1, 2 * bq), F32), pltpu.VMEM((1, 2 * bq), F32),
                        pltpu.VMEM((bq, 2 * bq), F32), pltpu.VMEM((bq, 2 * bq), F32)],
        compiler_params=_params("parallel", "arbitrary"),
        name="diff_attn_online" if online else "diff_attn",
    )(qk, qk, vt, knorm, lam_vecs, gsub_col)


def _proj_res_kernel(x_ref, a_ref, w_ref, o_ref, wb_ref):
    @pl.when(pl.program_id(0) == 0)
    def _():
        wb_ref[...] = w_ref[...].astype(BF16)

    o_ref[...] = x_ref[...] + _dot(a_ref[...], wb_ref[...])


def _proj_residual(x, a, w, tm):
    s, d = x.shape
    k = a.shape[1]
    return pl.pallas_call(
        _proj_res_kernel,
        grid=(s // tm,),
        in_specs=[
            pl.BlockSpec((tm, d), lambda i: (i, 0)),
            pl.BlockSpec((tm, k), lambda i: (i, 0)),
            pl.BlockSpec((k, d), lambda i: (0, 0)),
        ],
        out_specs=pl.BlockSpec((tm, d), lambda i: (i, 0)),
        out_shape=jax.ShapeDtypeStruct((s, d), F32),
        scratch_shapes=[pltpu.VMEM((k, d), BF16)],
        compiler_params=_params("arbitrary"),
        name="attn_out_proj",
    )(x, a, w)


def _swiglu_kernel(te_ref, nt_ref, x_ref, g_ref, wg_ref, wu_ref, wd_ref, o_ref, h_ref, acc_ref,
                   *, add_residual):
    t = pl.program_id(0)
    f = pl.program_id(1)

    @pl.when(t < nt_ref[0])
    def _():
        @pl.when(f == 0)
        def _():
            h_ref[...] = _rms(x_ref[...], g_ref[...]).astype(BF16)
            acc_ref[...] = jnp.zeros(acc_ref.shape, F32)

        h = h_ref[...]
        gate = _dot(h, wg_ref[0].astype(BF16))
        up = _dot(h, wu_ref[0].astype(BF16))
        act = (gate * jax.nn.sigmoid(gate) * up).astype(BF16)
        acc_ref[...] += _dot(act, wd_ref[0].astype(BF16))

        @pl.when(f == pl.num_programs(1) - 1)
        def _():
            if add_residual:
                o_ref[...] = x_ref[...] + acc_ref[...]
            else:
                o_ref[...] = acc_ref[...]

    @pl.when((t >= nt_ref[0]) & (f == 0))
    def _():
        o_ref[...] = jnp.zeros(o_ref.shape, F32)


def _grouped_swiglu(x, g, w_gate, w_up, w_down, tile_expert, n_tiles, tm, tf, add_residual):
    m, d = x.shape
    dff = w_gate.shape[2]
    nf = dff // tf
    max_tiles = tile_expert.shape[0]

    def _last(nt):
        return jnp.maximum(nt[0] - 1, 0)

    def row_map(t, f, te, nt):
        return (jnp.minimum(t, _last(nt)), 0)

    def _tf(t, f, nt):
        return jnp.where(t < nt[0], f, nf - 1)

    def _te(t, te, nt):
        return te[jnp.minimum(t, _last(nt))]

    return pl.pallas_call(
        functools.partial(_swiglu_kernel, add_residual=add_residual),
        grid_spec=pltpu.PrefetchScalarGridSpec(
            num_scalar_prefetch=2,
            grid=(max_tiles, nf),
            in_specs=[
                pl.BlockSpec((tm, d), row_map),
                pl.BlockSpec((1, d), lambda t, f, te, nt: (0, 0)),
                pl.BlockSpec((1, d, tf), lambda t, f, te, nt: (_te(t, te, nt), 0, _tf(t, f, nt))),
                pl.BlockSpec((1, d, tf), lambda t, f, te, nt: (_te(t, te, nt), 0, _tf(t, f, nt))),
                pl.BlockSpec((1, tf, d), lambda t, f, te, nt: (_te(t, te, nt), _tf(t, f, nt), 0)),
            ],
            out_specs=pl.BlockSpec((tm, d), lambda t, f, te, nt: (t, 0)),
            scratch_shapes=[pltpu.VMEM((tm, d), BF16), pltpu.VMEM((tm, d), F32)],
        ),
        out_shape=jax.ShapeDtypeStruct((m, d), F32),
        compiler_params=_params("arbitrary", "arbitrary"),
        name="swiglu_residual" if add_residual else "expert_swiglu",
    )(tile_expert, n_tiles, x, g, w_gate, w_up, w_down)


def _ple_tail(x, p_ref, wp_ref, wg_ref, g_e_ref, g_gate_ref, o_ref):
    e = _rms(_dot(p_ref[...].astype(BF16), wp_ref[...]), g_e_ref[...])
    gate = jax.nn.sigmoid(_dot(_rms(x, g_gate_ref[...]).astype(BF16), wg_ref[...]))
    o_ref[...] = x + gate * e


def _ple_kernel(x_ref, p_ref, wp_ref, wg_ref, g_e_ref, g_gate_ref, o_ref, wpb_ref, wgb_ref):
    @pl.when(pl.program_id(0) == 0)
    def _():
        wpb_ref[...] = wp_ref[...].astype(BF16)
        wgb_ref[...] = wg_ref[...].astype(BF16)

    _ple_tail(x_ref[...], p_ref, wpb_ref, wgb_ref, g_e_ref, g_gate_ref, o_ref)


def _combine_ple_kernel(pos_ref, x_ref, w_ref, ys_ref, p_ref, wp_ref, wg_ref, g_e_ref, g_gate_ref,
                        o_ref, wpb_ref, wgb_ref, ybuf_ref, sem, *, tc):
    @pl.when(pl.program_id(0) == 0)
    def _():
        wpb_ref[...] = wp_ref[...].astype(BF16)
        wgb_ref[...] = wg_ref[...].astype(BF16)

    def row_copy(k, t):
        return pltpu.make_async_copy(ys_ref.at[pl.ds(pos_ref[k, t], 1), :],
                                     ybuf_ref.at[k, pl.ds(t, 1), :], sem)

    def issue(t, _):
        row_copy(0, t).start(priority=0)
        row_copy(1, t).start(priority=1)
        return 0

    lax.fori_loop(0, tc, issue, 0, unroll=8)
    for k in range(2):
        pltpu.make_async_copy(ys_ref.at[pl.ds(0, tc), :], ybuf_ref.at[k], sem).wait()
    w = w_ref[...]
    x = x_ref[...] + w[:, 0:1] * ybuf_ref[0] + w[:, 1:2] * ybuf_ref[1]
    _ple_tail(x, p_ref, wpb_ref, wgb_ref, g_e_ref, g_gate_ref, o_ref)


def _ple(x, p, wp, wg, g_e, g_gate, tm, routed=None):
    s, d = x.shape
    pd = p.shape[1]
    row = lambda i: (i, 0)
    fixed = lambda i: (0, 0)
    tail_specs = [
        pl.BlockSpec((tm, pd), row),
        pl.BlockSpec((pd, d), fixed),
        pl.BlockSpec((d, d), fixed),
        pl.BlockSpec((1, d), fixed),
        pl.BlockSpec((1, d), fixed),
    ]
    scratch = [pltpu.VMEM((pd, d), BF16), pltpu.VMEM((d, d), BF16)]
    if routed is None:
        return pl.pallas_call(
            _ple_kernel,
            grid=(s // tm,),
            in_specs=[pl.BlockSpec((tm, d), row)] + tail_specs,
            out_specs=pl.BlockSpec((tm, d), row),
            out_shape=jax.ShapeDtypeStruct((s, d), F32),
            scratch_shapes=scratch,
            compiler_params=_params("arbitrary"),
            name="ple",
        )(x, p, wp, wg, g_e, g_gate)
    pos, w12, ys = routed
    return pl.pallas_call(
        functools.partial(_combine_ple_kernel, tc=tm),
        grid=(s // tm,),
        in_specs=[
            pl.BlockSpec((2, tm), lambda i: (0, i), memory_space=pltpu.SMEM),
            pl.BlockSpec((tm, d), row),
            pl.BlockSpec((tm, 2), row),
            pl.BlockSpec(memory_space=pl.ANY),
        ] + tail_specs,
        out_specs=pl.BlockSpec((tm, d), row),
        out_shape=jax.ShapeDtypeStruct((s, d), F32),
        scratch_shapes=scratch + [pltpu.VMEM((2, tm, d), F32), pltpu.SemaphoreType.DMA(())],
        compiler_params=_params("arbitrary"),
        name="combine_ple",
    )(pos, x, w12, ys, p, wp, wg, g_e, g_gate)


def _route(x, g_ref, wrt_ref, b_ref, tri_ref, carry_ref, info_ref, cnt_ref):
    h = _rms(x, g_ref[...])
    logits = lax.dot_general(wrt_ref[...], h, (((1,), (1,)), ((), ())),
                             preferred_element_type=F32,
                             precision=lax.Precision.HIGHEST) + b_ref[...]
    eidx = lax.broadcasted_iota(jnp.int32, logits.shape, 0)
    m1 = jnp.max(logits, axis=0, keepdims=True)
    e1 = jnp.min(jnp.where(logits == m1, eidx, N_EXPERTS), axis=0, keepdims=True)
    rest = jnp.where(eidx == e1, -jnp.inf, logits)
    m2 = jnp.max(rest, axis=0, keepdims=True)
    e2 = jnp.min(jnp.where(rest == m2, eidx, N_EXPERTS), axis=0, keepdims=True)
    z = jnp.exp(m2 - m1)
    w1 = 1.0 / (1.0 + z)
    w2 = z / (1.0 + z)
    sel1 = eidx == e1
    sel2 = eidx == e2
    sel = (sel1 | sel2).astype(F32)
    rank = carry_ref[:, 0:1] + _dot(sel.astype(BF16), tri_ref[...])
    r1 = jnp.sum(jnp.where(sel1, rank, 0.0), axis=0, keepdims=True)
    r2 = jnp.sum(jnp.where(sel2, rank, 0.0), axis=0, keepdims=True)
    rows = (e1.astype(F32), e2.astype(F32), w1, w2, r1, r2)
    info = jnp.zeros(logits.shape, F32)
    for k, v in enumerate(rows):
        info = jnp.where(eidx == k, v, info)
    info_ref[...] = info
    carry_ref[...] = carry_ref[...] + jnp.sum(sel, axis=1, keepdims=True)
    cnt_ref[...] = carry_ref[...]


def _conv_router_kernel(x_ref, g_ref, w1_ref, b1_ref, wdw_ref, bdw_ref, lng_ref, lnb_ref, w2_ref, b2_ref,
                        rg_ref, wrt_ref, rb_ref, o_ref, info_ref, cnt_ref,
                        ext_ref, c_ref, tri_ref, carry_ref, *, tm, rc):
    i = pl.program_id(0)
    d = x_ref.shape[1]

    @pl.when(i == 0)
    def _():
        ext_ref[0:HALO_ROWS, :] = jnp.zeros((HALO_ROWS, d), F32)
        a = lax.broadcasted_iota(jnp.int32, (tm, tm), 0)
        b = lax.broadcasted_iota(jnp.int32, (tm, tm), 1)
        tri_ref[...] = (a < b).astype(BF16)
        carry_ref[...] = jnp.zeros(carry_ref.shape, F32)

    @pl.when(i > 0)
    def _():
        ext_ref[0:HALO_ROWS, :] = ext_ref[tm:tm + HALO_ROWS, :]

    x = x_ref[...]
    y = _dot(_rms(x, g_ref[...]).astype(BF16), w1_ref[...]) + b1_ref[...]
    ext_ref[HALO_ROWS:, :] = y[:, :d] * jax.nn.sigmoid(y[:, d:])
    first = HALO_ROWS - (CONV_WIDTH - 1)
    win_rows = rc + HALO_ROWS

    def chunk(r, _):
        base = pl.multiple_of(r * rc, rc)
        for lo in range(0, d, LANES):
            cols = slice(lo, lo + LANES)
            win = ext_ref[pl.ds(base, win_rows), cols]
            acc = jnp.zeros((rc, LANES), F32) + bdw_ref[:, cols]
            for b in range(SUBLANES):
                wb = win if b == 0 else pltpu.roll(win, shift=win_rows - b, axis=0)
                for a in range(win_rows // SUBLANES):
                    t = SUBLANES * a + b - first
                    if 0 <= t < CONV_WIDTH:
                        acc = acc + wdw_ref[t:t + 1, cols] * wb[SUBLANES * a:SUBLANES * a + rc, :]
            c_ref[pl.ds(base, rc), cols] = acc
        return 0

    lax.fori_loop(0, tm // rc, chunk, 0)
    c = c_ref[...]
    mu = jnp.mean(c, axis=-1, keepdims=True)
    cc = c - mu
    y = cc * lax.rsqrt(jnp.mean(cc * cc, axis=-1, keepdims=True) + LN_EPS) * lng_ref[...] + lnb_ref[...]
    y = y * jax.nn.sigmoid(y)
    x1 = x + _dot(y.astype(BF16), w2_ref[...]) + b2_ref[...]
    o_ref[...] = x1
    _route(x1, rg_ref, wrt_ref, rb_ref, tri_ref, carry_ref, info_ref, cnt_ref)


def _conv_module_router(x, g, w1, b1, w_dw, b_dw, ln_g, ln_b, w2, b2, rg, w_router_t, rb_col, tm):
    s, d = x.shape
    row = lambda i: (i, 0)
    fixed = lambda i: (0, 0)
    return pl.pallas_call(
        functools.partial(_conv_router_kernel, tm=tm, rc=min(64, tm)),
        grid=(s // tm,),
        in_specs=[
            pl.BlockSpec((tm, d), row),
            pl.BlockSpec((1, d), fixed),
            pl.BlockSpec((d, 2 * d), fixed),
            pl.BlockSpec((1, 2 * d), fixed),
            pl.BlockSpec((CONV_WIDTH, d), fixed),
            pl.BlockSpec((1, d), fixed),
            pl.BlockSpec((1, d), fixed),
            pl.BlockSpec((1, d), fixed),
            pl.BlockSpec((d, d), fixed),
            pl.BlockSpec((1, d), fixed),
            pl.BlockSpec((1, d), fixed),
            pl.BlockSpec((N_EXPERTS, d), fixed),
            pl.BlockSpec((N_EXPERTS, 1), fixed),
        ],
        out_specs=[
            pl.BlockSpec((tm, d), row),
            pl.BlockSpec((N_EXPERTS, tm), lambda i: (0, i)),
            pl.BlockSpec((N_EXPERTS, LANES), fixed),
        ],
        out_shape=[jax.ShapeDtypeStruct((s, d), F32),
                   jax.ShapeDtypeStruct((N_EXPERTS, s), F32),
                   jax.ShapeDtypeStruct((N_EXPERTS, LANES), F32)],
        scratch_shapes=[pltpu.VMEM((tm + HALO_ROWS, d), F32), pltpu.VMEM((tm, d), F32),
                        pltpu.VMEM((tm, tm), BF16), pltpu.VMEM((N_EXPERTS, LANES), F32)],
        compiler_params=_params("arbitrary"),
        name="conv_module_router",
    )(x, g, w1, b1, w_dw, b_dw, ln_g, ln_b, w2, b2, rg, w_router_t, rb_col)


def _dispatch_kernel(pos_ref, x_ref, xs_in_ref, xs_ref, sem, *, td):
    del xs_in_ref

    def row_copy(k, t):
        return pltpu.make_async_copy(x_ref.at[pl.ds(t, 1), :],
                                     xs_ref.at[pl.ds(pos_ref[k, t], 1), :], sem)

    def issue(t, _):
        row_copy(0, t).start(priority=0)
        row_copy(1, t).start(priority=1)
        return 0

    lax.fori_loop(0, td, issue, 0, unroll=8)
    for _ in range(2):
        pltpu.make_async_copy(x_ref, xs_ref.at[pl.ds(0, td), :], sem).wait()


def _dispatch(x, pos, m_pad, td):
    s, d = x.shape
    return pl.pallas_call(
        functools.partial(_dispatch_kernel, td=td),
        grid=(s // td,),
        in_specs=[
            pl.BlockSpec((2, td), lambda i: (0, i), memory_space=pltpu.SMEM),
            pl.BlockSpec((td, d), lambda i: (i, 0)),
            pl.BlockSpec(memory_space=pl.ANY),
        ],
        out_specs=pl.BlockSpec(memory_space=pl.ANY),
        out_shape=jax.ShapeDtypeStruct((m_pad, d), F32),
        scratch_shapes=[pltpu.SemaphoreType.DMA(())],
        input_output_aliases={2: 0},
        compiler_params=_params("arbitrary"),
        name="moe_dispatch",
    )(pos, x, jnp.zeros((m_pad, d), F32))


def _tiles(s):
    t = lambda want: min(want, s)
    return dict(qkv=(t(1024), 512), attn=t(512), proj=t(512), ffn=(t(1024), 512), ple=t(512),
                conv=t(512), move=t(512), expert=(t(1024), 512))


def _layer_even(x, p, tiles, mix_g, ffn_g, w_qkv, w_o, g_q, g_k, lam_vecs, g_sub, lambda_init,
                w_gate, w_up, w_down, ple_wp, ple_g, ple_gate_g, ple_wg):
    s, d = x.shape
    q_gain = g_q * (ATTN_SCALE * LOG2_E)
    gain_row = jnp.concatenate([jnp.tile(q_gain.reshape(-1), N_HEADS),
                                jnp.tile(g_k.reshape(-1), N_HEADS)])[None, :]
    qk, vt = _qkv_proj(x, mix_g[None, :], w_qkv, gain_row, *tiles["qkv"])
    o = _diff_attention(qk, vt, q_gain, g_k, lam_vecs, g_sub[:, None], lambda_init, tiles["attn"])
    x = _proj_residual(x, o, w_o, tiles["proj"])
    tm, tf = tiles["ffn"]
    n_tiles = s // tm
    x = _grouped_swiglu(x, ffn_g[None, :], w_gate[None], w_up[None], w_down[None],
                        jnp.zeros((n_tiles,), jnp.int32), jnp.full((1,), n_tiles, jnp.int32),
                        tm, tf, add_residual=True)
    return _ple(x, p, ple_wp, ple_wg, ple_g[None, :], ple_gate_g[None, :], tiles["ple"])


def _layer_odd(x, p, tiles, mix_g, ffn_g, w_pw1, b_pw1, w_dw, b_dw, ln_g, ln_b, w_pw2, b_pw2,
               w_router, b_router, w_gate, w_up, w_down, ple_wp, ple_g, ple_gate_g, ple_wg):
    s, d = x.shape
    x, info, counts = _conv_module_router(
        x, mix_g[None, :], w_pw1.astype(BF16), b_pw1[None, :], w_dw, b_dw[None, :], ln_g[None, :],
        ln_b[None, :], w_pw2.astype(BF16), b_pw2[None, :], ffn_g[None, :], w_router.T,
        b_router[:, None], tiles["conv"])
    tm, tf = tiles["expert"]
    e12 = info[0:2].astype(jnp.int32)
    w12 = info[2:4].T
    r12 = info[4:6].astype(jnp.int32)
    cnt = counts[:, 0].astype(jnp.int32)
    group_tiles = (cnt + tm - 1) // tm
    tile_end = jnp.cumsum(group_tiles)
    offsets = (tile_end - group_tiles) * tm
    expert_ids = jnp.arange(N_EXPERTS, dtype=jnp.int32)
    onehot = e12[:, :, None] == expert_ids[None, None, :]
    pos = jnp.sum(jnp.where(onehot, offsets[None, None, :], 0), axis=-1) + r12
    max_tiles = (2 * s) // tm + N_EXPERTS
    tile_ids = jnp.arange(max_tiles, dtype=jnp.int32)
    tile_expert = jnp.minimum(jnp.sum((tile_end[None, :] <= tile_ids[:, None]).astype(jnp.int32), axis=1),
                              N_EXPERTS - 1)
    n_tiles = tile_end[-1:].astype(jnp.int32)
    xs = _dispatch(x, pos, max_tiles * tm, tiles["move"])
    ys = _grouped_swiglu(xs, ffn_g[None, :], w_gate, w_up, w_down, tile_expert, n_tiles,
                         tm, tf, add_residual=False)
    return _ple(x, p, ple_wp, ple_wg, ple_g[None, :], ple_gate_g[None, :], tiles["move"],
                routed=(pos, w12, ys))


def kernel(x, p, mix_norm_g, ffn_norm_g, attn_w_qkv, attn_w_o, attn_g_q, attn_g_k, attn_lam_q1, attn_lam_k1, attn_lam_q2, attn_lam_k2, attn_g_sub, conv_w_pw1, conv_b_pw1, conv_w_dw, conv_b_dw, conv_ln_g, conv_ln_b, conv_w_pw2, conv_b_pw2, ffn_w_gate, ffn_w_up, ffn_w_down, moe_w_router, moe_b_router, moe_w_gate, moe_w_up, moe_w_down, ple_w_proj, ple_norm_g, ple_gate_norm_g, ple_w_gate):
    depth, batch, s, _ = p.shape
    tiles = _tiles(s)
    outs = []
    for b in range(batch):
        xb = x[b]
        for i in range(depth):
            j = i // 2
            ple_args = (ple_w_proj[i], ple_norm_g[i], ple_gate_norm_g[i], ple_w_gate[i])
            if i % 2 == 0:
                lambda_init = 0.8 - 0.6 * math.exp(-0.3 * i)
                lam_vecs = jnp.stack([attn_lam_q1[j], attn_lam_k1[j], attn_lam_q2[j], attn_lam_k2[j]])
                xb = _layer_even(xb, p[i, b], tiles, mix_norm_g[i], ffn_norm_g[i], attn_w_qkv[j],
                                 attn_w_o[j], attn_g_q[j], attn_g_k[j], lam_vecs, attn_g_sub[j],
                                 lambda_init, ffn_w_gate[j], ffn_w_up[j], ffn_w_down[j], *ple_args)
            else:
                xb = _layer_odd(xb, p[i, b], tiles, mix_norm_g[i], ffn_norm_g[i], conv_w_pw1[j],
                                conv_b_pw1[j], conv_w_dw[j], conv_b_dw[j], conv_ln_g[j], conv_ln_b[j],
                                conv_w_pw2[j], conv_b_pw2[j], moe_w_router[j], moe_b_router[j],
                                moe_w_gate[j], moe_w_up[j], moe_w_down[j], *ple_args)
        outs.append(xb)
    return jnp.stack(outs)
```

```python
import functools
import math

import jax
import jax.numpy as jnp
from jax import lax
from jax.experimental import pallas as pl
from jax.experimental.pallas import tpu as pltpu

N_HEADS = 8
HEAD_DIM = 64
N_EXPERTS = 8
CONV_WIDTH = 31
NORM_EPS = 1e-6
LN_EPS = 1e-5
ATTN_SCALE = HEAD_DIM ** -0.5
LOG2_E = math.log2(math.e)

BF16 = jnp.bfloat16
F32 = jnp.float32
NEG_BIG = -1e30
VMEM_LIMIT_BYTES = 56 * 1024 * 1024
SUBLANES, LANES = 8, 128
ATTN_UNROLL = 4
BOUND_SLACK = 1.01
MAX_FIXED_SHIFT = 40.0
HALO_ROWS = 32


def _params(*sem):
    return pltpu.CompilerParams(dimension_semantics=sem, vmem_limit_bytes=VMEM_LIMIT_BYTES)


def _rms(xf, g):
    return xf * lax.rsqrt(jnp.mean(xf * xf, axis=-1, keepdims=True) + NORM_EPS) * g


def _dot(a, b):
    return jnp.dot(a, b, preferred_element_type=F32)


def _dot_nt(a, b):
    return lax.dot_general(a, b, (((1,), (1,)), ((), ())), preferred_element_type=F32)


def _qkv_kernel(x_ref, g_ref, w_ref, gain_ref, bd_ref, qk_ref, vt_ref, h_ref, *, n_qk_tiles):
    j = pl.program_id(1)

    @pl.when(j == 0)
    def _():
        h_ref[...] = _rms(x_ref[...], g_ref[...]).astype(BF16)

    y = _dot(h_ref[...], w_ref[...].astype(BF16))

    @pl.when(j < n_qk_tiles)
    def _():
        ss = _dot((y * y).astype(BF16), bd_ref[...])
        qk_ref[...] = (y * lax.rsqrt(ss * (1.0 / HEAD_DIM) + NORM_EPS) * gain_ref[...]).astype(BF16)

    @pl.when(j >= n_qk_tiles)
    def _():
        vt_ref[...] = y.T.astype(BF16)


def _qkv_proj(x, g, w, gain_row, tm, tn):
    s, d = x.shape
    n = w.shape[1]
    n_qk = 2 * n // 3
    n_qk_tiles = n_qk // tn
    lane = jnp.arange(tn) // HEAD_DIM
    bd = (lane[:, None] == lane[None, :]).astype(BF16)
    return pl.pallas_call(
        functools.partial(_qkv_kernel, n_qk_tiles=n_qk_tiles),
        grid=(s // tm, n // tn),
        in_specs=[
            pl.BlockSpec((tm, d), lambda i, j: (i, 0)),
            pl.BlockSpec((1, d), lambda i, j: (0, 0)),
            pl.BlockSpec((d, tn), lambda i, j: (0, j)),
            pl.BlockSpec((1, tn), lambda i, j: (0, jnp.minimum(j, n_qk_tiles - 1))),
            pl.BlockSpec((tn, tn), lambda i, j: (0, 0)),
        ],
        out_specs=[
            pl.BlockSpec((tm, tn), lambda i, j: (i, jnp.minimum(j, n_qk_tiles - 1))),
            pl.BlockSpec((tn, tm), lambda i, j: (jnp.maximum(j - n_qk_tiles, 0), i)),
        ],
        out_shape=[jax.ShapeDtypeStruct((s, n_qk), BF16),
                   jax.ShapeDtypeStruct((n - n_qk, s), BF16)],
        scratch_shapes=[pltpu.VMEM((tm, d), BF16)],
        compiler_params=_params("parallel", "arbitrary"),
        name="qkv_proj",
    )(x, g, w, gain_row, bd)


def _attn_kernel(q_ref, k_ref, vt_ref, knorm_ref, lam_ref, gsub_ref, o_ref, qq_ref, acc_ref, m_ref,
                 l_ref, sa_ref, sb_ref, *, bq, lambda_init, unroll, online):
    i = pl.program_id(1)
    q = q_ref[...]
    lane = lax.broadcasted_iota(jnp.int32, q.shape, 1)
    qq_ref[0:bq, :] = jnp.where(lane < HEAD_DIM, q, 0)
    qq_ref[bq:, :] = jnp.where(lane >= HEAD_DIM, q, 0)
    acc_ref[...] = jnp.zeros(acc_ref.shape, F32)
    l_ref[...] = jnp.zeros(l_ref.shape, F32)
    if online:
        m_ref[...] = jnp.full(m_ref.shape, NEG_BIG, F32)
    else:
        qq = qq_ref[...].astype(F32)
        q_norm2 = _dot_nt(jnp.ones((SUBLANES, 2 * HEAD_DIM), BF16), (qq * qq).astype(BF16))[0:1]
        m_ref[...] = jnp.sqrt(q_norm2) * knorm_ref[0] * BOUND_SLACK

    def score_tile(j):
        kb = k_ref[pl.ds(pl.multiple_of(j * bq, bq), bq), :]
        return _dot_nt(kb, qq_ref[...])

    def scores(j, s_ref):
        if online:
            s_ref[...] = score_tile(j)

    def softmax_pv(j, s_ref, masked):
        vb = vt_ref[:, pl.ds(pl.multiple_of(j * bq, bq), bq)]
        s = s_ref[...] if online else score_tile(j)
        if masked:
            key = lax.broadcasted_iota(jnp.int32, s.shape, 0)
            qry = lax.broadcasted_iota(jnp.int32, s.shape, 1)
            s = jnp.where(key <= jnp.where(qry >= bq, qry - bq, qry), s, NEG_BIG)
        if online:
            m_old = m_ref[...]
            m_new = jnp.maximum(m_old, jnp.max(s, axis=0, keepdims=True))
            alpha = jnp.exp2(m_old - m_new)
            p = jnp.exp2(s - m_new)
            l_ref[...] = alpha * l_ref[...] + jnp.sum(p, axis=0, keepdims=True)
            m_ref[...] = m_new
            acc_ref[...] = alpha * acc_ref[...] + _dot(vb, p.astype(BF16))
        else:
            p = jnp.exp2(s - m_ref[...])
            l_ref[...] += jnp.sum(p, axis=0, keepdims=True)
            acc_ref[...] += _dot(vb, p.astype(BF16))

    bufs = (sa_ref, sb_ref)
    scores(0, sa_ref)

    def group(g, _):
        for t in range(unroll):
            j = unroll * g + t
            scores(j + 1, bufs[(t + 1) % 2])
            softmax_pv(j, bufs[t % 2], masked=False)
        return 0

    lax.fori_loop(0, i // unroll, group, 0)
    j0 = (i // unroll) * unroll

    for r in range(unroll):
        @pl.when(i - j0 == r)
        def _(r=r):
            for t in range(r + 1):
                if t < r:
                    scores(j0 + t + 1, bufs[(t + 1) % 2])
                softmax_pv(j0 + t, bufs[t % 2], masked=(t == r))

    l = l_ref[...]
    lam_vecs = lam_ref[...]
    lam = (jnp.exp(jnp.sum(lam_vecs[0:1] * lam_vecs[1:2], axis=-1, keepdims=True))
           - jnp.exp(jnp.sum(lam_vecs[2:3] * lam_vecs[3:4], axis=-1, keepdims=True))
           + lambda_init)
    o = acc_ref[:, 0:bq] / l[:, 0:bq] - lam * (acc_ref[:, bq:] / l[:, bq:])
    o = o * lax.rsqrt(jnp.mean(o * o, axis=0, keepdims=True) + NORM_EPS)
    o = o * (gsub_ref[...] * (1.0 - lambda_init))
    o_ref[...] = o.T.astype(BF16)


def _diff_attention(qk, vt, q_gain, k_gain, lam_vecs, gsub_col, lambda_init, bq):
    unit = math.sqrt(HEAD_DIM)
    q_norm = unit * jnp.max(jnp.abs(q_gain), axis=-1)
    k_norm = unit * jnp.max(jnp.abs(k_gain), axis=-1)
    fits = jnp.all(q_norm * k_norm * BOUND_SLACK <= MAX_FIXED_SHIFT)
    knorm = jnp.repeat(k_norm, bq)[None, None, :]

    def call(online):
        return _diff_attention_call(qk, vt, knorm, lam_vecs, gsub_col, lambda_init, bq, online)

    return lax.cond(fits, lambda: call(False), lambda: call(True))


def _diff_attention_call(qk, vt, knorm, lam_vecs, gsub_col, lambda_init, bq, online):
    s = qk.shape[0]
    hd2 = 2 * HEAD_DIM
    return pl.pallas_call(
        functools.partial(_attn_kernel, bq=bq, lambda_init=lambda_init, unroll=ATTN_UNROLL,
                          online=online),
        grid=(N_HEADS, s // bq),
        in_specs=[
            pl.BlockSpec((bq, hd2), lambda h, i: (i, h)),
            pl.BlockSpec((s, hd2), lambda h, i: (0, N_HEADS + h)),
            pl.BlockSpec((hd2, s), lambda h, i: (h, 0)),
            pl.BlockSpec((1, 1, 2 * bq), lambda h, i: (0, 0, 0)),
            pl.BlockSpec((4, HEAD_DIM), lambda h, i: (0, 0)),
            pl.BlockSpec((hd2, 1), lambda h, i: (0, 0)),
        ],
        out_specs=pl.BlockSpec((bq, hd2), lambda h, i: (i, h)),
        out_shape=jax.ShapeDtypeStruct((s, N_HEADS * hd2), BF16),
        scratch_shapes=[pltpu.VMEM((2 * bq, hd2), BF16), pltpu.VMEM((hd2, 2 * bq), F32),
                        pltpu.VMEM((1, 2 * bq), F32), pltpu.VMEM((1, 2 * bq), F32),
                        pltpu.VMEM((bq, 2 * bq), F32), pltpu.VMEM((bq, 2 * bq), F32)],
        compiler_params=_params("parallel", "arbitrary"),
        name="diff_attn_online" if online else "diff_attn",
    )(qk, qk, vt, knorm, lam_vecs, gsub_col)


def _proj_res_kernel(x_ref, a_ref, w_ref, o_ref, wb_ref):
    @pl.when(pl.program_id(0) == 0)
    def _():
        wb_ref[...] = w_ref[...].astype(BF16)

    o_ref[...] = x_ref[...] + _dot(a_ref[...], wb_ref[...])


def _proj_residual(x, a, w, tm):
    s, d = x.shape
    k = a.shape[1]
    return pl.pallas_call(
        _proj_res_kernel,
        grid=(s // tm,),
        in_specs=[
            pl.BlockSpec((tm, d), lambda i: (i, 0)),
            pl.BlockSpec((tm, k), lambda i: (i, 0)),
            pl.BlockSpec((k, d), lambda i: (0, 0)),
        ],
        out_specs=pl.BlockSpec((tm, d), lambda i: (i, 0)),
        out_shape=jax.ShapeDtypeStruct((s, d), F32),
        scratch_shapes=[pltpu.VMEM((k, d), BF16)],
        compiler_params=_params("arbitrary"),
        name="attn_out_proj",
    )(x, a, w)


def _swiglu_kernel(te_ref, nt_ref, x_ref, g_ref, wg_ref, wu_ref, wd_ref, o_ref, h_ref, acc_ref,
                   *, add_residual):
    t = pl.program_id(0)
    f = pl.program_id(1)

    @pl.when(t < nt_ref[0])
    def _():
        @pl.when(f == 0)
        def _():
            h_ref[...] = _rms(x_ref[...], g_ref[...]).astype(BF16)
            acc_ref[...] = jnp.zeros(acc_ref.shape, F32)

        h = h_ref[...]
        gate = _dot(h, wg_ref[0].astype(BF16))
        up = _dot(h, wu_ref[0].astype(BF16))
        act = (gate * jax.nn.sigmoid(gate) * up).astype(BF16)
        acc_ref[...] += _dot(act, wd_ref[0].astype(BF16))

        @pl.when(f == pl.num_programs(1) - 1)
        def _():
            if add_residual:
                o_ref[...] = x_ref[...] + acc_ref[...]
            else:
                o_ref[...] = acc_ref[...]

    @pl.when((t >= nt_ref[0]) & (f == 0))
    def _():
        o_ref[...] = jnp.zeros(o_ref.shape, F32)


def _grouped_swiglu(x, g, w_gate, w_up, w_down, tile_expert, n_tiles, tm, tf, add_residual):
    m, d = x.shape
    dff = w_gate.shape[2]
    nf = dff // tf
    max_tiles = tile_expert.shape[0]

    def _last(nt):
        return jnp.maximum(nt[0] - 1, 0)

    def row_map(t, f, te, nt):
        return (jnp.minimum(t, _last(nt)), 0)

    def _tf(t, f, nt):
        return jnp.where(t < nt[0], f, nf - 1)

    def _te(t, te, nt):
        return te[jnp.minimum(t, _last(nt))]

    return pl.pallas_call(
        functools.partial(_swiglu_kernel, add_residual=add_residual),
        grid_spec=pltpu.PrefetchScalarGridSpec(
            num_scalar_prefetch=2,
            grid=(max_tiles, nf),
            in_specs=[
                pl.BlockSpec((tm, d), row_map),
                pl.BlockSpec((1, d), lambda t, f, te, nt: (0, 0)),
                pl.BlockSpec((1, d, tf), lambda t, f, te, nt: (_te(t, te, nt), 0, _tf(t, f, nt))),
                pl.BlockSpec((1, d, tf), lambda t, f, te, nt: (_te(t, te, nt), 0, _tf(t, f, nt))),
                pl.BlockSpec((1, tf, d), lambda t, f, te, nt: (_te(t, te, nt), _tf(t, f, nt), 0)),
            ],
            out_specs=pl.BlockSpec((tm, d), lambda t, f, te, nt: (t, 0)),
            scratch_shapes=[pltpu.VMEM((tm, d), BF16), pltpu.VMEM((tm, d), F32)],
        ),
        out_shape=jax.ShapeDtypeStruct((m, d), F32),
        compiler_params=_params("arbitrary", "arbitrary"),
        name="swiglu_residual",
    )(tile_expert, n_tiles, x, g, w_gate, w_up, w_down)


def _expert_kernel(te_ref, nt_ref, rt_first_ref, rt_next_ref, x_ref, g_ref, wg_ref, wu_ref, wd_ref,
                   o_ref, xbuf_ref, h_ref, acc_ref, sem, *, tm, nf):
    t = pl.program_id(0)
    f = pl.program_id(1)
    n = nt_ref[0]
    chunk = tm // nf

    def gather(rt_ref, row, slot):
        return pltpu.make_async_copy(x_ref.at[pl.ds(rt_ref[0, 0, row], 1), :],
                                     xbuf_ref.at[slot, pl.ds(row, 1), :], sem.at[slot])

    def wait_tile(slot):
        pltpu.make_async_copy(x_ref.at[pl.ds(0, tm), :], xbuf_ref.at[slot], sem.at[slot]).wait()

    @pl.when(t < n)
    def _():
        slot = t % 2

        @pl.when((t == 0) & (f == 0))
        def _():
            def issue(r, _):
                gather(rt_first_ref, r, 0).start()
                return 0
            lax.fori_loop(0, tm, issue, 0, unroll=8)

        @pl.when(f == 0)
        def _():
            wait_tile(slot)
            h_ref[...] = _rms(xbuf_ref[slot], g_ref[...]).astype(BF16)
            acc_ref[...] = jnp.zeros(acc_ref.shape, F32)

        for r in range(chunk):
            gather(rt_next_ref, f * chunk + r, 1 - slot).start()
        h = h_ref[...]
        gate = _dot(h, wg_ref[0].astype(BF16))
        up = _dot(h, wu_ref[0].astype(BF16))
        act = (gate * jax.nn.sigmoid(gate) * up).astype(BF16)
        acc_ref[...] += _dot(act, wd_ref[0].astype(BF16))

        @pl.when(f == nf - 1)
        def _():
            o_ref[...] = acc_ref[...]

    @pl.when((t >= n) & (f == 0))
    def _():
        o_ref[...] = jnp.zeros(o_ref.shape, F32)

        @pl.when(t == n)
        def _():
            wait_tile(n % 2)


def _expert_swiglu(x, g, w_gate, w_up, w_down, tile_expert, n_tiles, row_token, tm, tf):
    s, d = x.shape
    dff = w_gate.shape[2]
    nf = dff // tf
    assert tm % nf == 0
    n_steps = tile_expert.shape[0] + 1

    def _last(nt):
        return jnp.maximum(nt[0] - 1, 0)

    def _tf(t, f, nt):
        return jnp.where(t < nt[0], f, nf - 1)

    def _te(t, te, nt):
        return te[jnp.minimum(t, _last(nt))]

    smem_tile = functools.partial(pl.BlockSpec, (1, 1, tm), memory_space=pltpu.SMEM)
    return pl.pallas_call(
        functools.partial(_expert_kernel, tm=tm, nf=nf),
        grid_spec=pltpu.PrefetchScalarGridSpec(
            num_scalar_prefetch=2,
            grid=(n_steps, nf),
            in_specs=[
                smem_tile(lambda t, f, te, nt: (0, 0, 0)),
                smem_tile(lambda t, f, te, nt: (jnp.minimum(t + 1, n_steps - 1), 0, 0)),
                pl.BlockSpec(memory_space=pl.ANY),
                pl.BlockSpec((1, d), lambda t, f, te, nt: (0, 0)),
                pl.BlockSpec((1, d, tf), lambda t, f, te, nt: (_te(t, te, nt), 0, _tf(t, f, nt))),
                pl.BlockSpec((1, d, tf), lambda t, f, te, nt: (_te(t, te, nt), 0, _tf(t, f, nt))),
                pl.BlockSpec((1, tf, d), lambda t, f, te, nt: (_te(t, te, nt), _tf(t, f, nt), 0)),
            ],
            out_specs=pl.BlockSpec((tm, d), lambda t, f, te, nt: (t, 0)),
            scratch_shapes=[pltpu.VMEM((2, tm, d), F32), pltpu.VMEM((tm, d), BF16),
                            pltpu.VMEM((tm, d), F32), pltpu.SemaphoreType.DMA((2,))],
        ),
        out_shape=jax.ShapeDtypeStruct((n_steps * tm, d), F32),
        compiler_params=_params("arbitrary", "arbitrary"),
        name="expert_swiglu",
    )(tile_expert, n_tiles, row_token, row_token, x, g, w_gate, w_up, w_down)


def _ple_tail(x, p_ref, wp_ref, wg_ref, g_e_ref, g_gate_ref, o_ref):
    e = _rms(_dot(p_ref[...].astype(BF16), wp_ref[...]), g_e_ref[...])
    gate = jax.nn.sigmoid(_dot(_rms(x, g_gate_ref[...]).astype(BF16), wg_ref[...]))
    o_ref[...] = x + gate * e


def _ple_kernel(x_ref, p_ref, wp_ref, wg_ref, g_e_ref, g_gate_ref, o_ref, wpb_ref, wgb_ref):
    @pl.when(pl.program_id(0) == 0)
    def _():
        wpb_ref[...] = wp_ref[...].astype(BF16)
        wgb_ref[...] = wg_ref[...].astype(BF16)

    _ple_tail(x_ref[...], p_ref, wpb_ref, wgb_ref, g_e_ref, g_gate_ref, o_ref)


def _combine_ple_kernel(pos_ref, x_ref, w_ref, ys_ref, p_ref, wp_ref, wg_ref, g_e_ref, g_gate_ref,
                        o_ref, wpb_ref, wgb_ref, ybuf_ref, sem, *, tc):
    @pl.when(pl.program_id(0) == 0)
    def _():
        wpb_ref[...] = wp_ref[...].astype(BF16)
        wgb_ref[...] = wg_ref[...].astype(BF16)

    def row_copy(k, t):
        return pltpu.make_async_copy(ys_ref.at[pl.ds(pos_ref[k, t], 1), :],
                                     ybuf_ref.at[k, pl.ds(t, 1), :], sem)

    def issue(t, _):
        row_copy(0, t).start(priority=0)
        row_copy(1, t).start(priority=1)
        return 0

    lax.fori_loop(0, tc, issue, 0, unroll=8)
    for k in range(2):
        pltpu.make_async_copy(ys_ref.at[pl.ds(0, tc), :], ybuf_ref.at[k], sem).wait()
    w = w_ref[...]
    x = x_ref[...] + w[:, 0:1] * ybuf_ref[0] + w[:, 1:2] * ybuf_ref[1]
    _ple_tail(x, p_ref, wpb_ref, wgb_ref, g_e_ref, g_gate_ref, o_ref)


def _ple(x, p, wp, wg, g_e, g_gate, tm, routed=None):
    s, d = x.shape
    pd = p.shape[1]
    row = lambda i: (i, 0)
    fixed = lambda i: (0, 0)
    tail_specs = [
        pl.BlockSpec((tm, pd), row),
        pl.BlockSpec((pd, d), fixed),
        pl.BlockSpec((d, d), fixed),
        pl.BlockSpec((1, d), fixed),
        pl.BlockSpec((1, d), fixed),
    ]
    scratch = [pltpu.VMEM((pd, d), BF16), pltpu.VMEM((d, d), BF16)]
    if routed is None:
        return pl.pallas_call(
            _ple_kernel,
            grid=(s // tm,),
            in_specs=[pl.BlockSpec((tm, d), row)] + tail_specs,
            out_specs=pl.BlockSpec((tm, d), row),
            out_shape=jax.ShapeDtypeStruct((s, d), F32),
            scratch_shapes=scratch,
            compiler_params=_params("arbitrary"),
            name="ple",
        )(x, p, wp, wg, g_e, g_gate)
    pos, w12, ys = routed
    return pl.pallas_call(
        functools.partial(_combine_ple_kernel, tc=tm),
        grid=(s // tm,),
        in_specs=[
            pl.BlockSpec((2, tm), lambda i: (0, i), memory_space=pltpu.SMEM),
            pl.BlockSpec((tm, d), row),
            pl.BlockSpec((tm, 2), row),
            pl.BlockSpec(memory_space=pl.ANY),
        ] + tail_specs,
        out_specs=pl.BlockSpec((tm, d), row),
        out_shape=jax.ShapeDtypeStruct((s, d), F32),
        scratch_shapes=scratch + [pltpu.VMEM((2, tm, d), F32), pltpu.SemaphoreType.DMA(())],
        compiler_params=_params("arbitrary"),
        name="combine_ple",
    )(pos, x, w12, ys, p, wp, wg, g_e, g_gate)


def _route(x, g_ref, wrt_ref, b_ref, tri_ref, carry_ref, info_ref, cnt_ref):
    h = _rms(x, g_ref[...])
    logits = lax.dot_general(wrt_ref[...], h, (((1,), (1,)), ((), ())),
                             preferred_element_type=F32,
                             precision=lax.Precision.HIGHEST) + b_ref[...]
    eidx = lax.broadcasted_iota(jnp.int32, logits.shape, 0)
    m1 = jnp.max(logits, axis=0, keepdims=True)
    e1 = jnp.min(jnp.where(logits == m1, eidx, N_EXPERTS), axis=0, keepdims=True)
    rest = jnp.where(eidx == e1, -jnp.inf, logits)
    m2 = jnp.max(rest, axis=0, keepdims=True)
    e2 = jnp.min(jnp.where(rest == m2, eidx, N_EXPERTS), axis=0, keepdims=True)
    z = jnp.exp(m2 - m1)
    w1 = 1.0 / (1.0 + z)
    w2 = z / (1.0 + z)
    sel1 = eidx == e1
    sel2 = eidx == e2
    sel = (sel1 | sel2).astype(F32)
    rank = carry_ref[:, 0:1] + _dot(sel.astype(BF16), tri_ref[...])
    r1 = jnp.sum(jnp.where(sel1, rank, 0.0), axis=0, keepdims=True)
    r2 = jnp.sum(jnp.where(sel2, rank, 0.0), axis=0, keepdims=True)
    rows = (e1.astype(F32), e2.astype(F32), w1, w2, r1, r2)
    info = jnp.zeros(logits.shape, F32)
    for k, v in enumerate(rows):
        info = jnp.where(eidx == k, v, info)
    info_ref[...] = info
    carry_ref[...] = carry_ref[...] + jnp.sum(sel, axis=1, keepdims=True)
    cnt_ref[...] = carry_ref[...]


def _conv_router_kernel(x_ref, g_ref, w1_ref, b1_ref, wdw_ref, bdw_ref, lng_ref, lnb_ref, w2_ref, b2_ref,
                        rg_ref, wrt_ref, rb_ref, o_ref, info_ref, cnt_ref,
                        ext_ref, c_ref, tri_ref, carry_ref, *, tm, rc):
    i = pl.program_id(0)
    d = x_ref.shape[1]

    @pl.when(i == 0)
    def _():
        ext_ref[0:HALO_ROWS, :] = jnp.zeros((HALO_ROWS, d), F32)
        a = lax.broadcasted_iota(jnp.int32, (tm, tm), 0)
        b = lax.broadcasted_iota(jnp.int32, (tm, tm), 1)
        tri_ref[...] = (a < b).astype(BF16)
        carry_ref[...] = jnp.zeros(carry_ref.shape, F32)

    @pl.when(i > 0)
    def _():
        ext_ref[0:HALO_ROWS, :] = ext_ref[tm:tm + HALO_ROWS, :]

    x = x_ref[...]
    y = _dot(_rms(x, g_ref[...]).astype(BF16), w1_ref[...]) + b1_ref[...]
    ext_ref[HALO_ROWS:, :] = y[:, :d] * jax.nn.sigmoid(y[:, d:])
    first = HALO_ROWS - (CONV_WIDTH - 1)
    win_rows = rc + HALO_ROWS

    def chunk(r, _):
        base = pl.multiple_of(r * rc, rc)
        for lo in range(0, d, LANES):
            cols = slice(lo, lo + LANES)
            win = ext_ref[pl.ds(base, win_rows), cols]
            acc = jnp.zeros((rc, LANES), F32) + bdw_ref[:, cols]
            for b in range(SUBLANES):
                wb = win if b == 0 else pltpu.roll(win, shift=win_rows - b, axis=0)
                for a in range(win_rows // SUBLANES):
                    t = SUBLANES * a + b - first
                    if 0 <= t < CONV_WIDTH:
                        acc = acc + wdw_ref[t:t + 1, cols] * wb[SUBLANES * a:SUBLANES * a + rc, :]
            c_ref[pl.ds(base, rc), cols] = acc
        return 0

    lax.fori_loop(0, tm // rc, chunk, 0)
    c = c_ref[...]
    mu = jnp.mean(c, axis=-1, keepdims=True)
    cc = c - mu
    y = cc * lax.rsqrt(jnp.mean(cc * cc, axis=-1, keepdims=True) + LN_EPS) * lng_ref[...] + lnb_ref[...]
    y = y * jax.nn.sigmoid(y)
    x1 = x + _dot(y.astype(BF16), w2_ref[...]) + b2_ref[...]
    o_ref[...] = x1
    _route(x1, rg_ref, wrt_ref, rb_ref, tri_ref, carry_ref, info_ref, cnt_ref)


def _conv_module_router(x, g, w1, b1, w_dw, b_dw, ln_g, ln_b, w2, b2, rg, w_router_t, rb_col, tm):
    s, d = x.shape
    row = lambda i: (i, 0)
    fixed = lambda i: (0, 0)
    return pl.pallas_call(
        functools.partial(_conv_router_kernel, tm=tm, rc=min(64, tm)),
        grid=(s // tm,),
        in_specs=[
            pl.BlockSpec((tm, d), row),
            pl.BlockSpec((1, d), fixed),
            pl.BlockSpec((d, 2 * d), fixed),
            pl.BlockSpec((1, 2 * d), fixed),
            pl.BlockSpec((CONV_WIDTH, d), fixed),
            pl.BlockSpec((1, d), fixed),
            pl.BlockSpec((1, d), fixed),
            pl.BlockSpec((1, d), fixed),
            pl.BlockSpec((d, d), fixed),
            pl.BlockSpec((1, d), fixed),
            pl.BlockSpec((1, d), fixed),
            pl.BlockSpec((N_EXPERTS, d), fixed),
            pl.BlockSpec((N_EXPERTS, 1), fixed),
        ],
        out_specs=[
            pl.BlockSpec((tm, d), row),
            pl.BlockSpec((N_EXPERTS, tm), lambda i: (0, i)),
            pl.BlockSpec((N_EXPERTS, LANES), fixed),
        ],
        out_shape=[jax.ShapeDtypeStruct((s, d), F32),
                   jax.ShapeDtypeStruct((N_EXPERTS, s), F32),
                   jax.ShapeDtypeStruct((N_EXPERTS, LANES), F32)],
        scratch_shapes=[pltpu.VMEM((tm + HALO_ROWS, d), F32), pltpu.VMEM((tm, d), F32),
                        pltpu.VMEM((tm, tm), BF16), pltpu.VMEM((N_EXPERTS, LANES), F32)],
        compiler_params=_params("arbitrary"),
        name="conv_module_router",
    )(x, g, w1, b1, w_dw, b_dw, ln_g, ln_b, w2, b2, rg, w_router_t, rb_col)


def _tiles(s):
    t = lambda want: min(want, s)
    return dict(qkv=(t(1024), 512), attn=t(512), proj=t(512), ffn=(t(1024), 512), ple=t(512),
                conv=t(512), move=t(512), expert=(t(896), 512))


def _layer_even(x, p, tiles, mix_g, ffn_g, w_qkv, w_o, g_q, g_k, lam_vecs, g_sub, lambda_init,
                w_gate, w_up, w_down, ple_wp, ple_g, ple_gate_g, ple_wg):
    s, d = x.shape
    q_gain = g_q * (ATTN_SCALE * LOG2_E)
    gain_row = jnp.concatenate([jnp.tile(q_gain.reshape(-1), N_HEADS),
                                jnp.tile(g_k.reshape(-1), N_HEADS)])[None, :]
    qk, vt = _qkv_proj(x, mix_g[None, :], w_qkv, gain_row, *tiles["qkv"])
    o = _diff_attention(qk, vt, q_gain, g_k, lam_vecs, g_sub[:, None], lambda_init, tiles["attn"])
    x = _proj_residual(x, o, w_o, tiles["proj"])
    tm, tf = tiles["ffn"]
    n_tiles = s // tm
    x = _grouped_swiglu(x, ffn_g[None, :], w_gate[None], w_up[None], w_down[None],
                        jnp.zeros((n_tiles,), jnp.int32), jnp.full((1,), n_tiles, jnp.int32),
                        tm, tf, add_residual=True)
    return _ple(x, p, ple_wp, ple_wg, ple_g[None, :], ple_gate_g[None, :], tiles["ple"])


def _layer_odd(x, p, tiles, mix_g, ffn_g, w_pw1, b_pw1, w_dw, b_dw, ln_g, ln_b, w_pw2, b_pw2,
               w_router, b_router, w_gate, w_up, w_down, ple_wp, ple_g, ple_gate_g, ple_wg):
    s, d = x.shape
    x, info, counts = _conv_module_router(
        x, mix_g[None, :], w_pw1.astype(BF16), b_pw1[None, :], w_dw, b_dw[None, :], ln_g[None, :],
        ln_b[None, :], w_pw2.astype(BF16), b_pw2[None, :], ffn_g[None, :], w_router.T,
        b_router[:, None], tiles["conv"])
    tm, tf = tiles["expert"]
    e12 = info[0:2].astype(jnp.int32)
    w12 = info[2:4].T
    r12 = info[4:6].astype(jnp.int32)
    cnt = counts[:, 0].astype(jnp.int32)
    group_tiles = (cnt + tm - 1) // tm
    tile_end = jnp.cumsum(group_tiles)
    offsets = (tile_end - group_tiles) * tm
    expert_ids = jnp.arange(N_EXPERTS, dtype=jnp.int32)
    onehot = e12[:, :, None] == expert_ids[None, None, :]
    pos = jnp.sum(jnp.where(onehot, offsets[None, None, :], 0), axis=-1) + r12
    max_tiles = (2 * s) // tm + N_EXPERTS
    tile_ids = jnp.arange(max_tiles, dtype=jnp.int32)
    tile_expert = jnp.minimum(jnp.sum((tile_end[None, :] <= tile_ids[:, None]).astype(jnp.int32), axis=1),
                              N_EXPERTS - 1)
    n_tiles = tile_end[-1:].astype(jnp.int32)
    tokens = jnp.tile(jnp.arange(s, dtype=jnp.int32), 2)
    row_token = jnp.zeros(((max_tiles + 1) * tm,), jnp.int32).at[pos.reshape(-1)].set(
        tokens, unique_indices=True).reshape(max_tiles + 1, 1, tm)
    ys = _expert_swiglu(x, ffn_g[None, :], w_gate, w_up, w_down, tile_expert, n_tiles, row_token, tm, tf)
    return _ple(x, p, ple_wp, ple_wg, ple_g[None, :], ple_gate_g[None, :], tiles["move"],
                routed=(pos, w12, ys))


def kernel(x, p, mix_norm_g, ffn_norm_g, attn_w_qkv, attn_w_o, attn_g_q, attn_g_k, attn_lam_q1, attn_lam_k1, attn_lam_q2, attn_lam_k2, attn_g_sub, conv_w_pw1, conv_b_pw1, conv_w_dw, conv_b_dw, conv_ln_g, conv_ln_b, conv_w_pw2, conv_b_pw2, ffn_w_gate, ffn_w_up, ffn_w_down, moe_w_router, moe_b_router, moe_w_gate, moe_w_up, moe_w_down, ple_w_proj, ple_norm_g, ple_gate_norm_g, ple_w_gate):
    depth, batch, s, _ = p.shape
    tiles = _tiles(s)
    outs = []
    for b in range(batch):
        xb = x[b]
        for i in range(depth):
            j = i // 2
            ple_args = (ple_w_proj[i], ple_norm_g[i], ple_gate_norm_g[i], ple_w_gate[i])
            if i % 2 == 0:
                lambda_init = 0.8 - 0.6 * math.exp(-0.3 * i)
                lam_vecs = jnp.stack([attn_lam_q1[j], attn_lam_k1[j], attn_lam_q2[j], attn_lam_k2[j]])
                xb = _layer_even(xb, p[i, b], tiles, mix_norm_g[i], ffn_norm_g[i], attn_w_qkv[j],
                                 attn_w_o[j], attn_g_q[j], attn_g_k[j], lam_vecs, attn_g_sub[j],
                                 lambda_init, ffn_w_gate[j], ffn_w_up[j], ffn_w_down[j], *ple_args)
            else:
                xb = _layer_odd(xb, p[i, b], tiles, mix_norm_g[i], ffn_norm_g[i], conv_w_pw1[j],
                                conv_b_pw1[j], conv_w_dw[j], conv_b_dw[j], conv_ln_g[j], conv_ln_b[j],
                                conv_w_pw2[j], conv_b_pw2[j], moe_w_router[j], moe_b_router[j],
                                moe_w_gate[j], moe_w_up[j], moe_w_down[j], *ple_args)
        outs.append(xb)
    return jnp.stack(outs)
```

```python
import functools
import math

import jax
import jax.numpy as jnp
from jax import lax
from jax.experimental import pallas as pl
from jax.experimental.pallas import tpu as pltpu

N_HEADS = 8
HEAD_DIM = 64
N_EXPERTS = 8
CONV_WIDTH = 31
NORM_EPS = 1e-6
LN_EPS = 1e-5
ATTN_SCALE = HEAD_DIM ** -0.5
LOG2_E = math.log2(math.e)

BF16 = jnp.bfloat16
F32 = jnp.float32
NEG_BIG = -1e30
VMEM_LIMIT_BYTES = 56 * 1024 * 1024
SUBLANES, LANES = 8, 128
ATTN_UNROLL = 4
BOUND_SLACK = 1.01
MAX_FIXED_SHIFT = 40.0
HALO_ROWS = 32


def _params(*sem):
    return pltpu.CompilerParams(dimension_semantics=sem, vmem_limit_bytes=VMEM_LIMIT_BYTES)


def _rms(xf, g):
    return xf * lax.rsqrt(jnp.mean(xf * xf, axis=-1, keepdims=True) + NORM_EPS) * g


def _dot(a, b):
    return jnp.dot(a, b, preferred_element_type=F32)


def _dot_nt(a, b):
    return lax.dot_general(a, b, (((1,), (1,)), ((), ())), preferred_element_type=F32)


def _qkv_kernel(x_ref, g_ref, w_ref, gain_ref, bd_ref, qk_ref, vt_ref, h_ref, *, n_qk_tiles):
    j = pl.program_id(1)

    @pl.when(j == 0)
    def _():
        h_ref[...] = _rms(x_ref[...], g_ref[...]).astype(BF16)

    y = _dot(h_ref[...], w_ref[...].astype(BF16))

    @pl.when(j < n_qk_tiles)
    def _():
        ss = _dot((y * y).astype(BF16), bd_ref[...])
        qk_ref[...] = (y * lax.rsqrt(ss * (1.0 / HEAD_DIM) + NORM_EPS) * gain_ref[...]).astype(BF16)

    @pl.when(j >= n_qk_tiles)
    def _():
        vt_ref[...] = y.T.astype(BF16)


def _qkv_proj(x, g, w, gain_row, tm, tn):
    s, d = x.shape
    n = w.shape[1]
    n_qk = 2 * n // 3
    n_qk_tiles = n_qk // tn
    lane = jnp.arange(tn) // HEAD_DIM
    bd = (lane[:, None] == lane[None, :]).astype(BF16)
    return pl.pallas_call(
        functools.partial(_qkv_kernel, n_qk_tiles=n_qk_tiles),
        grid=(s // tm, n // tn),
        in_specs=[
            pl.BlockSpec((tm, d), lambda i, j: (i, 0)),
            pl.BlockSpec((1, d), lambda i, j: (0, 0)),
            pl.BlockSpec((d, tn), lambda i, j: (0, j)),
            pl.BlockSpec((1, tn), lambda i, j: (0, jnp.minimum(j, n_qk_tiles - 1))),
            pl.BlockSpec((tn, tn), lambda i, j: (0, 0)),
        ],
        out_specs=[
            pl.BlockSpec((tm, tn), lambda i, j: (i, jnp.minimum(j, n_qk_tiles - 1))),
            pl.BlockSpec((tn, tm), lambda i, j: (jnp.maximum(j - n_qk_tiles, 0), i)),
        ],
        out_shape=[jax.ShapeDtypeStruct((s, n_qk), BF16),
                   jax.ShapeDtypeStruct((n - n_qk, s), BF16)],
        scratch_shapes=[pltpu.VMEM((tm, d), BF16)],
        compiler_params=_params("parallel", "arbitrary"),
        name="qkv_proj",
    )(x, g, w, gain_row, bd)


def _attn_kernel(q_ref, k_ref, vt_ref, knorm_ref, lam_ref, gsub_ref, o_ref, qq_ref, acc_ref, m_ref,
                 l_ref, sa_ref, sb_ref, *, bq, lambda_init, unroll, online):
    i = pl.program_id(1)
    q = q_ref[...]
    lane = lax.broadcasted_iota(jnp.int32, q.shape, 1)
    qq_ref[0:bq, :] = jnp.where(lane < HEAD_DIM, q, 0)
    qq_ref[bq:, :] = jnp.where(lane >= HEAD_DIM, q, 0)
    acc_ref[...] = jnp.zeros(acc_ref.shape, F32)
    l_ref[...] = jnp.zeros(l_ref.shape, F32)
    if online:
        m_ref[...] = jnp.full(m_ref.shape, NEG_BIG, F32)
    else:
        qq = qq_ref[...].astype(F32)
        q_norm2 = _dot_nt(jnp.ones((SUBLANES, 2 * HEAD_DIM), BF16), (qq * qq).astype(BF16))[0:1]
        m_ref[...] = jnp.sqrt(q_norm2) * knorm_ref[0] * BOUND_SLACK

    def score_tile(j):
        kb = k_ref[pl.ds(pl.multiple_of(j * bq, bq), bq), :]
        return _dot_nt(kb, qq_ref[...])

    def scores(j, s_ref):
        if online:
            s_ref[...] = score_tile(j)

    def softmax_pv(j, s_ref, masked):
        vb = vt_ref[:, pl.ds(pl.multiple_of(j * bq, bq), bq)]
        s = s_ref[...] if online else score_tile(j)
        if masked:
            key = lax.broadcasted_iota(jnp.int32, s.shape, 0)
            qry = lax.broadcasted_iota(jnp.int32, s.shape, 1)
            s = jnp.where(key <= jnp.where(qry >= bq, qry - bq, qry), s, NEG_BIG)
        if online:
            m_old = m_ref[...]
            m_new = jnp.maximum(m_old, jnp.max(s, axis=0, keepdims=True))
            alpha = jnp.exp2(m_old - m_new)
            p = jnp.exp2(s - m_new)
            l_ref[...] = alpha * l_ref[...] + jnp.sum(p, axis=0, keepdims=True)
            m_ref[...] = m_new
            acc_ref[...] = alpha * acc_ref[...] + _dot(vb, p.astype(BF16))
        else:
            p = jnp.exp2(s - m_ref[...])
            l_ref[...] += jnp.sum(p, axis=0, keepdims=True)
            acc_ref[...] += _dot(vb, p.astype(BF16))

    bufs = (sa_ref, sb_ref)
    scores(0, sa_ref)

    def group(g, _):
        for t in range(unroll):
            j = unroll * g + t
            scores(j + 1, bufs[(t + 1) % 2])
            softmax_pv(j, bufs[t % 2], masked=False)
        return 0

    lax.fori_loop(0, i // unroll, group, 0)
    j0 = (i // unroll) * unroll

    for r in range(unroll):
        @pl.when(i - j0 == r)
        def _(r=r):
            for t in range(r + 1):
                if t < r:
                    scores(j0 + t + 1, bufs[(t + 1) % 2])
                softmax_pv(j0 + t, bufs[t % 2], masked=(t == r))

    l = l_ref[...]
    lam_vecs = lam_ref[...]
    lam = (jnp.exp(jnp.sum(lam_vecs[0:1] * lam_vecs[1:2], axis=-1, keepdims=True))
           - jnp.exp(jnp.sum(lam_vecs[2:3] * lam_vecs[3:4], axis=-1, keepdims=True))
           + lambda_init)
    o = acc_ref[:, 0:bq] / l[:, 0:bq] - lam * (acc_ref[:, bq:] / l[:, bq:])
    o = o * lax.rsqrt(jnp.mean(o * o, axis=0, keepdims=True) + NORM_EPS)
    o = o * (gsub_ref[...] * (1.0 - lambda_init))
    o_ref[...] = o.T.astype(BF16)


def _diff_attention(qk, vt, q_gain, k_gain, lam_vecs, gsub_col, lambda_init, bq):
    unit = math.sqrt(HEAD_DIM)
    q_norm = unit * jnp.max(jnp.abs(q_gain), axis=-1)
    k_norm = unit * jnp.max(jnp.abs(k_gain), axis=-1)
    fits = jnp.all(q_norm * k_norm * BOUND_SLACK <= MAX_FIXED_SHIFT)
    knorm = jnp.repeat(k_norm, bq)[None, None, :]

    def call(online):
        return _diff_attention_call(qk, vt, knorm, lam_vecs, gsub_col, lambda_init, bq, online)

    return lax.cond(fits, lambda: call(False), lambda: call(True))


def _diff_attention_call(qk, vt, knorm, lam_vecs, gsub_col, lambda_init, bq, online):
    s = qk.shape[0]
    hd2 = 2 * HEAD_DIM
    return pl.pallas_call(
        functools.partial(_attn_kernel, bq=bq, lambda_init=lambda_init, unroll=ATTN_UNROLL,
                          online=online),
        grid=(N_HEADS, s // bq),
        in_specs=[
            pl.BlockSpec((bq, hd2), lambda h, i: (i, h)),
            pl.BlockSpec((s, hd2), lambda h, i: (0, N_HEADS + h)),
            pl.BlockSpec((hd2, s), lambda h, i: (h, 0)),
            pl.BlockSpec((1, 1, 2 * bq), lambda h, i: (0, 0, 0)),
            pl.BlockSpec((4, HEAD_DIM), lambda h, i: (0, 0)),
            pl.BlockSpec((hd2, 1), lambda h, i: (0, 0)),
        ],
        out_specs=pl.BlockSpec((bq, hd2), lambda h, i: (i, h)),
        out_shape=jax.ShapeDtypeStruct((s, N_HEADS * hd2), BF16),
        scratch_shapes=[pltpu.VMEM((2 * bq, hd2), BF16), pltpu.VMEM((hd2, 2 * bq), F32),
                        pltpu.VMEM((1, 2 * bq), F32), pltpu.VMEM((1, 2 * bq), F32),
                        pltpu.VMEM((bq, 2 * bq), F32), pltpu.VMEM((bq, 2 * bq), F32)],
        compiler_params=_params("parallel", "arbitrary"),
        name="diff_attn_online" if online else "diff_attn",
    )(qk, qk, vt, knorm, lam_vecs, gsub_col)


def _proj_res_kernel(x_ref, a_ref, w_ref, o_ref, wb_ref):
    @pl.when(pl.program_id(0) == 0)
    def _():
        wb_ref[...] = w_ref[...].astype(BF16)

    o_ref[...] = x_ref[...] + _dot(a_ref[...], wb_ref[...])


def _proj_residual(x, a, w, tm):
    s, d = x.shape
    k = a.shape[1]
    return pl.pallas_call(
        _proj_res_kernel,
        grid=(s // tm,),
        in_specs=[
            pl.BlockSpec((tm, d), lambda i: (i, 0)),
            pl.BlockSpec((tm, k), lambda i: (i, 0)),
            pl.BlockSpec((k, d), lambda i: (0, 0)),
        ],
        out_specs=pl.BlockSpec((tm, d), lambda i: (i, 0)),
        out_shape=jax.ShapeDtypeStruct((s, d), F32),
        scratch_shapes=[pltpu.VMEM((k, d), BF16)],
        compiler_params=_params("arbitrary"),
        name="attn_out_proj",
    )(x, a, w)


def _swiglu_kernel(te_ref, nt_ref, x_ref, g_ref, wg_ref, wu_ref, wd_ref, o_ref, h_ref, acc_ref,
                   *, add_residual):
    t = pl.program_id(0)
    f = pl.program_id(1)

    @pl.when(t < nt_ref[0])
    def _():
        @pl.when(f == 0)
        def _():
            h_ref[...] = _rms(x_ref[...], g_ref[...]).astype(BF16)
            acc_ref[...] = jnp.zeros(acc_ref.shape, F32)

        h = h_ref[...]
        gate = _dot(h, wg_ref[0].astype(BF16))
        up = _dot(h, wu_ref[0].astype(BF16))
        act = (gate * jax.nn.sigmoid(gate) * up).astype(BF16)
        acc_ref[...] += _dot(act, wd_ref[0].astype(BF16))

        @pl.when(f == pl.num_programs(1) - 1)
        def _():
            if add_residual:
                o_ref[...] = x_ref[...] + acc_ref[...]
            else:
                o_ref[...] = acc_ref[...]

    @pl.when((t >= nt_ref[0]) & (f == 0))
    def _():
        o_ref[...] = jnp.zeros(o_ref.shape, F32)


def _grouped_swiglu(x, g, w_gate, w_up, w_down, tile_expert, n_tiles, tm, tf, add_residual):
    m, d = x.shape
    dff = w_gate.shape[2]
    nf = dff // tf
    max_tiles = tile_expert.shape[0]

    def _last(nt):
        return jnp.maximum(nt[0] - 1, 0)

    def row_map(t, f, te, nt):
        return (jnp.minimum(t, _last(nt)), 0)

    def _tf(t, f, nt):
        return jnp.where(t < nt[0], f, nf - 1)

    def _te(t, te, nt):
        return te[jnp.minimum(t, _last(nt))]

    return pl.pallas_call(
        functools.partial(_swiglu_kernel, add_residual=add_residual),
        grid_spec=pltpu.PrefetchScalarGridSpec(
            num_scalar_prefetch=2,
            grid=(max_tiles, nf),
            in_specs=[
                pl.BlockSpec((tm, d), row_map),
                pl.BlockSpec((1, d), lambda t, f, te, nt: (0, 0)),
                pl.BlockSpec((1, d, tf), lambda t, f, te, nt: (_te(t, te, nt), 0, _tf(t, f, nt))),
                pl.BlockSpec((1, d, tf), lambda t, f, te, nt: (_te(t, te, nt), 0, _tf(t, f, nt))),
                pl.BlockSpec((1, tf, d), lambda t, f, te, nt: (_te(t, te, nt), _tf(t, f, nt), 0)),
            ],
            out_specs=pl.BlockSpec((tm, d), lambda t, f, te, nt: (t, 0)),
            scratch_shapes=[pltpu.VMEM((tm, d), BF16), pltpu.VMEM((tm, d), F32)],
        ),
        out_shape=jax.ShapeDtypeStruct((m, d), F32),
        compiler_params=_params("arbitrary", "arbitrary"),
        name="swiglu_residual" if add_residual else "expert_swiglu",
    )(tile_expert, n_tiles, x, g, w_gate, w_up, w_down)


def _ple_tail(x, p_ref, wp_ref, wg_ref, g_e_ref, g_gate_ref, o_ref):
    e = _rms(_dot(p_ref[...].astype(BF16), wp_ref[...]), g_e_ref[...])
    gate = jax.nn.sigmoid(_dot(_rms(x, g_gate_ref[...]).astype(BF16), wg_ref[...]))
    o_ref[...] = x + gate * e


def _ple_kernel(x_ref, p_ref, wp_ref, wg_ref, g_e_ref, g_gate_ref, o_ref, wpb_ref, wgb_ref):
    @pl.when(pl.program_id(0) == 0)
    def _():
        wpb_ref[...] = wp_ref[...].astype(BF16)
        wgb_ref[...] = wg_ref[...].astype(BF16)

    _ple_tail(x_ref[...], p_ref, wpb_ref, wgb_ref, g_e_ref, g_gate_ref, o_ref)


def _combine_ple_kernel(pos_ref, pos_next_ref, x_ref, w_ref, ys_ref, p_ref, wp_ref, wg_ref, g_e_ref,
                        g_gate_ref, o_ref, wpb_ref, wgb_ref, ybuf_ref, sem, *, tc):
    i = pl.program_id(0)
    slot = i % 2

    def issue_tile(rows_ref, dst_slot):
        def issue(t, _):
            for k in range(2):
                pltpu.make_async_copy(ys_ref.at[pl.ds(rows_ref[k, t], 1), :],
                                      ybuf_ref.at[dst_slot, k, pl.ds(t, 1), :],
                                      sem.at[dst_slot]).start(priority=k)
            return 0
        lax.fori_loop(0, tc, issue, 0, unroll=8)

    @pl.when(i == 0)
    def _():
        wpb_ref[...] = wp_ref[...].astype(BF16)
        wgb_ref[...] = wg_ref[...].astype(BF16)
        issue_tile(pos_ref, 0)

    @pl.when(i + 1 < pl.num_programs(0))
    def _():
        issue_tile(pos_next_ref, 1 - slot)

    for k in range(2):
        pltpu.make_async_copy(ys_ref.at[pl.ds(0, tc), :], ybuf_ref.at[slot, k], sem.at[slot]).wait()
    w = w_ref[...]
    x = x_ref[...] + w[:, 0:1] * ybuf_ref[slot, 0] + w[:, 1:2] * ybuf_ref[slot, 1]
    _ple_tail(x, p_ref, wpb_ref, wgb_ref, g_e_ref, g_gate_ref, o_ref)


def _ple(x, p, wp, wg, g_e, g_gate, tm, routed=None):
    s, d = x.shape
    pd = p.shape[1]
    row = lambda i: (i, 0)
    fixed = lambda i: (0, 0)
    tail_specs = [
        pl.BlockSpec((tm, pd), row),
        pl.BlockSpec((pd, d), fixed),
        pl.BlockSpec((d, d), fixed),
        pl.BlockSpec((1, d), fixed),
        pl.BlockSpec((1, d), fixed),
    ]
    scratch = [pltpu.VMEM((pd, d), BF16), pltpu.VMEM((d, d), BF16)]
    if routed is None:
        return pl.pallas_call(
            _ple_kernel,
            grid=(s // tm,),
            in_specs=[pl.BlockSpec((tm, d), row)] + tail_specs,
            out_specs=pl.BlockSpec((tm, d), row),
            out_shape=jax.ShapeDtypeStruct((s, d), F32),
            scratch_shapes=scratch,
            compiler_params=_params("arbitrary"),
            name="ple",
        )(x, p, wp, wg, g_e, g_gate)
    pos, w12, ys = routed
    last = s // tm - 1
    return pl.pallas_call(
        functools.partial(_combine_ple_kernel, tc=tm),
        grid=(s // tm,),
        in_specs=[
            pl.BlockSpec((2, tm), lambda i: (0, i), memory_space=pltpu.SMEM),
            pl.BlockSpec((2, tm), lambda i: (0, jnp.minimum(i + 1, last)), memory_space=pltpu.SMEM),
            pl.BlockSpec((tm, d), row),
            pl.BlockSpec((tm, 2), row),
            pl.BlockSpec(memory_space=pl.ANY),
        ] + tail_specs,
        out_specs=pl.BlockSpec((tm, d), row),
        out_shape=jax.ShapeDtypeStruct((s, d), F32),
        scratch_shapes=scratch + [pltpu.VMEM((2, 2, tm, d), F32), pltpu.SemaphoreType.DMA((2,))],
        compiler_params=_params("arbitrary"),
        name="combine_ple",
    )(pos, pos, x, w12, ys, p, wp, wg, g_e, g_gate)


def _route(x, g_ref, wrt_ref, b_ref, tri_ref, carry_ref, info_ref, cnt_ref):
    h = _rms(x, g_ref[...])
    logits = lax.dot_general(wrt_ref[...], h, (((1,), (1,)), ((), ())),
                             preferred_element_type=F32,
                             precision=lax.Precision.HIGHEST) + b_ref[...]
    eidx = lax.broadcasted_iota(jnp.int32, logits.shape, 0)
    m1 = jnp.max(logits, axis=0, keepdims=True)
    e1 = jnp.min(jnp.where(logits == m1, eidx, N_EXPERTS), axis=0, keepdims=True)
    rest = jnp.where(eidx == e1, -jnp.inf, logits)
    m2 = jnp.max(rest, axis=0, keepdims=True)
    e2 = jnp.min(jnp.where(rest == m2, eidx, N_EXPERTS), axis=0, keepdims=True)
    z = jnp.exp(m2 - m1)
    w1 = 1.0 / (1.0 + z)
    w2 = z / (1.0 + z)
    sel1 = eidx == e1
    sel2 = eidx == e2
    sel = (sel1 | sel2).astype(F32)
    rank = carry_ref[:, 0:1] + _dot(sel.astype(BF16), tri_ref[...])
    r1 = jnp.sum(jnp.where(sel1, rank, 0.0), axis=0, keepdims=True)
    r2 = jnp.sum(jnp.where(sel2, rank, 0.0), axis=0, keepdims=True)
    rows = (e1.astype(F32), e2.astype(F32), w1, w2, r1, r2)
    info = jnp.zeros(logits.shape, F32)
    for k, v in enumerate(rows):
        info = jnp.where(eidx == k, v, info)
    info_ref[...] = info
    carry_ref[...] = carry_ref[...] + jnp.sum(sel, axis=1, keepdims=True)
    cnt_ref[...] = carry_ref[...]


def _conv_router_kernel(x_ref, g_ref, w1_ref, b1_ref, wdw_ref, bdw_ref, lng_ref, lnb_ref, w2_ref, b2_ref,
                        rg_ref, wrt_ref, rb_ref, o_ref, info_ref, cnt_ref,
                        ext_ref, c_ref, tri_ref, carry_ref, *, tm, rc):
    i = pl.program_id(0)
    d = x_ref.shape[1]

    @pl.when(i == 0)
    def _():
        ext_ref[0:HALO_ROWS, :] = jnp.zeros((HALO_ROWS, d), F32)
        a = lax.broadcasted_iota(jnp.int32, (tm, tm), 0)
        b = lax.broadcasted_iota(jnp.int32, (tm, tm), 1)
        tri_ref[...] = (a < b).astype(BF16)
        carry_ref[...] = jnp.zeros(carry_ref.shape, F32)

    @pl.when(i > 0)
    def _():
        ext_ref[0:HALO_ROWS, :] = ext_ref[tm:tm + HALO_ROWS, :]

    x = x_ref[...]
    y = _dot(_rms(x, g_ref[...]).astype(BF16), w1_ref[...]) + b1_ref[...]
    ext_ref[HALO_ROWS:, :] = y[:, :d] * jax.nn.sigmoid(y[:, d:])
    first = HALO_ROWS - (CONV_WIDTH - 1)
    win_rows = rc + HALO_ROWS

    def chunk(r, _):
        base = pl.multiple_of(r * rc, rc)
        for lo in range(0, d, LANES):
            cols = slice(lo, lo + LANES)
            win = ext_ref[pl.ds(base, win_rows), cols]
            acc = jnp.zeros((rc, LANES), F32) + bdw_ref[:, cols]
            for b in range(SUBLANES):
                wb = win if b == 0 else pltpu.roll(win, shift=win_rows - b, axis=0)
                for a in range(win_rows // SUBLANES):
                    t = SUBLANES * a + b - first
                    if 0 <= t < CONV_WIDTH:
                        acc = acc + wdw_ref[t:t + 1, cols] * wb[SUBLANES * a:SUBLANES * a + rc, :]
            c_ref[pl.ds(base, rc), cols] = acc
        return 0

    lax.fori_loop(0, tm // rc, chunk, 0)
    c = c_ref[...]
    mu = jnp.mean(c, axis=-1, keepdims=True)
    cc = c - mu
    y = cc * lax.rsqrt(jnp.mean(cc * cc, axis=-1, keepdims=True) + LN_EPS) * lng_ref[...] + lnb_ref[...]
    y = y * jax.nn.sigmoid(y)
    x1 = x + _dot(y.astype(BF16), w2_ref[...]) + b2_ref[...]
    o_ref[...] = x1
    _route(x1, rg_ref, wrt_ref, rb_ref, tri_ref, carry_ref, info_ref, cnt_ref)


def _conv_module_router(x, g, w1, b1, w_dw, b_dw, ln_g, ln_b, w2, b2, rg, w_router_t, rb_col, tm):
    s, d = x.shape
    row = lambda i: (i, 0)
    fixed = lambda i: (0, 0)
    return pl.pallas_call(
        functools.partial(_conv_router_kernel, tm=tm, rc=min(64, tm)),
        grid=(s // tm,),
        in_specs=[
            pl.BlockSpec((tm, d), row),
            pl.BlockSpec((1, d), fixed),
            pl.BlockSpec((d, 2 * d), fixed),
            pl.BlockSpec((1, 2 * d), fixed),
            pl.BlockSpec((CONV_WIDTH, d), fixed),
            pl.BlockSpec((1, d), fixed),
            pl.BlockSpec((1, d), fixed),
            pl.BlockSpec((1, d), fixed),
            pl.BlockSpec((d, d), fixed),
            pl.BlockSpec((1, d), fixed),
            pl.BlockSpec((1, d), fixed),
            pl.BlockSpec((N_EXPERTS, d), fixed),
            pl.BlockSpec((N_EXPERTS, 1), fixed),
        ],
        out_specs=[
            pl.BlockSpec((tm, d), row),
            pl.BlockSpec((N_EXPERTS, tm), lambda i: (0, i)),
            pl.BlockSpec((N_EXPERTS, LANES), fixed),
        ],
        out_shape=[jax.ShapeDtypeStruct((s, d), F32),
                   jax.ShapeDtypeStruct((N_EXPERTS, s), F32),
                   jax.ShapeDtypeStruct((N_EXPERTS, LANES), F32)],
        scratch_shapes=[pltpu.VMEM((tm + HALO_ROWS, d), F32), pltpu.VMEM((tm, d), F32),
                        pltpu.VMEM((tm, tm), BF16), pltpu.VMEM((N_EXPERTS, LANES), F32)],
        compiler_params=_params("arbitrary"),
        name="conv_module_router",
    )(x, g, w1, b1, w_dw, b_dw, ln_g, ln_b, w2, b2, rg, w_router_t, rb_col)


def _dispatch_kernel(pos_ref, x_ref, xs_in_ref, xs_ref, sem, *, td):
    del xs_in_ref

    def row_copy(k, t):
        return pltpu.make_async_copy(x_ref.at[pl.ds(t, 1), :],
                                     xs_ref.at[pl.ds(pos_ref[k, t], 1), :], sem)

    def issue(t, _):
        row_copy(0, t).start(priority=0)
        row_copy(1, t).start(priority=1)
        return 0

    lax.fori_loop(0, td, issue, 0, unroll=8)
    for _ in range(2):
        pltpu.make_async_copy(x_ref, xs_ref.at[pl.ds(0, td), :], sem).wait()


def _dispatch(x, pos, m_pad, td):
    s, d = x.shape
    return pl.pallas_call(
        functools.partial(_dispatch_kernel, td=td),
        grid=(s // td,),
        in_specs=[
            pl.BlockSpec((2, td), lambda i: (0, i), memory_space=pltpu.SMEM),
            pl.BlockSpec((td, d), lambda i: (i, 0)),
            pl.BlockSpec(memory_space=pl.ANY),
        ],
        out_specs=pl.BlockSpec(memory_space=pl.ANY),
        out_shape=jax.ShapeDtypeStruct((m_pad, d), F32),
        scratch_shapes=[pltpu.SemaphoreType.DMA(())],
        input_output_aliases={2: 0},
        compiler_params=_params("arbitrary"),
        name="moe_dispatch",
    )(pos, x, jnp.zeros((m_pad, d), F32))


def _tiles(s):
    t = lambda want: min(want, s)
    return dict(qkv=(t(1024), 512), attn=t(512), proj=t(1024), ffn=(t(1024), 512), ple=t(1024),
                conv=t(512), move=t(512), expert=(t(1024), 512))


def _layer_even(x, p, tiles, mix_g, ffn_g, w_qkv, w_o, g_q, g_k, lam_vecs, g_sub, lambda_init,
                w_gate, w_up, w_down, ple_wp, ple_g, ple_gate_g, ple_wg):
    s, d = x.shape
    q_gain = g_q * (ATTN_SCALE * LOG2_E)
    gain_row = jnp.concatenate([jnp.tile(q_gain.reshape(-1), N_HEADS),
                                jnp.tile(g_k.reshape(-1), N_HEADS)])[None, :]
    qk, vt = _qkv_proj(x, mix_g[None, :], w_qkv, gain_row, *tiles["qkv"])
    o = _diff_attention(qk, vt, q_gain, g_k, lam_vecs, g_sub[:, None], lambda_init, tiles["attn"])
    x = _proj_residual(x, o, w_o, tiles["proj"])
    tm, tf = tiles["ffn"]
    n_tiles = s // tm
    x = _grouped_swiglu(x, ffn_g[None, :], w_gate[None], w_up[None], w_down[None],
                        jnp.zeros((n_tiles,), jnp.int32), jnp.full((1,), n_tiles, jnp.int32),
                        tm, tf, add_residual=True)
    return _ple(x, p, ple_wp, ple_wg, ple_g[None, :], ple_gate_g[None, :], tiles["ple"])


def _layer_odd(x, p, tiles, mix_g, ffn_g, w_pw1, b_pw1, w_dw, b_dw, ln_g, ln_b, w_pw2, b_pw2,
               w_router, b_router, w_gate, w_up, w_down, ple_wp, ple_g, ple_gate_g, ple_wg):
    s, d = x.shape
    x, info, counts = _conv_module_router(
        x, mix_g[None, :], w_pw1.astype(BF16), b_pw1[None, :], w_dw, b_dw[None, :], ln_g[None, :],
        ln_b[None, :], w_pw2.astype(BF16), b_pw2[None, :], ffn_g[None, :], w_router.T,
        b_router[:, None], tiles["conv"])
    tm, tf = tiles["expert"]
    e12 = info[0:2].astype(jnp.int32)
    w12 = info[2:4].T
    r12 = info[4:6].astype(jnp.int32)
    cnt = counts[:, 0].astype(jnp.int32)
    group_tiles = (cnt + tm - 1) // tm
    tile_end = jnp.cumsum(group_tiles)
    offsets = (tile_end - group_tiles) * tm
    expert_ids = jnp.arange(N_EXPERTS, dtype=jnp.int32)
    onehot = e12[:, :, None] == expert_ids[None, None, :]
    pos = jnp.sum(jnp.where(onehot, offsets[None, None, :], 0), axis=-1) + r12
    max_tiles = (2 * s) // tm + N_EXPERTS
    tile_ids = jnp.arange(max_tiles, dtype=jnp.int32)
    tile_expert = jnp.minimum(jnp.sum((tile_end[None, :] <= tile_ids[:, None]).astype(jnp.int32), axis=1),
                              N_EXPERTS - 1)
    n_tiles = tile_end[-1:].astype(jnp.int32)
    xs = _dispatch(x, pos, max_tiles * tm, tiles["move"])
    ys = _grouped_swiglu(xs, ffn_g[None, :], w_gate, w_up, w_down, tile_expert, n_tiles,
                         tm, tf, add_residual=False)
    return _ple(x, p, ple_wp, ple_wg, ple_g[None, :], ple_gate_g[None, :], tiles["move"],
                routed=(pos, w12, ys))


def kernel(x, p, mix_norm_g, ffn_norm_g, attn_w_qkv, attn_w_o, attn_g_q, attn_g_k, attn_lam_q1, attn_lam_k1, attn_lam_q2, attn_lam_k2, attn_g_sub, conv_w_pw1, conv_b_pw1, conv_w_dw, conv_b_dw, conv_ln_g, conv_ln_b, conv_w_pw2, conv_b_pw2, ffn_w_gate, ffn_w_up, ffn_w_down, moe_w_router, moe_b_router, moe_w_gate, moe_w_up, moe_w_down, ple_w_proj, ple_norm_g, ple_gate_norm_g, ple_w_gate):
    depth, batch, s, _ = p.shape
    tiles = _tiles(s)
    outs = []
    for b in range(batch):
        xb = x[b]
        for i in range(depth):
            j = i // 2
            ple_args = (ple_w_proj[i], ple_norm_g[i], ple_gate_norm_g[i], ple_w_gate[i])
            if i % 2 == 0:
                lambda_init = 0.8 - 0.6 * math.exp(-0.3 * i)
                lam_vecs = jnp.stack([attn_lam_q1[j], attn_lam_k1[j], attn_lam_q2[j], attn_lam_k2[j]])
                xb = _layer_even(xb, p[i, b], tiles, mix_norm_g[i], ffn_norm_g[i], attn_w_qkv[j],
                                 attn_w_o[j], attn_g_q[j], attn_g_k[j], lam_vecs, attn_g_sub[j],
                                 lambda_init, ffn_w_gate[j], ffn_w_up[j], ffn_w_down[j], *ple_args)
            else:
                xb = _layer_odd(xb, p[i, b], tiles, mix_norm_g[i], ffn_norm_g[i], conv_w_pw1[j],
                                conv_b_pw1[j], conv_w_dw[j], conv_b_dw[j], conv_ln_g[j], conv_ln_b[j],
                                conv_w_pw2[j], conv_b_pw2[j], moe_w_router[j], moe_b_router[j],
                                moe_w_gate[j], moe_w_up[j], moe_w_down[j], *ple_args)
        outs.append(xb)
    return jnp.stack(outs)
```

```python
import functools
import math

import jax
import jax.numpy as jnp
from jax import lax
from jax.experimental import pallas as pl
from jax.experimental.pallas import tpu as pltpu

N_HEADS = 8
HEAD_DIM = 64
N_EXPERTS = 8
CONV_WIDTH = 31
NORM_EPS = 1e-6
LN_EPS = 1e-5
ATTN_SCALE = HEAD_DIM ** -0.5
LOG2_E = math.log2(math.e)

BF16 = jnp.bfloat16
F32 = jnp.float32
NEG_BIG = -1e30
VMEM_LIMIT_BYTES = 56 * 1024 * 1024
SUBLANES, LANES = 8, 128
MXU_DIM = 256
ATTN_UNROLL = 8
BOUND_SLACK = 1.01
MAX_FIXED_SHIFT = 40.0
HALO_ROWS = 32


def _params(*sem):
    return pltpu.CompilerParams(dimension_semantics=sem, vmem_limit_bytes=VMEM_LIMIT_BYTES)


def _rms(xf, g):
    return xf * lax.rsqrt(jnp.mean(xf * xf, axis=-1, keepdims=True) + NORM_EPS) * g


def _dot(a, b):
    return jnp.dot(a, b, preferred_element_type=F32)


def _dot_nt(a, b):
    return lax.dot_general(a, b, (((1,), (1,)), ((), ())), preferred_element_type=F32)


def _qkv_kernel(x_ref, g_ref, w_ref, gain_ref, bd_ref, qk_ref, vt_ref, h_ref, *, n_qk_tiles):
    j = pl.program_id(1)

    @pl.when(j == 0)
    def _():
        h_ref[...] = _rms(x_ref[...], g_ref[...]).astype(BF16)

    y = _dot(h_ref[...], w_ref[...].astype(BF16))

    @pl.when(j < n_qk_tiles)
    def _():
        y2 = (y * y).astype(BF16)
        width = bd_ref.shape[0]
        ss = jnp.concatenate([_dot(y2[:, c:c + width], bd_ref[...]) for c in range(0, y2.shape[1], width)],
                             axis=1)
        qk_ref[...] = (y * lax.rsqrt(ss * (1.0 / HEAD_DIM) + NORM_EPS) * gain_ref[...]).astype(BF16)

    @pl.when(j >= n_qk_tiles)
    def _():
        vt_ref[...] = y.T.astype(BF16)


def _qkv_proj(x, g, w, gain_row, tm, tn):
    s, d = x.shape
    n = w.shape[1]
    n_qk = 2 * n // 3
    n_qk_tiles = n_qk // tn
    lane = jnp.arange(MXU_DIM) // HEAD_DIM
    bd = (lane[:, None] == lane[None, :]).astype(BF16)
    return pl.pallas_call(
        functools.partial(_qkv_kernel, n_qk_tiles=n_qk_tiles),
        grid=(s // tm, n // tn),
        in_specs=[
            pl.BlockSpec((tm, d), lambda i, j: (i, 0)),
            pl.BlockSpec((1, d), lambda i, j: (0, 0)),
            pl.BlockSpec((d, tn), lambda i, j: (0, j)),
            pl.BlockSpec((1, tn), lambda i, j: (0, jnp.minimum(j, n_qk_tiles - 1))),
            pl.BlockSpec((MXU_DIM, MXU_DIM), lambda i, j: (0, 0)),
        ],
        out_specs=[
            pl.BlockSpec((tm, tn), lambda i, j: (i, jnp.minimum(j, n_qk_tiles - 1))),
            pl.BlockSpec((tn, tm), lambda i, j: (jnp.maximum(j - n_qk_tiles, 0), i)),
        ],
        out_shape=[jax.ShapeDtypeStruct((s, n_qk), BF16),
                   jax.ShapeDtypeStruct((n - n_qk, s), BF16)],
        scratch_shapes=[pltpu.VMEM((tm, d), BF16)],
        compiler_params=_params("parallel", "arbitrary"),
        name="qkv_proj",
    )(x, g, w, gain_row, bd)


def _attn_kernel(q_ref, k_ref, vt_ref, knorm_ref, lam_ref, gsub_ref, o_ref, qq_ref, acc_ref, m_ref,
                 l_ref, mask_ref, *, bq, lambda_init, unroll, online):
    i = pl.program_id(1)

    @pl.when((pl.program_id(0) == 0) & (i == 0))
    def _():
        key = lax.broadcasted_iota(jnp.int32, mask_ref.shape, 0)
        qry = lax.broadcasted_iota(jnp.int32, mask_ref.shape, 1)
        mask_ref[...] = jnp.where(key <= jnp.where(qry >= bq, qry - bq, qry), 0.0, NEG_BIG)

    q = q_ref[...]
    lane = lax.broadcasted_iota(jnp.int32, q.shape, 1)
    qq_ref[0:bq, :] = jnp.where(lane < HEAD_DIM, q, 0)
    qq_ref[bq:, :] = jnp.where(lane >= HEAD_DIM, q, 0)
    acc_ref[...] = jnp.zeros(acc_ref.shape, F32)
    l_ref[...] = jnp.zeros(l_ref.shape, F32)
    if online:
        m_ref[...] = jnp.full(m_ref.shape, NEG_BIG, F32)
    else:
        qq = qq_ref[...].astype(F32)
        q_norm2 = _dot_nt(jnp.ones((SUBLANES, 2 * HEAD_DIM), BF16), (qq * qq).astype(BF16))[0:1]
        m_ref[...] = jnp.sqrt(q_norm2) * knorm_ref[0] * BOUND_SLACK

    def block(j, masked):
        start = pl.multiple_of(j * bq, bq)
        kb = k_ref[pl.ds(start, bq), :]
        vb = vt_ref[:, pl.ds(start, bq)]
        s = _dot_nt(kb, qq_ref[...])
        if masked:
            s = s + mask_ref[...]
        if online:
            m_old = m_ref[...]
            m_new = jnp.maximum(m_old, jnp.max(s, axis=0, keepdims=True))
            alpha = jnp.exp2(m_old - m_new)
            p = jnp.exp2(s - m_new)
            l_ref[...] = alpha * l_ref[...] + jnp.sum(p, axis=0, keepdims=True)
            m_ref[...] = m_new
            acc_ref[...] = alpha * acc_ref[...] + _dot(vb, p.astype(BF16))
        else:
            p = jnp.exp2(s - m_ref[...])
            l_ref[...] += jnp.sum(p, axis=0, keepdims=True)
            acc_ref[...] += _dot(vb, p.astype(BF16))

    def group(g, _):
        for t in range(unroll):
            block(unroll * g + t, masked=False)
        return 0

    lax.fori_loop(0, i // unroll, group, 0)
    size = unroll // 2
    while size >= 1:
        @pl.when((i % unroll) & size != 0)
        def _(size=size):
            first = i - (i % (2 * size))
            for t in range(size):
                block(first + t, masked=False)
        size //= 2
    block(i, masked=True)

    l = l_ref[...]
    lam_vecs = lam_ref[...]
    lam = (jnp.exp(jnp.sum(lam_vecs[0:1] * lam_vecs[1:2], axis=-1, keepdims=True))
           - jnp.exp(jnp.sum(lam_vecs[2:3] * lam_vecs[3:4], axis=-1, keepdims=True))
           + lambda_init)
    o = acc_ref[:, 0:bq] / l[:, 0:bq] - lam * (acc_ref[:, bq:] / l[:, bq:])
    o = o * lax.rsqrt(jnp.mean(o * o, axis=0, keepdims=True) + NORM_EPS)
    o = o * (gsub_ref[...] * (1.0 - lambda_init))
    o_ref[...] = o.T.astype(BF16)


def _diff_attention(qk, vt, q_gain, k_gain, lam_vecs, gsub_col, lambda_init, bq):
    unit = math.sqrt(HEAD_DIM)
    q_norm = unit * jnp.max(jnp.abs(q_gain), axis=-1)
    k_norm = unit * jnp.max(jnp.abs(k_gain), axis=-1)
    fits = jnp.all(q_norm * k_norm * BOUND_SLACK <= MAX_FIXED_SHIFT)
    knorm = jnp.repeat(k_norm, bq)[None, None, :]

    def call(online):
        return _diff_attention_call(qk, vt, knorm, lam_vecs, gsub_col, lambda_init, bq, online)

    return lax.cond(fits, lambda: call(False), lambda: call(True))


def _diff_attention_call(qk, vt, knorm, lam_vecs, gsub_col, lambda_init, bq, online):
    s = qk.shape[0]
    hd2 = 2 * HEAD_DIM
    return pl.pallas_call(
        functools.partial(_attn_kernel, bq=bq, lambda_init=lambda_init, unroll=ATTN_UNROLL,
                          online=online),
        grid=(N_HEADS, s // bq),
        in_specs=[
            pl.BlockSpec((bq, hd2), lambda h, i: (i, h)),
            pl.BlockSpec((s, hd2), lambda h, i: (0, N_HEADS + h)),
            pl.BlockSpec((hd2, s), lambda h, i: (h, 0)),
            pl.BlockSpec((1, 1, 2 * bq), lambda h, i: (0, 0, 0)),
            pl.BlockSpec((4, HEAD_DIM), lambda h, i: (0, 0)),
            pl.BlockSpec((hd2, 1), lambda h, i: (0, 0)),
        ],
        out_specs=pl.BlockSpec((bq, hd2), lambda h, i: (i, h)),
        out_shape=jax.ShapeDtypeStruct((s, N_HEADS * hd2), BF16),
        scratch_shapes=[pltpu.VMEM((2 * bq, hd2), BF16), pltpu.VMEM((hd2, 2 * bq), F32),
                        pltpu.VMEM((1, 2 * bq), F32), pltpu.VMEM((1, 2 * bq), F32),
                        pltpu.VMEM((bq, 2 * bq), F32)],
        compiler_params=_params("arbitrary", "arbitrary"),
        name="diff_attn_online" if online else "diff_attn",
    )(qk, qk, vt, knorm, lam_vecs, gsub_col)


def _proj_res_kernel(x_ref, a_ref, w_ref, o_ref, wb_ref):
    @pl.when(pl.program_id(0) == 0)
    def _():
        wb_ref[...] = w_ref[...].astype(BF16)

    o_ref[...] = x_ref[...] + _dot(a_ref[...], wb_ref[...])


def _proj_residual(x, a, w, tm):
    s, d = x.shape
    k = a.shape[1]
    return pl.pallas_call(
        _proj_res_kernel,
        grid=(s // tm,),
        in_specs=[
            pl.BlockSpec((tm, d), lambda i: (i, 0)),
            pl.BlockSpec((tm, k), lambda i: (i, 0)),
            pl.BlockSpec((k, d), lambda i: (0, 0)),
        ],
        out_specs=pl.BlockSpec((tm, d), lambda i: (i, 0)),
        out_shape=jax.ShapeDtypeStruct((s, d), F32),
        scratch_shapes=[pltpu.VMEM((k, d), BF16)],
        compiler_params=_params("arbitrary"),
        name="attn_out_proj",
    )(x, a, w)


def _swiglu_kernel(te_ref, nt_ref, x_ref, g_ref, wg_ref, wu_ref, wd_ref, o_ref, h_ref, acc_ref,
                   *, add_residual):
    t = pl.program_id(0)
    f = pl.program_id(1)

    @pl.when(t < nt_ref[0])
    def _():
        @pl.when(f == 0)
        def _():
            h_ref[...] = _rms(x_ref[...], g_ref[...]).astype(BF16)
            acc_ref[...] = jnp.zeros(acc_ref.shape, F32)

        h = h_ref[...]
        gate = _dot(h, wg_ref[0].astype(BF16))
        up = _dot(h, wu_ref[0].astype(BF16))
        act = (gate * jax.nn.sigmoid(gate) * up).astype(BF16)
        acc_ref[...] += _dot(act, wd_ref[0].astype(BF16))

        @pl.when(f == pl.num_programs(1) - 1)
        def _():
            if add_residual:
                o_ref[...] = x_ref[...] + acc_ref[...]
            else:
                o_ref[...] = acc_ref[...]

    @pl.when((t >= nt_ref[0]) & (f == 0))
    def _():
        o_ref[...] = jnp.zeros(o_ref.shape, F32)


def _grouped_swiglu(x, g, w_gate, w_up, w_down, tile_expert, n_tiles, tm, tf, add_residual):
    m, d = x.shape
    dff = w_gate.shape[2]
    nf = dff // tf
    max_tiles = tile_expert.shape[0]

    def _last(nt):
        return jnp.maximum(nt[0] - 1, 0)

    def row_map(t, f, te, nt):
        return (jnp.minimum(t, _last(nt)), 0)

    def _tf(t, f, nt):
        return jnp.where(t < nt[0], f, nf - 1)

    def _te(t, te, nt):
        return te[jnp.minimum(t, _last(nt))]

    return pl.pallas_call(
        functools.partial(_swiglu_kernel, add_residual=add_residual),
        grid_spec=pltpu.PrefetchScalarGridSpec(
            num_scalar_prefetch=2,
            grid=(max_tiles, nf),
            in_specs=[
                pl.BlockSpec((tm, d), row_map),
                pl.BlockSpec((1, d), lambda t, f, te, nt: (0, 0)),
                pl.BlockSpec((1, d, tf), lambda t, f, te, nt: (_te(t, te, nt), 0, _tf(t, f, nt))),
                pl.BlockSpec((1, d, tf), lambda t, f, te, nt: (_te(t, te, nt), 0, _tf(t, f, nt))),
                pl.BlockSpec((1, tf, d), lambda t, f, te, nt: (_te(t, te, nt), _tf(t, f, nt), 0)),
            ],
            out_specs=pl.BlockSpec((tm, d), lambda t, f, te, nt: (t, 0)),
            scratch_shapes=[pltpu.VMEM((tm, d), BF16), pltpu.VMEM((tm, d), F32)],
        ),
        out_shape=jax.ShapeDtypeStruct((m, d), F32),
        compiler_params=_params("arbitrary", "arbitrary"),
        name="swiglu_residual" if add_residual else "expert_swiglu",
    )(tile_expert, n_tiles, x, g, w_gate, w_up, w_down)


def _ple_tail(x, p_ref, wp_ref, wg_ref, g_e_ref, g_gate_ref, o_ref):
    e = _rms(_dot(p_ref[...].astype(BF16), wp_ref[...]), g_e_ref[...])
    gate = jax.nn.sigmoid(_dot(_rms(x, g_gate_ref[...]).astype(BF16), wg_ref[...]))
    o_ref[...] = x + gate * e


def _ple_kernel(x_ref, p_ref, wp_ref, wg_ref, g_e_ref, g_gate_ref, o_ref, wpb_ref, wgb_ref):
    @pl.when(pl.program_id(0) == 0)
    def _():
        wpb_ref[...] = wp_ref[...].astype(BF16)
        wgb_ref[...] = wg_ref[...].astype(BF16)

    _ple_tail(x_ref[...], p_ref, wpb_ref, wgb_ref, g_e_ref, g_gate_ref, o_ref)


def _combine_ple_kernel(pos_ref, pos_next_ref, x_ref, w_ref, ys_ref, p_ref, wp_ref, wg_ref, g_e_ref,
                        g_gate_ref, o_ref, wpb_ref, wgb_ref, ybuf_ref, sem, *, tc):
    i = pl.program_id(0)
    slot = i % 2

    def issue_tile(rows_ref, dst_slot):
        def issue(t, _):
            for k in range(2):
                pltpu.make_async_copy(ys_ref.at[pl.ds(rows_ref[k, t], 1), :],
                                      ybuf_ref.at[dst_slot, k, pl.ds(t, 1), :],
                                      sem.at[dst_slot]).start(priority=k)
            return 0
        lax.fori_loop(0, tc, issue, 0, unroll=8)

    @pl.when(i == 0)
    def _():
        wpb_ref[...] = wp_ref[...].astype(BF16)
        wgb_ref[...] = wg_ref[...].astype(BF16)
        issue_tile(pos_ref, 0)

    @pl.when(i + 1 < pl.num_programs(0))
    def _():
        issue_tile(pos_next_ref, 1 - slot)

    for k in range(2):
        pltpu.make_async_copy(ys_ref.at[pl.ds(0, tc), :], ybuf_ref.at[slot, k], sem.at[slot]).wait()
    w = w_ref[...]
    x = x_ref[...] + w[:, 0:1] * ybuf_ref[slot, 0] + w[:, 1:2] * ybuf_ref[slot, 1]
    _ple_tail(x, p_ref, wpb_ref, wgb_ref, g_e_ref, g_gate_ref, o_ref)


def _ple(x, p, wp, wg, g_e, g_gate, tm, routed=None):
    s, d = x.shape
    pd = p.shape[1]
    row = lambda i: (i, 0)
    fixed = lambda i: (0, 0)
    tail_specs = [
        pl.BlockSpec((tm, pd), row),
        pl.BlockSpec((pd, d), fixed),
        pl.BlockSpec((d, d), fixed),
        pl.BlockSpec((1, d), fixed),
        pl.BlockSpec((1, d), fixed),
    ]
    scratch = [pltpu.VMEM((pd, d), BF16), pltpu.VMEM((d, d), BF16)]
    if routed is None:
        return pl.pallas_call(
            _ple_kernel,
            grid=(s // tm,),
            in_specs=[pl.BlockSpec((tm, d), row)] + tail_specs,
            out_specs=pl.BlockSpec((tm, d), row),
            out_shape=jax.ShapeDtypeStruct((s, d), F32),
            scratch_shapes=scratch,
            compiler_params=_params("arbitrary"),
            name="ple",
        )(x, p, wp, wg, g_e, g_gate)
    pos, w12, ys = routed
    last = s // tm - 1
    return pl.pallas_call(
        functools.partial(_combine_ple_kernel, tc=tm),
        grid=(s // tm,),
        in_specs=[
            pl.BlockSpec((2, tm), lambda i: (0, i), memory_space=pltpu.SMEM),
            pl.BlockSpec((2, tm), lambda i: (0, jnp.minimum(i + 1, last)), memory_space=pltpu.SMEM),
            pl.BlockSpec((tm, d), row),
            pl.BlockSpec((tm, 2), row),
            pl.BlockSpec(memory_space=pl.ANY),
        ] + tail_specs,
        out_specs=pl.BlockSpec((tm, d), row),
        out_shape=jax.ShapeDtypeStruct((s, d), F32),
        scratch_shapes=scratch + [pltpu.VMEM((2, 2, tm, d), F32), pltpu.SemaphoreType.DMA((2,))],
        compiler_params=_params("arbitrary"),
        name="combine_ple",
    )(pos, pos, x, w12, ys, p, wp, wg, g_e, g_gate)


def _route(x, g_ref, wrt_ref, b_ref, tri_ref, carry_ref, info_ref, cnt_ref):
    h = _rms(x, g_ref[...])
    logits = lax.dot_general(wrt_ref[...], h, (((1,), (1,)), ((), ())),
                             preferred_element_type=F32,
                             precision=lax.Precision.HIGHEST) + b_ref[...]
    eidx = lax.broadcasted_iota(jnp.int32, logits.shape, 0)
    m1 = jnp.max(logits, axis=0, keepdims=True)
    e1 = jnp.min(jnp.where(logits == m1, eidx, N_EXPERTS), axis=0, keepdims=True)
    rest = jnp.where(eidx == e1, -jnp.inf, logits)
    m2 = jnp.max(rest, axis=0, keepdims=True)
    e2 = jnp.min(jnp.where(rest == m2, eidx, N_EXPERTS), axis=0, keepdims=True)
    z = jnp.exp(m2 - m1)
    w1 = 1.0 / (1.0 + z)
    w2 = z / (1.0 + z)
    sel1 = eidx == e1
    sel2 = eidx == e2
    sel = (sel1 | sel2).astype(F32)
    rank = carry_ref[:, 0:1] + _dot(sel.astype(BF16), tri_ref[...])
    r1 = jnp.sum(jnp.where(sel1, rank, 0.0), axis=0, keepdims=True)
    r2 = jnp.sum(jnp.where(sel2, rank, 0.0), axis=0, keepdims=True)
    rows = (e1.astype(F32), e2.astype(F32), w1, w2, r1, r2)
    info = jnp.zeros(logits.shape, F32)
    for k, v in enumerate(rows):
        info = jnp.where(eidx == k, v, info)
    info_ref[...] = info
    carry_ref[...] = carry_ref[...] + jnp.sum(sel, axis=1, keepdims=True)
    cnt_ref[...] = carry_ref[...]


def _conv_router_kernel(x_ref, g_ref, w1_ref, b1_ref, wdw_ref, bdw_ref, lng_ref, lnb_ref, w2_ref, b2_ref,
                        rg_ref, wrt_ref, rb_ref, o_ref, info_ref, cnt_ref,
                        ext_ref, c_ref, tri_ref, carry_ref, *, tm, rc):
    i = pl.program_id(0)
    d = x_ref.shape[1]

    @pl.when(i == 0)
    def _():
        ext_ref[0:HALO_ROWS, :] = jnp.zeros((HALO_ROWS, d), F32)
        a = lax.broadcasted_iota(jnp.int32, (tm, tm), 0)
        b = lax.broadcasted_iota(jnp.int32, (tm, tm), 1)
        tri_ref[...] = (a < b).astype(BF16)
        carry_ref[...] = jnp.zeros(carry_ref.shape, F32)

    @pl.when(i > 0)
    def _():
        ext_ref[0:HALO_ROWS, :] = ext_ref[tm:tm + HALO_ROWS, :]

    x = x_ref[...]
    y = _dot(_rms(x, g_ref[...]).astype(BF16), w1_ref[...]) + b1_ref[...]
    ext_ref[HALO_ROWS:, :] = y[:, :d] * jax.nn.sigmoid(y[:, d:])
    first = HALO_ROWS - (CONV_WIDTH - 1)
    win_rows = rc + HALO_ROWS

    def chunk(r, _):
        base = pl.multiple_of(r * rc, rc)
        for lo in range(0, d, LANES):
            cols = slice(lo, lo + LANES)
            win = ext_ref[pl.ds(base, win_rows), cols]
            acc = jnp.zeros((rc, LANES), F32) + bdw_ref[:, cols]
            for b in range(SUBLANES):
                wb = win if b == 0 else pltpu.roll(win, shift=win_rows - b, axis=0)
                for a in range(win_rows // SUBLANES):
                    t = SUBLANES * a + b - first
                    if 0 <= t < CONV_WIDTH:
                        acc = acc + wdw_ref[t:t + 1, cols] * wb[SUBLANES * a:SUBLANES * a + rc, :]
            c_ref[pl.ds(base, rc), cols] = acc
        return 0

    lax.fori_loop(0, tm // rc, chunk, 0)
    c = c_ref[...]
    mu = jnp.mean(c, axis=-1, keepdims=True)
    cc = c - mu
    y = cc * lax.rsqrt(jnp.mean(cc * cc, axis=-1, keepdims=True) + LN_EPS) * lng_ref[...] + lnb_ref[...]
    y = y * jax.nn.sigmoid(y)
    x1 = x + _dot(y.astype(BF16), w2_ref[...]) + b2_ref[...]
    o_ref[...] = x1
    _route(x1, rg_ref, wrt_ref, rb_ref, tri_ref, carry_ref, info_ref, cnt_ref)


def _conv_module_router(x, g, w1, b1, w_dw, b_dw, ln_g, ln_b, w2, b2, rg, w_router_t, rb_col, tm):
    s, d = x.shape
    row = lambda i: (i, 0)
    fixed = lambda i: (0, 0)
    return pl.pallas_call(
        functools.partial(_conv_router_kernel, tm=tm, rc=min(64, tm)),
        grid=(s // tm,),
        in_specs=[
            pl.BlockSpec((tm, d), row),
            pl.BlockSpec((1, d), fixed),
            pl.BlockSpec((d, 2 * d), fixed),
            pl.BlockSpec((1, 2 * d), fixed),
            pl.BlockSpec((CONV_WIDTH, d), fixed),
            pl.BlockSpec((1, d), fixed),
            pl.BlockSpec((1, d), fixed),
            pl.BlockSpec((1, d), fixed),
            pl.BlockSpec((d, d), fixed),
            pl.BlockSpec((1, d), fixed),
            pl.BlockSpec((1, d), fixed),
            pl.BlockSpec((N_EXPERTS, d), fixed),
            pl.BlockSpec((N_EXPERTS, 1), fixed),
        ],
        out_specs=[
            pl.BlockSpec((tm, d), row),
            pl.BlockSpec((N_EXPERTS, tm), lambda i: (0, i)),
            pl.BlockSpec((N_EXPERTS, LANES), fixed),
        ],
        out_shape=[jax.ShapeDtypeStruct((s, d), F32),
                   jax.ShapeDtypeStruct((N_EXPERTS, s), F32),
                   jax.ShapeDtypeStruct((N_EXPERTS, LANES), F32)],
        scratch_shapes=[pltpu.VMEM((tm + HALO_ROWS, d), F32), pltpu.VMEM((tm, d), F32),
                        pltpu.VMEM((tm, tm), BF16), pltpu.VMEM((N_EXPERTS, LANES), F32)],
        compiler_params=_params("arbitrary"),
        name="conv_module_router",
    )(x, g, w1, b1, w_dw, b_dw, ln_g, ln_b, w2, b2, rg, w_router_t, rb_col)


def _dispatch_kernel(pos_ref, x_ref, xs_in_ref, xs_ref, sem, *, td):
    del xs_in_ref

    def row_copy(k, t):
        return pltpu.make_async_copy(x_ref.at[pl.ds(t, 1), :],
                                     xs_ref.at[pl.ds(pos_ref[k, t], 1), :], sem)

    def issue(t, _):
        row_copy(0, t).start(priority=0)
        row_copy(1, t).start(priority=1)
        return 0

    lax.fori_loop(0, td, issue, 0, unroll=8)
    for _ in range(2):
        pltpu.make_async_copy(x_ref, xs_ref.at[pl.ds(0, td), :], sem).wait()


def _dispatch(x, pos, m_pad, td):
    s, d = x.shape
    return pl.pallas_call(
        functools.partial(_dispatch_kernel, td=td),
        grid=(s // td,),
        in_specs=[
            pl.BlockSpec((2, td), lambda i: (0, i), memory_space=pltpu.SMEM),
            pl.BlockSpec((td, d), lambda i: (i, 0)),
            pl.BlockSpec(memory_space=pl.ANY),
        ],
        out_specs=pl.BlockSpec(memory_space=pl.ANY),
        out_shape=jax.ShapeDtypeStruct((m_pad, d), F32),
        scratch_shapes=[pltpu.SemaphoreType.DMA(())],
        input_output_aliases={2: 0},
        compiler_params=_params("arbitrary"),
        name="moe_dispatch",
    )(pos, x, jnp.zeros((m_pad, d), F32))


def _tiles(s):
    t = lambda want: min(want, s)
    return dict(qkv=(t(1024), 1024), attn=t(512), proj=t(1024), ffn=(t(1024), 512), ple=t(1024),
                conv=t(512), move=t(512), expert=(t(1024), 512))


def _layer_even(x, p, tiles, mix_g, ffn_g, w_qkv, w_o, g_q, g_k, lam_vecs, g_sub, lambda_init,
                w_gate, w_up, w_down, ple_wp, ple_g, ple_gate_g, ple_wg):
    s, d = x.shape
    q_gain = g_q * (ATTN_SCALE * LOG2_E)
    gain_row = jnp.concatenate([jnp.tile(q_gain.reshape(-1), N_HEADS),
                                jnp.tile(g_k.reshape(-1), N_HEADS)])[None, :]
    qk, vt = _qkv_proj(x, mix_g[None, :], w_qkv, gain_row, *tiles["qkv"])
    o = _diff_attention(qk, vt, q_gain, g_k, lam_vecs, g_sub[:, None], lambda_init, tiles["attn"])
    x = _proj_residual(x, o, w_o, tiles["proj"])
    tm, tf = tiles["ffn"]
    n_tiles = s // tm
    x = _grouped_swiglu(x, ffn_g[None, :], w_gate[None], w_up[None], w_down[None],
                        jnp.zeros((n_tiles,), jnp.int32), jnp.full((1,), n_tiles, jnp.int32),
                        tm, tf, add_residual=True)
    return _ple(x, p, ple_wp, ple_wg, ple_g[None, :], ple_gate_g[None, :], tiles["ple"])


def _layer_odd(x, p, tiles, mix_g, ffn_g, w_pw1, b_pw1, w_dw, b_dw, ln_g, ln_b, w_pw2, b_pw2,
               w_router, b_router, w_gate, w_up, w_down, ple_wp, ple_g, ple_gate_g, ple_wg):
    s, d = x.shape
    x, info, counts = _conv_module_router(
        x, mix_g[None, :], w_pw1.astype(BF16), b_pw1[None, :], w_dw, b_dw[None, :], ln_g[None, :],
        ln_b[None, :], w_pw2.astype(BF16), b_pw2[None, :], ffn_g[None, :], w_router.T,
        b_router[:, None], tiles["conv"])
    tm, tf = tiles["expert"]
    e12 = info[0:2].astype(jnp.int32)
    w12 = info[2:4].T
    r12 = info[4:6].astype(jnp.int32)
    cnt = counts[:, 0].astype(jnp.int32)
    group_tiles = (cnt + tm - 1) // tm
    tile_end = jnp.cumsum(group_tiles)
    offsets = (tile_end - group_tiles) * tm
    expert_ids = jnp.arange(N_EXPERTS, dtype=jnp.int32)
    onehot = e12[:, :, None] == expert_ids[None, None, :]
    pos = jnp.sum(jnp.where(onehot, offsets[None, None, :], 0), axis=-1) + r12
    max_tiles = (2 * s) // tm + N_EXPERTS
    tile_ids = jnp.arange(max_tiles, dtype=jnp.int32)
    tile_expert = jnp.minimum(jnp.sum((tile_end[None, :] <= tile_ids[:, None]).astype(jnp.int32), axis=1),
                              N_EXPERTS - 1)
    n_tiles = tile_end[-1:].astype(jnp.int32)
    xs = _dispatch(x, pos, max_tiles * tm, tiles["move"])
    ys = _grouped_swiglu(xs, ffn_g[None, :], w_gate, w_up, w_down, tile_expert, n_tiles,
                         tm, tf, add_residual=False)
    return _ple(x, p, ple_wp, ple_wg, ple_g[None, :], ple_gate_g[None, :], tiles["move"],
                routed=(pos, w12, ys))


def kernel(x, p, mix_norm_g, ffn_norm_g, attn_w_qkv, attn_w_o, attn_g_q, attn_g_k, attn_lam_q1, attn_lam_k1, attn_lam_q2, attn_lam_k2, attn_g_sub, conv_w_pw1, conv_b_pw1, conv_w_dw, conv_b_dw, conv_ln_g, conv_ln_b, conv_w_pw2, conv_b_pw2, ffn_w_gate, ffn_w_up, ffn_w_down, moe_w_router, moe_b_router, moe_w_gate, moe_w_up, moe_w_down, ple_w_proj, ple_norm_g, ple_gate_norm_g, ple_w_gate):
    depth, batch, s, _ = p.shape
    tiles = _tiles(s)
    outs = []
    for b in range(batch):
        xb = x[b]
        for i in range(depth):
            j = i // 2
            ple_args = (ple_w_proj[i], ple_norm_g[i], ple_gate_norm_g[i], ple_w_gate[i])
            if i % 2 == 0:
                lambda_init = 0.8 - 0.6 * math.exp(-0.3 * i)
                lam_vecs = jnp.stack([attn_lam_q1[j], attn_lam_k1[j], attn_lam_q2[j], attn_lam_k2[j]])
                xb = _layer_even(xb, p[i, b], tiles, mix_norm_g[i], ffn_norm_g[i], attn_w_qkv[j],
                                 attn_w_o[j], attn_g_q[j], attn_g_k[j], lam_vecs, attn_g_sub[j],
                                 lambda_init, ffn_w_gate[j], ffn_w_up[j], ffn_w_down[j], *ple_args)
            else:
                xb = _layer_odd(xb, p[i, b], tiles, mix_norm_g[i], ffn_norm_g[i], conv_w_pw1[j],
                                conv_b_pw1[j], conv_w_dw[j], conv_b_dw[j], conv_ln_g[j], conv_ln_b[j],
                                conv_w_pw2[j], conv_b_pw2[j], moe_w_router[j], moe_b_router[j],
                                moe_w_gate[j], moe_w_up[j], moe_w_down[j], *ple_args)
        outs.append(xb)
    return jnp.stack(outs)
```

```python
import functools
import math

import jax
import jax.numpy as jnp
from jax import lax
from jax.experimental import pallas as pl
from jax.experimental.pallas import tpu as pltpu

N_HEADS = 8
HEAD_DIM = 64
N_EXPERTS = 8
CONV_WIDTH = 31
NORM_EPS = 1e-6
LN_EPS = 1e-5
ATTN_SCALE = HEAD_DIM ** -0.5
LOG2_E = math.log2(math.e)

BF16 = jnp.bfloat16
F32 = jnp.float32
NEG_BIG = -1e30
VMEM_LIMIT_BYTES = 56 * 1024 * 1024
SUBLANES, LANES = 8, 128
MXU_DIM = 256
ATTN_UNROLL = 8
BOUND_SLACK = 1.01
MAX_FIXED_SHIFT = 40.0
HALO_ROWS = 32


def _params(*sem):
    return pltpu.CompilerParams(dimension_semantics=sem, vmem_limit_bytes=VMEM_LIMIT_BYTES)


def _rms(xf, g):
    return xf * lax.rsqrt(jnp.mean(xf * xf, axis=-1, keepdims=True) + NORM_EPS) * g


def _dot(a, b):
    return jnp.dot(a, b, preferred_element_type=F32)


def _dot_nt(a, b):
    return lax.dot_general(a, b, (((1,), (1,)), ((), ())), preferred_element_type=F32)


def _qkv_kernel(x_ref, g_ref, w_ref, gain_ref, bd_ref, qk_ref, vt_ref, h_ref, *, n_qk_tiles):
    j = pl.program_id(1)

    @pl.when(j == 0)
    def _():
        h_ref[...] = _rms(x_ref[...], g_ref[...]).astype(BF16)

    y = _dot(h_ref[...], w_ref[...].astype(BF16))

    @pl.when(j < n_qk_tiles)
    def _():
        y2 = (y * y).astype(BF16)
        width = bd_ref.shape[0]
        ss = jnp.concatenate([_dot(y2[:, c:c + width], bd_ref[...]) for c in range(0, y2.shape[1], width)],
                             axis=1)
        qk_ref[...] = (y * lax.rsqrt(ss * (1.0 / HEAD_DIM) + NORM_EPS) * gain_ref[...]).astype(BF16)

    @pl.when(j >= n_qk_tiles)
    def _():
        vt_ref[...] = y.T.astype(BF16)


def _qkv_proj(x, g, w, gain_row, tm, tn):
    s, d = x.shape
    n = w.shape[1]
    n_qk = 2 * n // 3
    n_qk_tiles = n_qk // tn
    lane = jnp.arange(MXU_DIM) // HEAD_DIM
    bd = (lane[:, None] == lane[None, :]).astype(BF16)
    return pl.pallas_call(
        functools.partial(_qkv_kernel, n_qk_tiles=n_qk_tiles),
        grid=(s // tm, n // tn),
        in_specs=[
            pl.BlockSpec((tm, d), lambda i, j: (i, 0)),
            pl.BlockSpec((1, d), lambda i, j: (0, 0)),
            pl.BlockSpec((d, tn), lambda i, j: (0, j)),
            pl.BlockSpec((1, tn), lambda i, j: (0, jnp.minimum(j, n_qk_tiles - 1))),
            pl.BlockSpec((MXU_DIM, MXU_DIM), lambda i, j: (0, 0)),
        ],
        out_specs=[
            pl.BlockSpec((tm, tn), lambda i, j: (i, jnp.minimum(j, n_qk_tiles - 1))),
            pl.BlockSpec((tn, tm), lambda i, j: (jnp.maximum(j - n_qk_tiles, 0), i)),
        ],
        out_shape=[jax.ShapeDtypeStruct((s, n_qk), BF16),
                   jax.ShapeDtypeStruct((n - n_qk, s), BF16)],
        scratch_shapes=[pltpu.VMEM((tm, d), BF16)],
        compiler_params=_params("parallel", "arbitrary"),
        name="qkv_proj",
    )(x, g, w, gain_row, bd)


def _attn_kernel(q_ref, k_ref, vt_ref, knorm_ref, lam_ref, gsub_ref, o_ref, qq_ref, acc_ref, m_ref,
                 l_ref, mask_ref, *, bq, lambda_init, unroll, online):
    i = pl.program_id(1)

    @pl.when((pl.program_id(0) == 0) & (i == 0))
    def _():
        key = lax.broadcasted_iota(jnp.int32, mask_ref.shape, 0)
        qry = lax.broadcasted_iota(jnp.int32, mask_ref.shape, 1)
        mask_ref[...] = jnp.where(key <= jnp.where(qry >= bq, qry - bq, qry), 0.0, NEG_BIG)

    q = q_ref[...]
    lane = lax.broadcasted_iota(jnp.int32, q.shape, 1)
    qq_ref[0:bq, :] = jnp.where(lane < HEAD_DIM, q, 0)
    qq_ref[bq:, :] = jnp.where(lane >= HEAD_DIM, q, 0)
    acc_ref[...] = jnp.zeros(acc_ref.shape, F32)
    l_ref[...] = jnp.zeros(l_ref.shape, F32)
    if online:
        m_ref[...] = jnp.full(m_ref.shape, NEG_BIG, F32)
    else:
        qq = qq_ref[...].astype(F32)
        q_norm2 = _dot_nt(jnp.ones((SUBLANES, 2 * HEAD_DIM), BF16), (qq * qq).astype(BF16))[0:1]
        m_ref[...] = jnp.sqrt(q_norm2) * knorm_ref[0] * BOUND_SLACK

    def block(j, masked):
        start = pl.multiple_of(j * bq, bq)
        kb = k_ref[pl.ds(start, bq), :]
        vb = vt_ref[:, pl.ds(start, bq)]
        s = _dot_nt(kb, qq_ref[...])
        if masked:
            s = s + mask_ref[...]
        if online:
            m_old = m_ref[...]
            m_new = jnp.maximum(m_old, jnp.max(s, axis=0, keepdims=True))
            alpha = jnp.exp2(m_old - m_new)
            p = jnp.exp2(s - m_new)
            l_ref[...] = alpha * l_ref[...] + jnp.sum(p, axis=0, keepdims=True)
            m_ref[...] = m_new
            acc_ref[...] = alpha * acc_ref[...] + _dot(vb, p.astype(BF16))
        else:
            p = jnp.exp2(s - m_ref[...])
            l_ref[...] += jnp.sum(p, axis=0, keepdims=True)
            acc_ref[...] += _dot(vb, p.astype(BF16))

    def group(g, _):
        for t in range(unroll):
            block(unroll * g + t, masked=False)
        return 0

    lax.fori_loop(0, i // unroll, group, 0)
    size = unroll // 2
    while size >= 1:
        @pl.when((i % unroll) & size != 0)
        def _(size=size):
            first = i - (i % (2 * size))
            for t in range(size):
                block(first + t, masked=False)
        size //= 2
    block(i, masked=True)

    l = l_ref[...]
    lam_vecs = lam_ref[...]
    lam = (jnp.exp(jnp.sum(lam_vecs[0:1] * lam_vecs[1:2], axis=-1, keepdims=True))
           - jnp.exp(jnp.sum(lam_vecs[2:3] * lam_vecs[3:4], axis=-1, keepdims=True))
           + lambda_init)
    o = acc_ref[:, 0:bq] / l[:, 0:bq] - lam * (acc_ref[:, bq:] / l[:, bq:])
    o = o * lax.rsqrt(jnp.mean(o * o, axis=0, keepdims=True) + NORM_EPS)
    o = o * (gsub_ref[...] * (1.0 - lambda_init))
    o_ref[...] = o.T.astype(BF16)


def _diff_attention(qk, vt, q_gain, k_gain, lam_vecs, gsub_col, lambda_init, bq):
    unit = math.sqrt(HEAD_DIM)
    q_norm = unit * jnp.max(jnp.abs(q_gain), axis=-1)
    k_norm = unit * jnp.max(jnp.abs(k_gain), axis=-1)
    fits = jnp.all(q_norm * k_norm * BOUND_SLACK <= MAX_FIXED_SHIFT)
    knorm = jnp.repeat(k_norm, bq)[None, None, :]

    def call(online):
        return _diff_attention_call(qk, vt, knorm, lam_vecs, gsub_col, lambda_init, bq, online)

    return lax.cond(fits, lambda: call(False), lambda: call(True))


def _diff_attention_call(qk, vt, knorm, lam_vecs, gsub_col, lambda_init, bq, online):
    s = qk.shape[0]
    hd2 = 2 * HEAD_DIM
    return pl.pallas_call(
        functools.partial(_attn_kernel, bq=bq, lambda_init=lambda_init, unroll=ATTN_UNROLL,
                          online=online),
        grid=(N_HEADS, s // bq),
        in_specs=[
            pl.BlockSpec((bq, hd2), lambda h, i: (i, h)),
            pl.BlockSpec((s, hd2), lambda h, i: (0, N_HEADS + h)),
            pl.BlockSpec((hd2, s), lambda h, i: (h, 0)),
            pl.BlockSpec((1, 1, 2 * bq), lambda h, i: (0, 0, 0)),
            pl.BlockSpec((4, HEAD_DIM), lambda h, i: (0, 0)),
            pl.BlockSpec((hd2, 1), lambda h, i: (0, 0)),
        ],
        out_specs=pl.BlockSpec((bq, hd2), lambda h, i: (i, h)),
        out_shape=jax.ShapeDtypeStruct((s, N_HEADS * hd2), BF16),
        scratch_shapes=[pltpu.VMEM((2 * bq, hd2), BF16), pltpu.VMEM((hd2, 2 * bq), F32),
                        pltpu.VMEM((1, 2 * bq), F32), pltpu.VMEM((1, 2 * bq), F32),
                        pltpu.VMEM((bq, 2 * bq), F32)],
        compiler_params=_params("arbitrary", "arbitrary"),
        name="diff_attn_online" if online else "diff_attn",
    )(qk, qk, vt, knorm, lam_vecs, gsub_col)


def _proj_res_kernel(x_ref, a_ref, w_ref, o_ref, wb_ref):
    @pl.when(pl.program_id(0) == 0)
    def _():
        wb_ref[...] = w_ref[...].astype(BF16)

    o_ref[...] = x_ref[...] + _dot(a_ref[...], wb_ref[...])


def _proj_residual(x, a, w, tm):
    s, d = x.shape
    k = a.shape[1]
    return pl.pallas_call(
        _proj_res_kernel,
        grid=(s // tm,),
        in_specs=[
            pl.BlockSpec((tm, d), lambda i: (i, 0)),
            pl.BlockSpec((tm, k), lambda i: (i, 0)),
            pl.BlockSpec((k, d), lambda i: (0, 0)),
        ],
        out_specs=pl.BlockSpec((tm, d), lambda i: (i, 0)),
        out_shape=jax.ShapeDtypeStruct((s, d), F32),
        scratch_shapes=[pltpu.VMEM((k, d), BF16)],
        compiler_params=_params("arbitrary"),
        name="attn_out_proj",
    )(x, a, w)


def _swiglu_kernel(te_ref, nt_ref, x_ref, g_ref, wg_ref, wu_ref, wd_ref, o_ref, h_ref, acc_ref,
                   *, add_residual):
    t = pl.program_id(0)
    f = pl.program_id(1)

    @pl.when(t < nt_ref[0])
    def _():
        @pl.when(f == 0)
        def _():
            h_ref[...] = _rms(x_ref[...], g_ref[...]).astype(BF16)
            acc_ref[...] = jnp.zeros(acc_ref.shape, F32)

        h = h_ref[...]
        gate = _dot(h, wg_ref[0].astype(BF16))
        up = _dot(h, wu_ref[0].astype(BF16))
        act = (gate * jax.nn.sigmoid(gate) * up).astype(BF16)
        acc_ref[...] += _dot(act, wd_ref[0].astype(BF16))

        @pl.when(f == pl.num_programs(1) - 1)
        def _():
            if add_residual:
                o_ref[...] = x_ref[...] + acc_ref[...]
            else:
                o_ref[...] = acc_ref[...]

    @pl.when((t >= nt_ref[0]) & (f == 0))
    def _():
        o_ref[...] = jnp.zeros(o_ref.shape, F32)


def _grouped_swiglu(x, g, w_gate, w_up, w_down, tile_expert, n_tiles, tm, tf, add_residual):
    m, d = x.shape
    dff = w_gate.shape[2]
    nf = dff // tf
    max_tiles = tile_expert.shape[0]

    def _last(nt):
        return jnp.maximum(nt[0] - 1, 0)

    def row_map(t, f, te, nt):
        return (jnp.minimum(t, _last(nt)), 0)

    def _tf(t, f, nt):
        return jnp.where(t < nt[0], f, nf - 1)

    def _te(t, te, nt):
        return te[jnp.minimum(t, _last(nt))]

    return pl.pallas_call(
        functools.partial(_swiglu_kernel, add_residual=add_residual),
        grid_spec=pltpu.PrefetchScalarGridSpec(
            num_scalar_prefetch=2,
            grid=(max_tiles, nf),
            in_specs=[
                pl.BlockSpec((tm, d), row_map),
                pl.BlockSpec((1, d), lambda t, f, te, nt: (0, 0)),
                pl.BlockSpec((1, d, tf), lambda t, f, te, nt: (_te(t, te, nt), 0, _tf(t, f, nt))),
                pl.BlockSpec((1, d, tf), lambda t, f, te, nt: (_te(t, te, nt), 0, _tf(t, f, nt))),
                pl.BlockSpec((1, tf, d), lambda t, f, te, nt: (_te(t, te, nt), _tf(t, f, nt), 0)),
            ],
            out_specs=pl.BlockSpec((tm, d), lambda t, f, te, nt: (t, 0)),
            scratch_shapes=[pltpu.VMEM((tm, d), BF16), pltpu.VMEM((tm, d), F32)],
        ),
        out_shape=jax.ShapeDtypeStruct((m, d), F32),
        compiler_params=_params("arbitrary", "arbitrary"),
        name="swiglu_residual" if add_residual else "expert_swiglu",
    )(tile_expert, n_tiles, x, g, w_gate, w_up, w_down)


def _ple_tail(x, p_ref, wp_ref, wg_ref, g_e_ref, g_gate_ref, o_ref):
    e = _rms(_dot(p_ref[...].astype(BF16), wp_ref[...]), g_e_ref[...])
    gate = jax.nn.sigmoid(_dot(_rms(x, g_gate_ref[...]).astype(BF16), wg_ref[...]))
    o_ref[...] = x + gate * e


def _ple_kernel(x_ref, p_ref, wp_ref, wg_ref, g_e_ref, g_gate_ref, o_ref, wpb_ref, wgb_ref):
    @pl.when(pl.program_id(0) == 0)
    def _():
        wpb_ref[...] = wp_ref[...].astype(BF16)
        wgb_ref[...] = wg_ref[...].astype(BF16)

    _ple_tail(x_ref[...], p_ref, wpb_ref, wgb_ref, g_e_ref, g_gate_ref, o_ref)


def _combine_ple_kernel(pos_ref, pos_next_ref, x_ref, w_ref, ys_ref, p_ref, wp_ref, wg_ref, g_e_ref,
                        g_gate_ref, o_ref, wpb_ref, wgb_ref, ybuf_ref, sem, *, tc):
    i = pl.program_id(0)
    slot = i % 2

    def issue_tile(rows_ref, dst_slot):
        def issue(t, _):
            for k in range(2):
                pltpu.make_async_copy(ys_ref.at[pl.ds(rows_ref[k, t], 1), :],
                                      ybuf_ref.at[dst_slot, k, pl.ds(t, 1), :],
                                      sem.at[dst_slot]).start(priority=k)
            return 0
        lax.fori_loop(0, tc, issue, 0, unroll=8)

    @pl.when(i == 0)
    def _():
        wpb_ref[...] = wp_ref[...].astype(BF16)
        wgb_ref[...] = wg_ref[...].astype(BF16)
        issue_tile(pos_ref, 0)

    @pl.when(i + 1 < pl.num_programs(0))
    def _():
        issue_tile(pos_next_ref, 1 - slot)

    for k in range(2):
        pltpu.make_async_copy(ys_ref.at[pl.ds(0, tc), :], ybuf_ref.at[slot, k], sem.at[slot]).wait()
    w = w_ref[...]
    x = x_ref[...] + w[:, 0:1] * ybuf_ref[slot, 0] + w[:, 1:2] * ybuf_ref[slot, 1]
    _ple_tail(x, p_ref, wpb_ref, wgb_ref, g_e_ref, g_gate_ref, o_ref)


def _ple(x, p_layer, wp, wg, g_e, g_gate, tm, routed=None):
    s, d = x.shape
    p, layer, batch = p_layer
    pd = p.shape[-1]
    row = lambda i: (i, 0)
    fixed = lambda i: (0, 0)
    tail_specs = [
        pl.BlockSpec((None, None, tm, pd), lambda i: (layer, batch, i, 0)),
        pl.BlockSpec((pd, d), fixed),
        pl.BlockSpec((d, d), fixed),
        pl.BlockSpec((1, d), fixed),
        pl.BlockSpec((1, d), fixed),
    ]
    scratch = [pltpu.VMEM((pd, d), BF16), pltpu.VMEM((d, d), BF16)]
    if routed is None:
        return pl.pallas_call(
            _ple_kernel,
            grid=(s // tm,),
            in_specs=[pl.BlockSpec((tm, d), row)] + tail_specs,
            out_specs=pl.BlockSpec((tm, d), row),
            out_shape=jax.ShapeDtypeStruct((s, d), F32),
            scratch_shapes=scratch,
            compiler_params=_params("arbitrary"),
            name="ple",
        )(x, p, wp, wg, g_e, g_gate)
    pos, w12, ys = routed
    last = s // tm - 1
    return pl.pallas_call(
        functools.partial(_combine_ple_kernel, tc=tm),
        grid=(s // tm,),
        in_specs=[
            pl.BlockSpec((2, tm), lambda i: (0, i), memory_space=pltpu.SMEM),
            pl.BlockSpec((2, tm), lambda i: (0, jnp.minimum(i + 1, last)), memory_space=pltpu.SMEM),
            pl.BlockSpec((tm, d), row),
            pl.BlockSpec((tm, 2), row),
            pl.BlockSpec(memory_space=pl.ANY),
        ] + tail_specs,
        out_specs=pl.BlockSpec((tm, d), row),
        out_shape=jax.ShapeDtypeStruct((s, d), F32),
        scratch_shapes=scratch + [pltpu.VMEM((2, 2, tm, d), F32), pltpu.SemaphoreType.DMA((2,))],
        compiler_params=_params("arbitrary"),
        name="combine_ple",
    )(pos, pos, x, w12, ys, p, wp, wg, g_e, g_gate)


def _route(x, g_ref, wrt_ref, b_ref, tri_ref, carry_ref, info_ref, cnt_ref):
    h = _rms(x, g_ref[...])
    logits = lax.dot_general(wrt_ref[...], h, (((1,), (1,)), ((), ())),
                             preferred_element_type=F32,
                             precision=lax.Precision.HIGHEST) + b_ref[...]
    eidx = lax.broadcasted_iota(jnp.int32, logits.shape, 0)
    m1 = jnp.max(logits, axis=0, keepdims=True)
    e1 = jnp.min(jnp.where(logits == m1, eidx, N_EXPERTS), axis=0, keepdims=True)
    rest = jnp.where(eidx == e1, -jnp.inf, logits)
    m2 = jnp.max(rest, axis=0, keepdims=True)
    e2 = jnp.min(jnp.where(rest == m2, eidx, N_EXPERTS), axis=0, keepdims=True)
    z = jnp.exp(m2 - m1)
    w1 = 1.0 / (1.0 + z)
    w2 = z / (1.0 + z)
    sel1 = eidx == e1
    sel2 = eidx == e2
    sel = (sel1 | sel2).astype(F32)
    rank = carry_ref[:, 0:1] + _dot(sel.astype(BF16), tri_ref[...])
    r1 = jnp.sum(jnp.where(sel1, rank, 0.0), axis=0, keepdims=True)
    r2 = jnp.sum(jnp.where(sel2, rank, 0.0), axis=0, keepdims=True)
    rows = (e1.astype(F32), e2.astype(F32), w1, w2, r1, r2)
    info = jnp.zeros(logits.shape, F32)
    for k, v in enumerate(rows):
        info = jnp.where(eidx == k, v, info)
    info_ref[...] = info
    carry_ref[...] = carry_ref[...] + jnp.sum(sel, axis=1, keepdims=True)
    cnt_ref[...] = carry_ref[...]


def _conv_router_kernel(x_ref, g_ref, w1_ref, b1_ref, wdw_ref, bdw_ref, lng_ref, lnb_ref, w2_ref, b2_ref,
                        rg_ref, wrt_ref, rb_ref, o_ref, info_ref, cnt_ref,
                        ext_ref, c_ref, tri_ref, carry_ref, *, tm, rc):
    i = pl.program_id(0)
    d = x_ref.shape[1]

    @pl.when(i == 0)
    def _():
        ext_ref[0:HALO_ROWS, :] = jnp.zeros((HALO_ROWS, d), F32)
        a = lax.broadcasted_iota(jnp.int32, (tm, tm), 0)
        b = lax.broadcasted_iota(jnp.int32, (tm, tm), 1)
        tri_ref[...] = (a < b).astype(BF16)
        carry_ref[...] = jnp.zeros(carry_ref.shape, F32)

    @pl.when(i > 0)
    def _():
        ext_ref[0:HALO_ROWS, :] = ext_ref[tm:tm + HALO_ROWS, :]

    x = x_ref[...]
    y = _dot(_rms(x, g_ref[...]).astype(BF16), w1_ref[...]) + b1_ref[...]
    ext_ref[HALO_ROWS:, :] = y[:, :d] * jax.nn.sigmoid(y[:, d:])
    first = HALO_ROWS - (CONV_WIDTH - 1)
    win_rows = rc + HALO_ROWS

    def chunk(r, _):
        base = pl.multiple_of(r * rc, rc)
        for lo in range(0, d, LANES):
            cols = slice(lo, lo + LANES)
            win = ext_ref[pl.ds(base, win_rows), cols]
            acc = jnp.zeros((rc, LANES), F32) + bdw_ref[:, cols]
            for b in range(SUBLANES):
                wb = win if b == 0 else pltpu.roll(win, shift=win_rows - b, axis=0)
                for a in range(win_rows // SUBLANES):
                    t = SUBLANES * a + b - first
                    if 0 <= t < CONV_WIDTH:
                        acc = acc + wdw_ref[t:t + 1, cols] * wb[SUBLANES * a:SUBLANES * a + rc, :]
            c_ref[pl.ds(base, rc), cols] = acc
        return 0

    lax.fori_loop(0, tm // rc, chunk, 0)
    c = c_ref[...]
    mu = jnp.mean(c, axis=-1, keepdims=True)
    cc = c - mu
    y = cc * lax.rsqrt(jnp.mean(cc * cc, axis=-1, keepdims=True) + LN_EPS) * lng_ref[...] + lnb_ref[...]
    y = y * jax.nn.sigmoid(y)
    x1 = x + _dot(y.astype(BF16), w2_ref[...]) + b2_ref[...]
    o_ref[...] = x1
    _route(x1, rg_ref, wrt_ref, rb_ref, tri_ref, carry_ref, info_ref, cnt_ref)


def _conv_module_router(x, g, w1, b1, w_dw, b_dw, ln_g, ln_b, w2, b2, rg, w_router_t, rb_col, tm):
    s, d = x.shape
    row = lambda i: (i, 0)
    fixed = lambda i: (0, 0)
    return pl.pallas_call(
        functools.partial(_conv_router_kernel, tm=tm, rc=min(64, tm)),
        grid=(s // tm,),
        in_specs=[
            pl.BlockSpec((tm, d), row),
            pl.BlockSpec((1, d), fixed),
            pl.BlockSpec((d, 2 * d), fixed),
            pl.BlockSpec((1, 2 * d), fixed),
            pl.BlockSpec((CONV_WIDTH, d), fixed),
            pl.BlockSpec((1, d), fixed),
            pl.BlockSpec((1, d), fixed),
            pl.BlockSpec((1, d), fixed),
            pl.BlockSpec((d, d), fixed),
            pl.BlockSpec((1, d), fixed),
            pl.BlockSpec((1, d), fixed),
            pl.BlockSpec((N_EXPERTS, d), fixed),
            pl.BlockSpec((N_EXPERTS, 1), fixed),
        ],
        out_specs=[
            pl.BlockSpec((tm, d), row),
            pl.BlockSpec((N_EXPERTS, tm), lambda i: (0, i)),
            pl.BlockSpec((N_EXPERTS, LANES), fixed),
        ],
        out_shape=[jax.ShapeDtypeStruct((s, d), F32),
                   jax.ShapeDtypeStruct((N_EXPERTS, s), F32),
                   jax.ShapeDtypeStruct((N_EXPERTS, LANES), F32)],
        scratch_shapes=[pltpu.VMEM((tm + HALO_ROWS, d), F32), pltpu.VMEM((tm, d), F32),
                        pltpu.VMEM((tm, tm), BF16), pltpu.VMEM((N_EXPERTS, LANES), F32)],
        compiler_params=_params("arbitrary"),
        name="conv_module_router",
    )(x, g, w1, b1, w_dw, b_dw, ln_g, ln_b, w2, b2, rg, w_router_t, rb_col)


def _dispatch_kernel(pos_ref, x_ref, xs_in_ref, xs_ref, sem, *, td):
    del xs_in_ref

    def row_copy(k, t):
        return pltpu.make_async_copy(x_ref.at[pl.ds(t, 1), :],
                                     xs_ref.at[pl.ds(pos_ref[k, t], 1), :], sem)

    def issue(t, _):
        row_copy(0, t).start(priority=0)
        row_copy(1, t).start(priority=1)
        return 0

    lax.fori_loop(0, td, issue, 0, unroll=8)
    for _ in range(2):
        pltpu.make_async_copy(x_ref, xs_ref.at[pl.ds(0, td), :], sem).wait()


def _dispatch(x, pos, m_pad, td):
    s, d = x.shape
    return pl.pallas_call(
        functools.partial(_dispatch_kernel, td=td),
        grid=(s // td,),
        in_specs=[
            pl.BlockSpec((2, td), lambda i: (0, i), memory_space=pltpu.SMEM),
            pl.BlockSpec((td, d), lambda i: (i, 0)),
            pl.BlockSpec(memory_space=pl.ANY),
        ],
        out_specs=pl.BlockSpec(memory_space=pl.ANY),
        out_shape=jax.ShapeDtypeStruct((m_pad, d), F32),
        scratch_shapes=[pltpu.SemaphoreType.DMA(())],
        input_output_aliases={2: 0},
        compiler_params=_params("arbitrary"),
        name="moe_dispatch",
    )(pos, x, jnp.zeros((m_pad, d), F32))


def _tiles(s):
    t = lambda want: min(want, s)
    return dict(qkv=(t(1024), 1024), attn=t(512), proj=t(1024), ffn=(t(1024), 512), ple=t(1024),
                conv=t(1024), move=t(512), expert=(t(1024), 512))


def _layer_even(x, p, tiles, mix_g, ffn_g, w_qkv, w_o, g_q, g_k, lam_vecs, g_sub, lambda_init,
                w_gate, w_up, w_down, ple_wp, ple_g, ple_gate_g, ple_wg):
    s, d = x.shape
    q_gain = g_q * (ATTN_SCALE * LOG2_E)
    gain_row = jnp.concatenate([jnp.tile(q_gain.reshape(-1), N_HEADS),
                                jnp.tile(g_k.reshape(-1), N_HEADS)])[None, :]
    qk, vt = _qkv_proj(x, mix_g[None, :], w_qkv, gain_row, *tiles["qkv"])
    o = _diff_attention(qk, vt, q_gain, g_k, lam_vecs, g_sub[:, None], lambda_init, tiles["attn"])
    x = _proj_residual(x, o, w_o, tiles["proj"])
    tm, tf = tiles["ffn"]
    n_tiles = s // tm
    x = _grouped_swiglu(x, ffn_g[None, :], w_gate[None], w_up[None], w_down[None],
                        jnp.zeros((n_tiles,), jnp.int32), jnp.full((1,), n_tiles, jnp.int32),
                        tm, tf, add_residual=True)
    return _ple(x, p, ple_wp, ple_wg, ple_g[None, :], ple_gate_g[None, :], tiles["ple"])


def _layer_odd(x, p, tiles, mix_g, ffn_g, w_pw1, b_pw1, w_dw, b_dw, ln_g, ln_b, w_pw2, b_pw2,
               w_router, b_router, w_gate, w_up, w_down, ple_wp, ple_g, ple_gate_g, ple_wg):
    s, d = x.shape
    x, info, counts = _conv_module_router(
        x, mix_g[None, :], w_pw1.astype(BF16), b_pw1[None, :], w_dw, b_dw[None, :], ln_g[None, :],
        ln_b[None, :], w_pw2.astype(BF16), b_pw2[None, :], ffn_g[None, :], w_router.T,
        b_router[:, None], tiles["conv"])
    tm, tf = tiles["expert"]
    e12 = info[0:2].astype(jnp.int32)
    w12 = info[2:4].T
    r12 = info[4:6].astype(jnp.int32)
    cnt = counts[:, 0].astype(jnp.int32)
    group_tiles = (cnt + tm - 1) // tm
    tile_end = jnp.cumsum(group_tiles)
    offsets = (tile_end - group_tiles) * tm
    expert_ids = jnp.arange(N_EXPERTS, dtype=jnp.int32)
    onehot = e12[:, :, None] == expert_ids[None, None, :]
    pos = jnp.sum(jnp.where(onehot, offsets[None, None, :], 0), axis=-1) + r12
    max_tiles = (2 * s) // tm + N_EXPERTS
    tile_ids = jnp.arange(max_tiles, dtype=jnp.int32)
    tile_expert = jnp.minimum(jnp.sum((tile_end[None, :] <= tile_ids[:, None]).astype(jnp.int32), axis=1),
                              N_EXPERTS - 1)
    n_tiles = tile_end[-1:].astype(jnp.int32)
    xs = _dispatch(x, pos, max_tiles * tm, tiles["move"])
    ys = _grouped_swiglu(xs, ffn_g[None, :], w_gate, w_up, w_down, tile_expert, n_tiles,
                         tm, tf, add_residual=False)
    return _ple(x, p, ple_wp, ple_wg, ple_g[None, :], ple_gate_g[None, :], tiles["move"],
                routed=(pos, w12, ys))


def kernel(x, p, mix_norm_g, ffn_norm_g, attn_w_qkv, attn_w_o, attn_g_q, attn_g_k, attn_lam_q1, attn_lam_k1, attn_lam_q2, attn_lam_k2, attn_g_sub, conv_w_pw1, conv_b_pw1, conv_w_dw, conv_b_dw, conv_ln_g, conv_ln_b, conv_w_pw2, conv_b_pw2, ffn_w_gate, ffn_w_up, ffn_w_down, moe_w_router, moe_b_router, moe_w_gate, moe_w_up, moe_w_down, ple_w_proj, ple_norm_g, ple_gate_norm_g, ple_w_gate):
    depth, batch, s, _ = p.shape
    tiles = _tiles(s)
    outs = []
    for b in range(batch):
        xb = x[b]
        for i in range(depth):
            j = i // 2
            ple_args = (ple_w_proj[i], ple_norm_g[i], ple_gate_norm_g[i], ple_w_gate[i])
            if i % 2 == 0:
                lambda_init = 0.8 - 0.6 * math.exp(-0.3 * i)
                lam_vecs = jnp.stack([attn_lam_q1[j], attn_lam_k1[j], attn_lam_q2[j], attn_lam_k2[j]])
                xb = _layer_even(xb, (p, i, b), tiles, mix_norm_g[i], ffn_norm_g[i], attn_w_qkv[j],
                                 attn_w_o[j], attn_g_q[j], attn_g_k[j], lam_vecs, attn_g_sub[j],
                                 lambda_init, ffn_w_gate[j], ffn_w_up[j], ffn_w_down[j], *ple_args)
            else:
                xb = _layer_odd(xb, (p, i, b), tiles, mix_norm_g[i], ffn_norm_g[i], conv_w_pw1[j],
                                conv_b_pw1[j], conv_w_dw[j], conv_b_dw[j], conv_ln_g[j], conv_ln_b[j],
                                conv_w_pw2[j], conv_b_pw2[j], moe_w_router[j], moe_b_router[j],
                                moe_w_gate[j], moe_w_up[j], moe_w_down[j], *ple_args)
        outs.append(xb)
    return jnp.stack(outs)
```

```python
import functools
import math

import jax
import jax.numpy as jnp
from jax import lax
from jax.experimental import pallas as pl
from jax.experimental.pallas import tpu as pltpu

N_HEADS = 8
HEAD_DIM = 64
N_EXPERTS = 8
CONV_WIDTH = 31
NORM_EPS = 1e-6
LN_EPS = 1e-5
ATTN_SCALE = HEAD_DIM ** -0.5
LOG2_E = math.log2(math.e)

BF16 = jnp.bfloat16
F32 = jnp.float32
NEG_BIG = -1e30
VMEM_LIMIT_BYTES = 56 * 1024 * 1024
SUBLANES, LANES = 8, 128
MXU_DIM = 256
ATTN_UNROLL = 8
BOUND_SLACK = 1.01
MAX_FIXED_SHIFT = 40.0
HALO_ROWS = 32


def _params(*sem):
    return pltpu.CompilerParams(dimension_semantics=sem, vmem_limit_bytes=VMEM_LIMIT_BYTES)


def _rms(xf, g):
    return xf * lax.rsqrt(jnp.mean(xf * xf, axis=-1, keepdims=True) + NORM_EPS) * g


def _dot(a, b):
    return jnp.dot(a, b, preferred_element_type=F32)


def _dot_nt(a, b):
    return lax.dot_general(a, b, (((1,), (1,)), ((), ())), preferred_element_type=F32)


def _qkv_kernel(x_ref, g_ref, w_ref, gain_ref, bd_ref, qk_ref, vt_ref, h_ref, *, n_qk_tiles):
    j = pl.program_id(1)

    @pl.when(j == 0)
    def _():
        h_ref[...] = _rms(x_ref[...], g_ref[...]).astype(BF16)

    y = _dot(h_ref[...], w_ref[...].astype(BF16))

    @pl.when(j < n_qk_tiles)
    def _():
        y2 = (y * y).astype(BF16)
        width = bd_ref.shape[0]
        ss = jnp.concatenate([_dot(y2[:, c:c + width], bd_ref[...]) for c in range(0, y2.shape[1], width)],
                             axis=1)
        qk_ref[...] = (y * lax.rsqrt(ss * (1.0 / HEAD_DIM) + NORM_EPS) * gain_ref[...]).astype(BF16)

    @pl.when(j >= n_qk_tiles)
    def _():
        vt_ref[...] = y.T.astype(BF16)


def _qkv_proj(x, g, w, gain_row, tm, tn):
    s, d = x.shape
    n = w.shape[1]
    n_qk = 2 * n // 3
    n_qk_tiles = n_qk // tn
    lane = jnp.arange(MXU_DIM) // HEAD_DIM
    bd = (lane[:, None] == lane[None, :]).astype(BF16)
    return pl.pallas_call(
        functools.partial(_qkv_kernel, n_qk_tiles=n_qk_tiles),
        grid=(s // tm, n // tn),
        in_specs=[
            pl.BlockSpec((tm, d), lambda i, j: (i, 0)),
            pl.BlockSpec((1, d), lambda i, j: (0, 0)),
            pl.BlockSpec((d, tn), lambda i, j: (0, j)),
            pl.BlockSpec((1, tn), lambda i, j: (0, jnp.minimum(j, n_qk_tiles - 1))),
            pl.BlockSpec((MXU_DIM, MXU_DIM), lambda i, j: (0, 0)),
        ],
        out_specs=[
            pl.BlockSpec((tm, tn), lambda i, j: (i, jnp.minimum(j, n_qk_tiles - 1))),
            pl.BlockSpec((tn, tm), lambda i, j: (jnp.maximum(j - n_qk_tiles, 0), i)),
        ],
        out_shape=[jax.ShapeDtypeStruct((s, n_qk), BF16),
                   jax.ShapeDtypeStruct((n - n_qk, s), BF16)],
        scratch_shapes=[pltpu.VMEM((tm, d), BF16)],
        compiler_params=_params("parallel", "arbitrary"),
        name="qkv_proj",
    )(x, g, w, gain_row, bd)


def _attn_kernel(q_ref, k_ref, vt_ref, shift_ref, lam_ref, gsub_ref, o_ref, qq_ref, acc_ref, m_ref,
                 l_ref, mask_ref, *, bq, lambda_init, unroll, online):
    i = pl.program_id(1)

    @pl.when((pl.program_id(0) == 0) & (i == 0))
    def _():
        key = lax.broadcasted_iota(jnp.int32, mask_ref.shape, 0)
        qry = lax.broadcasted_iota(jnp.int32, mask_ref.shape, 1)
        mask_ref[...] = jnp.where(key <= jnp.where(qry >= bq, qry - bq, qry), 0.0, NEG_BIG)

    q = q_ref[...]
    lane = lax.broadcasted_iota(jnp.int32, q.shape, 1)
    qq_ref[0:bq, :] = jnp.where(lane < HEAD_DIM, q, 0)
    qq_ref[bq:, :] = jnp.where(lane >= HEAD_DIM, q, 0)
    acc_ref[...] = jnp.zeros(acc_ref.shape, F32)
    l_ref[...] = jnp.zeros(l_ref.shape, F32)
    if online:
        m_ref[...] = jnp.full(m_ref.shape, NEG_BIG, F32)
    else:
        m_ref[...] = shift_ref[0]

    def block(j, masked):
        start = pl.multiple_of(j * bq, bq)
        kb = k_ref[pl.ds(start, bq), :]
        vb = vt_ref[:, pl.ds(start, bq)]
        s = _dot_nt(kb, qq_ref[...])
        if masked:
            s = s + mask_ref[...]
        if online:
            m_old = m_ref[...]
            m_new = jnp.maximum(m_old, jnp.max(s, axis=0, keepdims=True))
            alpha = jnp.exp2(m_old - m_new)
            p = jnp.exp2(s - m_new)
            l_ref[...] = alpha * l_ref[...] + jnp.sum(p, axis=0, keepdims=True)
            m_ref[...] = m_new
            acc_ref[...] = alpha * acc_ref[...] + _dot(vb, p.astype(BF16))
        else:
            p = jnp.exp2(s - m_ref[...])
            l_ref[...] += jnp.sum(p, axis=0, keepdims=True)
            acc_ref[...] += _dot(vb, p.astype(BF16))

    def group(g, _):
        for t in range(unroll):
            block(unroll * g + t, masked=False)
        return 0

    lax.fori_loop(0, i // unroll, group, 0)
    size = unroll // 2
    while size >= 1:
        @pl.when((i % unroll) & size != 0)
        def _(size=size):
            first = i - (i % (2 * size))
            for t in range(size):
                block(first + t, masked=False)
        size //= 2
    block(i, masked=True)

    l = l_ref[...]
    lam_vecs = lam_ref[...]
    lam = (jnp.exp(jnp.sum(lam_vecs[0:1] * lam_vecs[1:2], axis=-1, keepdims=True))
           - jnp.exp(jnp.sum(lam_vecs[2:3] * lam_vecs[3:4], axis=-1, keepdims=True))
           + lambda_init)
    o = acc_ref[:, 0:bq] / l[:, 0:bq] - lam * (acc_ref[:, bq:] / l[:, bq:])
    o = o * lax.rsqrt(jnp.mean(o * o, axis=0, keepdims=True) + NORM_EPS)
    o = o * (gsub_ref[...] * (1.0 - lambda_init))
    o_ref[...] = o.T.astype(BF16)


def _diff_attention(qk, vt, q_gain, k_gain, lam_vecs, gsub_col, lambda_init, bq):
    unit = math.sqrt(HEAD_DIM)
    q_norm = unit * jnp.max(jnp.abs(q_gain), axis=-1)
    k_norm = unit * jnp.max(jnp.abs(k_gain), axis=-1)
    shift = q_norm * k_norm * BOUND_SLACK
    fits = jnp.all(shift <= MAX_FIXED_SHIFT)
    shift_row = jnp.repeat(shift, bq)[None, None, :]

    def call(online):
        return _diff_attention_call(qk, vt, shift_row, lam_vecs, gsub_col, lambda_init, bq, online)

    return lax.cond(fits, lambda: call(False), lambda: call(True))


def _diff_attention_call(qk, vt, shift_row, lam_vecs, gsub_col, lambda_init, bq, online):
    s = qk.shape[0]
    hd2 = 2 * HEAD_DIM
    return pl.pallas_call(
        functools.partial(_attn_kernel, bq=bq, lambda_init=lambda_init, unroll=ATTN_UNROLL,
                          online=online),
        grid=(N_HEADS, s // bq),
        in_specs=[
            pl.BlockSpec((bq, hd2), lambda h, i: (i, h)),
            pl.BlockSpec((s, hd2), lambda h, i: (0, N_HEADS + h)),
            pl.BlockSpec((hd2, s), lambda h, i: (h, 0)),
            pl.BlockSpec((1, 1, 2 * bq), lambda h, i: (0, 0, 0)),
            pl.BlockSpec((4, HEAD_DIM), lambda h, i: (0, 0)),
            pl.BlockSpec((hd2, 1), lambda h, i: (0, 0)),
        ],
        out_specs=pl.BlockSpec((bq, hd2), lambda h, i: (i, h)),
        out_shape=jax.ShapeDtypeStruct((s, N_HEADS * hd2), BF16),
        scratch_shapes=[pltpu.VMEM((2 * bq, hd2), BF16), pltpu.VMEM((hd2, 2 * bq), F32),
                        pltpu.VMEM((1, 2 * bq), F32), pltpu.VMEM((1, 2 * bq), F32),
                        pltpu.VMEM((bq, 2 * bq), F32)],
        compiler_params=_params("arbitrary", "arbitrary"),
        name="diff_attn_online" if online else "diff_attn",
    )(qk, qk, vt, shift_row, lam_vecs, gsub_col)


def _proj_res_kernel(x_ref, a_ref, w_ref, o_ref, wb_ref):
    @pl.when(pl.program_id(0) == 0)
    def _():
        wb_ref[...] = w_ref[...].astype(BF16)

    o_ref[...] = x_ref[...] + _dot(a_ref[...], wb_ref[...])


def _proj_residual(x, a, w, tm):
    s, d = x.shape
    k = a.shape[1]
    return pl.pallas_call(
        _proj_res_kernel,
        grid=(s // tm,),
        in_specs=[
            pl.BlockSpec((tm, d), lambda i: (i, 0)),
            pl.BlockSpec((tm, k), lambda i: (i, 0)),
            pl.BlockSpec((k, d), lambda i: (0, 0)),
        ],
        out_specs=pl.BlockSpec((tm, d), lambda i: (i, 0)),
        out_shape=jax.ShapeDtypeStruct((s, d), F32),
        scratch_shapes=[pltpu.VMEM((k, d), BF16)],
        compiler_params=_params("arbitrary"),
        name="attn_out_proj",
    )(x, a, w)


def _swiglu_kernel(te_ref, nt_ref, x_ref, g_ref, wg_ref, wu_ref, wd_ref, o_ref, h_ref, acc_ref,
                   *, add_residual):
    t = pl.program_id(0)
    f = pl.program_id(1)

    @pl.when(t < nt_ref[0])
    def _():
        @pl.when(f == 0)
        def _():
            h_ref[...] = _rms(x_ref[...], g_ref[...]).astype(BF16)
            acc_ref[...] = jnp.zeros(acc_ref.shape, F32)

        h = h_ref[...]
        gate = _dot(h, wg_ref[0].astype(BF16))
        up = _dot(h, wu_ref[0].astype(BF16))
        act = (gate * jax.nn.sigmoid(gate) * up).astype(BF16)
        acc_ref[...] += _dot(act, wd_ref[0].astype(BF16))

        @pl.when(f == pl.num_programs(1) - 1)
        def _():
            if add_residual:
                o_ref[...] = x_ref[...] + acc_ref[...]
            else:
                o_ref[...] = acc_ref[...]

    @pl.when((t >= nt_ref[0]) & (f == 0))
    def _():
        o_ref[...] = jnp.zeros(o_ref.shape, F32)


def _grouped_swiglu(x, g, w_gate, w_up, w_down, tile_expert, n_tiles, tm, tf, add_residual):
    m, d = x.shape
    dff = w_gate.shape[2]
    nf = dff // tf
    max_tiles = tile_expert.shape[0]

    def _last(nt):
        return jnp.maximum(nt[0] - 1, 0)

    def row_map(t, f, te, nt):
        return (jnp.minimum(t, _last(nt)), 0)

    def _tf(t, f, nt):
        return jnp.where(t < nt[0], f, nf - 1)

    def _te(t, te, nt):
        return te[jnp.minimum(t, _last(nt))]

    return pl.pallas_call(
        functools.partial(_swiglu_kernel, add_residual=add_residual),
        grid_spec=pltpu.PrefetchScalarGridSpec(
            num_scalar_prefetch=2,
            grid=(max_tiles, nf),
            in_specs=[
                pl.BlockSpec((tm, d), row_map),
                pl.BlockSpec((1, d), lambda t, f, te, nt: (0, 0)),
                pl.BlockSpec((1, d, tf), lambda t, f, te, nt: (_te(t, te, nt), 0, _tf(t, f, nt))),
                pl.BlockSpec((1, d, tf), lambda t, f, te, nt: (_te(t, te, nt), 0, _tf(t, f, nt))),
                pl.BlockSpec((1, tf, d), lambda t, f, te, nt: (_te(t, te, nt), _tf(t, f, nt), 0)),
            ],
            out_specs=pl.BlockSpec((tm, d), lambda t, f, te, nt: (t, 0)),
            scratch_shapes=[pltpu.VMEM((tm, d), BF16), pltpu.VMEM((tm, d), F32)],
        ),
        out_shape=jax.ShapeDtypeStruct((m, d), F32),
        compiler_params=_params("arbitrary", "arbitrary"),
        name="swiglu_residual" if add_residual else "expert_swiglu",
    )(tile_expert, n_tiles, x, g, w_gate, w_up, w_down)


def _ple_tail(x, p_ref, wp_ref, wg_ref, g_e_ref, g_gate_ref, o_ref):
    e = _rms(_dot(p_ref[...].astype(BF16), wp_ref[...]), g_e_ref[...])
    gate = jax.nn.sigmoid(_dot(_rms(x, g_gate_ref[...]).astype(BF16), wg_ref[...]))
    o_ref[...] = x + gate * e


def _ple_kernel(x_ref, p_ref, wp_ref, wg_ref, g_e_ref, g_gate_ref, o_ref, wpb_ref, wgb_ref):
    @pl.when(pl.program_id(0) == 0)
    def _():
        wpb_ref[...] = wp_ref[...].astype(BF16)
        wgb_ref[...] = wg_ref[...].astype(BF16)

    _ple_tail(x_ref[...], p_ref, wpb_ref, wgb_ref, g_e_ref, g_gate_ref, o_ref)


def _combine_ple_kernel(pos_ref, pos_next_ref, x_ref, w_ref, ys_ref, p_ref, wp_ref, wg_ref, g_e_ref,
                        g_gate_ref, o_ref, wpb_ref, wgb_ref, ybuf_ref, sem, *, tc):
    i = pl.program_id(0)
    slot = i % 2

    def issue_tile(rows_ref, dst_slot):
        def issue(t, _):
            for k in range(2):
                pltpu.make_async_copy(ys_ref.at[pl.ds(rows_ref[k, t], 1), :],
                                      ybuf_ref.at[dst_slot, k, pl.ds(t, 1), :],
                                      sem.at[dst_slot]).start(priority=k)
            return 0
        lax.fori_loop(0, tc, issue, 0, unroll=8)

    @pl.when(i == 0)
    def _():
        wpb_ref[...] = wp_ref[...].astype(BF16)
        wgb_ref[...] = wg_ref[...].astype(BF16)
        issue_tile(pos_ref, 0)

    @pl.when(i + 1 < pl.num_programs(0))
    def _():
        issue_tile(pos_next_ref, 1 - slot)

    for k in range(2):
        pltpu.make_async_copy(ys_ref.at[pl.ds(0, tc), :], ybuf_ref.at[slot, k], sem.at[slot]).wait()
    w = w_ref[...]
    x = x_ref[...] + w[:, 0:1] * ybuf_ref[slot, 0] + w[:, 1:2] * ybuf_ref[slot, 1]
    _ple_tail(x, p_ref, wpb_ref, wgb_ref, g_e_ref, g_gate_ref, o_ref)


def _ple(x, p_layer, wp, wg, g_e, g_gate, tm, routed=None):
    s, d = x.shape
    p, layer, batch = p_layer
    pd = p.shape[-1]
    row = lambda i: (i, 0)
    fixed = lambda i: (0, 0)
    tail_specs = [
        pl.BlockSpec((None, None, tm, pd), lambda i: (layer, batch, i, 0)),
        pl.BlockSpec((pd, d), fixed),
        pl.BlockSpec((d, d), fixed),
        pl.BlockSpec((1, d), fixed),
        pl.BlockSpec((1, d), fixed),
    ]
    scratch = [pltpu.VMEM((pd, d), BF16), pltpu.VMEM((d, d), BF16)]
    if routed is None:
        return pl.pallas_call(
            _ple_kernel,
            grid=(s // tm,),
            in_specs=[pl.BlockSpec((tm, d), row)] + tail_specs,
            out_specs=pl.BlockSpec((tm, d), row),
            out_shape=jax.ShapeDtypeStruct((s, d), F32),
            scratch_shapes=scratch,
            compiler_params=_params("arbitrary"),
            name="ple",
        )(x, p, wp, wg, g_e, g_gate)
    pos, w12, ys = routed
    last = s // tm - 1
    return pl.pallas_call(
        functools.partial(_combine_ple_kernel, tc=tm),
        grid=(s // tm,),
        in_specs=[
            pl.BlockSpec((2, tm), lambda i: (0, i), memory_space=pltpu.SMEM),
            pl.BlockSpec((2, tm), lambda i: (0, jnp.minimum(i + 1, last)), memory_space=pltpu.SMEM),
            pl.BlockSpec((tm, d), row),
            pl.BlockSpec((tm, 2), row),
            pl.BlockSpec(memory_space=pl.ANY),
        ] + tail_specs,
        out_specs=pl.BlockSpec((tm, d), row),
        out_shape=jax.ShapeDtypeStruct((s, d), F32),
        scratch_shapes=scratch + [pltpu.VMEM((2, 2, tm, d), F32), pltpu.SemaphoreType.DMA((2,))],
        compiler_params=_params("arbitrary"),
        name="combine_ple",
    )(pos, pos, x, w12, ys, p, wp, wg, g_e, g_gate)


def _route(x, g_ref, wrt_ref, b_ref, tri_ref, carry_ref, info_ref, cnt_ref):
    h = _rms(x, g_ref[...])
    logits = lax.dot_general(wrt_ref[...], h, (((1,), (1,)), ((), ())),
                             preferred_element_type=F32,
                             precision=lax.Precision.HIGHEST) + b_ref[...]
    eidx = lax.broadcasted_iota(jnp.int32, logits.shape, 0)
    m1 = jnp.max(logits, axis=0, keepdims=True)
    e1 = jnp.min(jnp.where(logits == m1, eidx, N_EXPERTS), axis=0, keepdims=True)
    rest = jnp.where(eidx == e1, -jnp.inf, logits)
    m2 = jnp.max(rest, axis=0, keepdims=True)
    e2 = jnp.min(jnp.where(rest == m2, eidx, N_EXPERTS), axis=0, keepdims=True)
    z = jnp.exp(m2 - m1)
    w1 = 1.0 / (1.0 + z)
    w2 = z / (1.0 + z)
    sel1 = eidx == e1
    sel2 = eidx == e2
    sel = (sel1 | sel2).astype(F32)
    rank = carry_ref[:, 0:1] + _dot(sel.astype(BF16), tri_ref[...])
    r1 = jnp.sum(jnp.where(sel1, rank, 0.0), axis=0, keepdims=True)
    r2 = jnp.sum(jnp.where(sel2, rank, 0.0), axis=0, keepdims=True)
    rows = (e1.astype(F32), e2.astype(F32), w1, w2, r1, r2)
    info = jnp.zeros(logits.shape, F32)
    for k, v in enumerate(rows):
        info = jnp.where(eidx == k, v, info)
    info_ref[...] = info
    carry_ref[...] = carry_ref[...] + jnp.sum(sel, axis=1, keepdims=True)
    cnt_ref[...] = carry_ref[...]


def _conv_router_kernel(x_ref, g_ref, w1_ref, b1_ref, wdw_ref, bdw_ref, lng_ref, lnb_ref, w2_ref, b2_ref,
                        rg_ref, wrt_ref, rb_ref, o_ref, info_ref, cnt_ref,
                        ext_ref, c_ref, tri_ref, carry_ref, *, tm, rc):
    i = pl.program_id(0)
    d = x_ref.shape[1]

    @pl.when(i == 0)
    def _():
        ext_ref[0:HALO_ROWS, :] = jnp.zeros((HALO_ROWS, d), F32)
        a = lax.broadcasted_iota(jnp.int32, (tm, tm), 0)
        b = lax.broadcasted_iota(jnp.int32, (tm, tm), 1)
        tri_ref[...] = (a < b).astype(BF16)
        carry_ref[...] = jnp.zeros(carry_ref.shape, F32)

    @pl.when(i > 0)
    def _():
        ext_ref[0:HALO_ROWS, :] = ext_ref[tm:tm + HALO_ROWS, :]

    x = x_ref[...]
    y = _dot(_rms(x, g_ref[...]).astype(BF16), w1_ref[...]) + b1_ref[...]
    ext_ref[HALO_ROWS:, :] = y[:, :d] * jax.nn.sigmoid(y[:, d:])
    first = HALO_ROWS - (CONV_WIDTH - 1)
    win_rows = rc + HALO_ROWS

    def chunk(r, _):
        base = pl.multiple_of(r * rc, rc)
        for lo in range(0, d, LANES):
            cols = slice(lo, lo + LANES)
            win = ext_ref[pl.ds(base, win_rows), cols]
            acc = jnp.zeros((rc, LANES), F32) + bdw_ref[:, cols]
            for b in range(SUBLANES):
                wb = win if b == 0 else pltpu.roll(win, shift=win_rows - b, axis=0)
                for a in range(win_rows // SUBLANES):
                    t = SUBLANES * a + b - first
                    if 0 <= t < CONV_WIDTH:
                        acc = acc + wdw_ref[t:t + 1, cols] * wb[SUBLANES * a:SUBLANES * a + rc, :]
            c_ref[pl.ds(base, rc), cols] = acc
        return 0

    lax.fori_loop(0, tm // rc, chunk, 0)
    c = c_ref[...]
    mu = jnp.mean(c, axis=-1, keepdims=True)
    cc = c - mu
    y = cc * lax.rsqrt(jnp.mean(cc * cc, axis=-1, keepdims=True) + LN_EPS) * lng_ref[...] + lnb_ref[...]
    y = y * jax.nn.sigmoid(y)
    x1 = x + _dot(y.astype(BF16), w2_ref[...]) + b2_ref[...]
    o_ref[...] = x1
    _route(x1, rg_ref, wrt_ref, rb_ref, tri_ref, carry_ref, info_ref, cnt_ref)


def _conv_module_router(x, g, w1, b1, w_dw, b_dw, ln_g, ln_b, w2, b2, rg, w_router_t, rb_col, tm):
    s, d = x.shape
    row = lambda i: (i, 0)
    fixed = lambda i: (0, 0)
    return pl.pallas_call(
        functools.partial(_conv_router_kernel, tm=tm, rc=min(64, tm)),
        grid=(s // tm,),
        in_specs=[
            pl.BlockSpec((tm, d), row),
            pl.BlockSpec((1, d), fixed),
            pl.BlockSpec((d, 2 * d), fixed),
            pl.BlockSpec((1, 2 * d), fixed),
            pl.BlockSpec((CONV_WIDTH, d), fixed),
            pl.BlockSpec((1, d), fixed),
            pl.BlockSpec((1, d), fixed),
            pl.BlockSpec((1, d), fixed),
            pl.BlockSpec((d, d), fixed),
            pl.BlockSpec((1, d), fixed),
            pl.BlockSpec((1, d), fixed),
            pl.BlockSpec((N_EXPERTS, d), fixed),
            pl.BlockSpec((N_EXPERTS, 1), fixed),
        ],
        out_specs=[
            pl.BlockSpec((tm, d), row),
            pl.BlockSpec((N_EXPERTS, tm), lambda i: (0, i)),
            pl.BlockSpec((N_EXPERTS, LANES), fixed),
        ],
        out_shape=[jax.ShapeDtypeStruct((s, d), F32),
                   jax.ShapeDtypeStruct((N_EXPERTS, s), F32),
                   jax.ShapeDtypeStruct((N_EXPERTS, LANES), F32)],
        scratch_shapes=[pltpu.VMEM((tm + HALO_ROWS, d), F32), pltpu.VMEM((tm, d), F32),
                        pltpu.VMEM((tm, tm), BF16), pltpu.VMEM((N_EXPERTS, LANES), F32)],
        compiler_params=_params("arbitrary"),
        name="conv_module_router",
    )(x, g, w1, b1, w_dw, b_dw, ln_g, ln_b, w2, b2, rg, w_router_t, rb_col)


def _dispatch_kernel(pos_ref, x_ref, xs_in_ref, xs_ref, sem, *, td):
    del xs_in_ref

    def row_copy(k, t):
        return pltpu.make_async_copy(x_ref.at[pl.ds(t, 1), :],
                                     xs_ref.at[pl.ds(pos_ref[k, t], 1), :], sem)

    def issue(t, _):
        row_copy(0, t).start(priority=0)
        row_copy(1, t).start(priority=1)
        return 0

    lax.fori_loop(0, td, issue, 0, unroll=8)
    for _ in range(2):
        pltpu.make_async_copy(x_ref, xs_ref.at[pl.ds(0, td), :], sem).wait()


def _dispatch(x, pos, m_pad, td):
    s, d = x.shape
    return pl.pallas_call(
        functools.partial(_dispatch_kernel, td=td),
        grid=(s // td,),
        in_specs=[
            pl.BlockSpec((2, td), lambda i: (0, i), memory_space=pltpu.SMEM),
            pl.BlockSpec((td, d), lambda i: (i, 0)),
            pl.BlockSpec(memory_space=pl.ANY),
        ],
        out_specs=pl.BlockSpec(memory_space=pl.ANY),
        out_shape=jax.ShapeDtypeStruct((m_pad, d), F32),
        scratch_shapes=[pltpu.SemaphoreType.DMA(())],
        input_output_aliases={2: 0},
        compiler_params=_params("arbitrary"),
        name="moe_dispatch",
    )(pos, x, jnp.zeros((m_pad, d), F32))


def _tiles(s):
    t = lambda want: min(want, s)
    return dict(qkv=(t(1024), 1024), attn=t(512), proj=t(1024), ffn=(t(1024), 512), ple=t(1024),
                conv=t(1024), move=t(512), expert=(t(1024), 512))


def _layer_even(x, p, tiles, mix_g, ffn_g, w_qkv, w_o, g_q, g_k, lam_vecs, g_sub, lambda_init,
                w_gate, w_up, w_down, ple_wp, ple_g, ple_gate_g, ple_wg):
    s, d = x.shape
    q_gain = g_q * (ATTN_SCALE * LOG2_E)
    gain_row = jnp.concatenate([jnp.tile(q_gain.reshape(-1), N_HEADS),
                                jnp.tile(g_k.reshape(-1), N_HEADS)])[None, :]
    qk, vt = _qkv_proj(x, mix_g[None, :], w_qkv, gain_row, *tiles["qkv"])
    o = _diff_attention(qk, vt, q_gain, g_k, lam_vecs, g_sub[:, None], lambda_init, tiles["attn"])
    x = _proj_residual(x, o, w_o, tiles["proj"])
    tm, tf = tiles["ffn"]
    n_tiles = s // tm
    x = _grouped_swiglu(x, ffn_g[None, :], w_gate[None], w_up[None], w_down[None],
                        jnp.zeros((n_tiles,), jnp.int32), jnp.full((1,), n_tiles, jnp.int32),
                        tm, tf, add_residual=True)
    return _ple(x, p, ple_wp, ple_wg, ple_g[None, :], ple_gate_g[None, :], tiles["ple"])


def _layer_odd(x, p, tiles, mix_g, ffn_g, w_pw1, b_pw1, w_dw, b_dw, ln_g, ln_b, w_pw2, b_pw2,
               w_router, b_router, w_gate, w_up, w_down, ple_wp, ple_g, ple_gate_g, ple_wg):
    s, d = x.shape
    x, info, counts = _conv_module_router(
        x, mix_g[None, :], w_pw1.astype(BF16), b_pw1[None, :], w_dw, b_dw[None, :], ln_g[None, :],
        ln_b[None, :], w_pw2.astype(BF16), b_pw2[None, :], ffn_g[None, :], w_router.T,
        b_router[:, None], tiles["conv"])
    tm, tf = tiles["expert"]
    e12 = info[0:2].astype(jnp.int32)
    w12 = info[2:4].T
    r12 = info[4:6].astype(jnp.int32)
    cnt = counts[:, 0].astype(jnp.int32)
    group_tiles = (cnt + tm - 1) // tm
    tile_end = jnp.cumsum(group_tiles)
    offsets = (tile_end - group_tiles) * tm
    expert_ids = jnp.arange(N_EXPERTS, dtype=jnp.int32)
    onehot = e12[:, :, None] == expert_ids[None, None, :]
    pos = jnp.sum(jnp.where(onehot, offsets[None, None, :], 0), axis=-1) + r12
    max_tiles = (2 * s) // tm + N_EXPERTS
    tile_ids = jnp.arange(max_tiles, dtype=jnp.int32)
    tile_expert = jnp.minimum(jnp.sum((tile_end[None, :] <= tile_ids[:, None]).astype(jnp.int32), axis=1),
                              N_EXPERTS - 1)
    n_tiles = tile_end[-1:].astype(jnp.int32)
    xs = _dispatch(x, pos, max_tiles * tm, tiles["move"])
    ys = _grouped_swiglu(xs, ffn_g[None, :], w_gate, w_up, w_down, tile_expert, n_tiles,
                         tm, tf, add_residual=False)
    return _ple(x, p, ple_wp, ple_wg, ple_g[None, :], ple_gate_g[None, :], tiles["move"],
                routed=(pos, w12, ys))


def kernel(x, p, mix_norm_g, ffn_norm_g, attn_w_qkv, attn_w_o, attn_g_q, attn_g_k, attn_lam_q1, attn_lam_k1, attn_lam_q2, attn_lam_k2, attn_g_sub, conv_w_pw1, conv_b_pw1, conv_w_dw, conv_b_dw, conv_ln_g, conv_ln_b, conv_w_pw2, conv_b_pw2, ffn_w_gate, ffn_w_up, ffn_w_down, moe_w_router, moe_b_router, moe_w_gate, moe_w_up, moe_w_down, ple_w_proj, ple_norm_g, ple_gate_norm_g, ple_w_gate):
    depth, batch, s, _ = p.shape
    tiles = _tiles(s)
    outs = []
    for b in range(batch):
        xb = x[b]
        for i in range(depth):
            j = i // 2
            ple_args = (ple_w_proj[i], ple_norm_g[i], ple_gate_norm_g[i], ple_w_gate[i])
            if i % 2 == 0:
                lambda_init = 0.8 - 0.6 * math.exp(-0.3 * i)
                lam_vecs = jnp.stack([attn_lam_q1[j], attn_lam_k1[j], attn_lam_q2[j], attn_lam_k2[j]])
                xb = _layer_even(xb, (p, i, b), tiles, mix_norm_g[i], ffn_norm_g[i], attn_w_qkv[j],
                                 attn_w_o[j], attn_g_q[j], attn_g_k[j], lam_vecs, attn_g_sub[j],
                                 lambda_init, ffn_w_gate[j], ffn_w_up[j], ffn_w_down[j], *ple_args)
            else:
                xb = _layer_odd(xb, (p, i, b), tiles, mix_norm_g[i], ffn_norm_g[i], conv_w_pw1[j],
                                conv_b_pw1[j], conv_w_dw[j], conv_b_dw[j], conv_ln_g[j], conv_ln_b[j],
                                conv_w_pw2[j], conv_b_pw2[j], moe_w_router[j], moe_b_router[j],
                                moe_w_gate[j], moe_w_up[j], moe_w_down[j], *ple_args)
        outs.append(xb)
    return jnp.stack(outs)
```

```python
import functools
import math

import jax
import jax.numpy as jnp
from jax import lax
from jax.experimental import pallas as pl
from jax.experimental.pallas import tpu as pltpu

N_HEADS = 8
HEAD_DIM = 64
N_EXPERTS = 8
CONV_WIDTH = 31
NORM_EPS = 1e-6
LN_EPS = 1e-5
ATTN_SCALE = HEAD_DIM ** -0.5
LOG2_E = math.log2(math.e)

BF16 = jnp.bfloat16
F32 = jnp.float32
NEG_BIG = -1e30
VMEM_LIMIT_BYTES = 56 * 1024 * 1024
SUBLANES, LANES = 8, 128
MXU_DIM = 256
ATTN_UNROLL = 8
BOUND_SLACK = 1.01
MAX_FIXED_SHIFT = 40.0
HALO_ROWS = 32


def _params(*sem):
    return pltpu.CompilerParams(dimension_semantics=sem, vmem_limit_bytes=VMEM_LIMIT_BYTES)


def _rms(xf, g):
    return xf * lax.rsqrt(jnp.mean(xf * xf, axis=-1, keepdims=True) + NORM_EPS) * g


def _dot(a, b):
    return jnp.dot(a, b, preferred_element_type=F32)


def _dot_nt(a, b):
    return lax.dot_general(a, b, (((1,), (1,)), ((), ())), preferred_element_type=F32)


def _qkv_kernel(x_ref, g_ref, w_ref, gain_ref, bd_ref, qk_ref, vt_ref, h_ref, *, n_qk_tiles):
    j = pl.program_id(1)

    @pl.when(j == 0)
    def _():
        h_ref[...] = _rms(x_ref[...], g_ref[...]).astype(BF16)

    y = _dot(h_ref[...], w_ref[...].astype(BF16))

    @pl.when(j < n_qk_tiles)
    def _():
        y2 = (y * y).astype(BF16)
        width = bd_ref.shape[0]
        ss = jnp.concatenate([_dot(y2[:, c:c + width], bd_ref[...]) for c in range(0, y2.shape[1], width)],
                             axis=1)
        qk_ref[...] = (y * lax.rsqrt(ss * (1.0 / HEAD_DIM) + NORM_EPS) * gain_ref[...]).astype(BF16)

    @pl.when(j >= n_qk_tiles)
    def _():
        vt_ref[...] = y.T.astype(BF16)


def _qkv_proj(x, g, w, gain_row, tm, tn):
    s, d = x.shape
    n = w.shape[1]
    n_qk = 2 * n // 3
    n_qk_tiles = n_qk // tn
    lane = jnp.arange(MXU_DIM) // HEAD_DIM
    bd = (lane[:, None] == lane[None, :]).astype(BF16)
    return pl.pallas_call(
        functools.partial(_qkv_kernel, n_qk_tiles=n_qk_tiles),
        grid=(s // tm, n // tn),
        in_specs=[
            pl.BlockSpec((tm, d), lambda i, j: (i, 0)),
            pl.BlockSpec((1, d), lambda i, j: (0, 0)),
            pl.BlockSpec((d, tn), lambda i, j: (0, j)),
            pl.BlockSpec((1, tn), lambda i, j: (0, jnp.minimum(j, n_qk_tiles - 1))),
            pl.BlockSpec((MXU_DIM, MXU_DIM), lambda i, j: (0, 0)),
        ],
        out_specs=[
            pl.BlockSpec((tm, tn), lambda i, j: (i, jnp.minimum(j, n_qk_tiles - 1))),
            pl.BlockSpec((tn, tm), lambda i, j: (jnp.maximum(j - n_qk_tiles, 0), i)),
        ],
        out_shape=[jax.ShapeDtypeStruct((s, n_qk), BF16),
                   jax.ShapeDtypeStruct((n - n_qk, s), BF16)],
        scratch_shapes=[pltpu.VMEM((tm, d), BF16)],
        compiler_params=_params("parallel", "arbitrary"),
        name="qkv_proj",
    )(x, g, w, gain_row, bd)


def _attn_kernel(q_ref, k_ref, vt_ref, shift_ref, lam_ref, gsub_ref, o_ref, qq_ref, acc_ref, m_ref,
                 l_ref, mask_ref, *, bq, lambda_init, unroll, online):
    i = pl.program_id(1)

    @pl.when((pl.program_id(0) == 0) & (i == 0))
    def _():
        key = lax.broadcasted_iota(jnp.int32, mask_ref.shape, 0)
        qry = lax.broadcasted_iota(jnp.int32, mask_ref.shape, 1)
        mask_ref[...] = jnp.where(key <= jnp.where(qry >= bq, qry - bq, qry), 0.0, NEG_BIG)

    q = q_ref[...]
    lane = lax.broadcasted_iota(jnp.int32, q.shape, 1)
    qq_ref[0:bq, :] = jnp.where(lane < HEAD_DIM, q, 0)
    qq_ref[bq:, :] = jnp.where(lane >= HEAD_DIM, q, 0)
    acc_ref[...] = jnp.zeros(acc_ref.shape, F32)
    l_ref[...] = jnp.zeros(l_ref.shape, F32)
    if online:
        m_ref[...] = jnp.full(m_ref.shape, NEG_BIG, F32)
    else:
        m_ref[...] = shift_ref[0]

    def block(j, masked):
        start = pl.multiple_of(j * bq, bq)
        kb = k_ref[pl.ds(start, bq), :]
        vb = vt_ref[:, pl.ds(start, bq)]
        s = _dot_nt(kb, qq_ref[...])
        if masked:
            s = s + mask_ref[...]
        if online:
            m_old = m_ref[...]
            m_new = jnp.maximum(m_old, jnp.max(s, axis=0, keepdims=True))
            alpha = jnp.exp2(m_old - m_new)
            p = jnp.exp2(s - m_new)
            l_ref[...] = alpha * l_ref[...] + jnp.sum(p, axis=0, keepdims=True)
            m_ref[...] = m_new
            acc_ref[...] = alpha * acc_ref[...] + _dot(vb, p.astype(BF16))
        else:
            p = jnp.exp2(s - m_ref[...])
            l_ref[...] += jnp.sum(p, axis=0, keepdims=True)
            acc_ref[...] += _dot(vb, p.astype(BF16))

    def group(g, _):
        for t in range(unroll):
            block(unroll * g + t, masked=False)
        return 0

    lax.fori_loop(0, i // unroll, group, 0)
    size = unroll // 2
    while size >= 1:
        @pl.when((i % unroll) & size != 0)
        def _(size=size):
            first = i - (i % (2 * size))
            for t in range(size):
                block(first + t, masked=False)
        size //= 2
    block(i, masked=True)

    l = l_ref[...]
    lam_vecs = lam_ref[...]
    lam = (jnp.exp(jnp.sum(lam_vecs[0:1] * lam_vecs[1:2], axis=-1, keepdims=True))
           - jnp.exp(jnp.sum(lam_vecs[2:3] * lam_vecs[3:4], axis=-1, keepdims=True))
           + lambda_init)
    o = acc_ref[:, 0:bq] / l[:, 0:bq] - lam * (acc_ref[:, bq:] / l[:, bq:])
    o = o * lax.rsqrt(jnp.mean(o * o, axis=0, keepdims=True) + NORM_EPS)
    o = o * (gsub_ref[...] * (1.0 - lambda_init))
    o_ref[...] = o.T.astype(BF16)


def _diff_attention(qk, vt, q_gain, k_gain, lam_vecs, gsub_col, lambda_init, bq):
    unit = math.sqrt(HEAD_DIM)
    q_norm = unit * jnp.max(jnp.abs(q_gain), axis=-1)
    k_norm = unit * jnp.max(jnp.abs(k_gain), axis=-1)
    shift = q_norm * k_norm * BOUND_SLACK
    fits = jnp.all(shift <= MAX_FIXED_SHIFT)
    shift_row = jnp.repeat(shift, bq)[None, None, :]

    def call(online):
        return _diff_attention_call(qk, vt, shift_row, lam_vecs, gsub_col, lambda_init, bq, online)

    return lax.cond(fits, lambda: call(False), lambda: call(True))


def _diff_attention_call(qk, vt, shift_row, lam_vecs, gsub_col, lambda_init, bq, online):
    s = qk.shape[0]
    hd2 = 2 * HEAD_DIM
    return pl.pallas_call(
        functools.partial(_attn_kernel, bq=bq, lambda_init=lambda_init, unroll=ATTN_UNROLL,
                          online=online),
        grid=(N_HEADS, s // bq),
        in_specs=[
            pl.BlockSpec((bq, hd2), lambda h, i: (i, h)),
            pl.BlockSpec((s, hd2), lambda h, i: (0, N_HEADS + h)),
            pl.BlockSpec((hd2, s), lambda h, i: (h, 0)),
            pl.BlockSpec((1, 1, 2 * bq), lambda h, i: (0, 0, 0)),
            pl.BlockSpec((4, HEAD_DIM), lambda h, i: (0, 0)),
            pl.BlockSpec((hd2, 1), lambda h, i: (0, 0)),
        ],
        out_specs=pl.BlockSpec((bq, hd2), lambda h, i: (i, h)),
        out_shape=jax.ShapeDtypeStruct((s, N_HEADS * hd2), BF16),
        scratch_shapes=[pltpu.VMEM((2 * bq, hd2), BF16), pltpu.VMEM((hd2, 2 * bq), F32),
                        pltpu.VMEM((1, 2 * bq), F32), pltpu.VMEM((1, 2 * bq), F32),
                        pltpu.VMEM((bq, 2 * bq), F32)],
        compiler_params=_params("arbitrary", "arbitrary"),
        name="diff_attn_online" if online else "diff_attn",
    )(qk, qk, vt, shift_row, lam_vecs, gsub_col)


def _proj_res_kernel(x_ref, a_ref, w_ref, o_ref, wb_ref):
    @pl.when(pl.program_id(0) == 0)
    def _():
        wb_ref[...] = w_ref[...].astype(BF16)

    o_ref[...] = x_ref[...] + _dot(a_ref[...], wb_ref[...])


def _proj_residual(x, a, w, tm):
    s, d = x.shape
    k = a.shape[1]
    return pl.pallas_call(
        _proj_res_kernel,
        grid=(s // tm,),
        in_specs=[
            pl.BlockSpec((tm, d), lambda i: (i, 0)),
            pl.BlockSpec((tm, k), lambda i: (i, 0)),
            pl.BlockSpec((k, d), lambda i: (0, 0)),
        ],
        out_specs=pl.BlockSpec((tm, d), lambda i: (i, 0)),
        out_shape=jax.ShapeDtypeStruct((s, d), F32),
        scratch_shapes=[pltpu.VMEM((k, d), BF16)],
        compiler_params=_params("arbitrary"),
        name="attn_out_proj",
    )(x, a, w)


def _swiglu_kernel(te_ref, nt_ref, x_ref, g_ref, wg_ref, wu_ref, wd_ref, o_ref, h_ref, acc_ref,
                   *, add_residual):
    t = pl.program_id(0)
    f = pl.program_id(1)

    @pl.when(t < nt_ref[0])
    def _():
        @pl.when(f == 0)
        def _():
            h_ref[...] = _rms(x_ref[...], g_ref[...]).astype(BF16)
            acc_ref[...] = jnp.zeros(acc_ref.shape, F32)

        h = h_ref[...]
        gate = _dot(h, wg_ref[0].astype(BF16))
        up = _dot(h, wu_ref[0].astype(BF16))
        act = (gate * jax.nn.sigmoid(gate) * up).astype(BF16)
        acc_ref[...] += _dot(act, wd_ref[0].astype(BF16))

        @pl.when(f == pl.num_programs(1) - 1)
        def _():
            if add_residual:
                o_ref[...] = x_ref[...] + acc_ref[...]
            else:
                o_ref[...] = acc_ref[...]

    @pl.when((t >= nt_ref[0]) & (f == 0))
    def _():
        o_ref[...] = jnp.zeros(o_ref.shape, F32)


def _grouped_swiglu(x, g, w_gate, w_up, w_down, tile_expert, n_tiles, tm, tf, add_residual):
    m, d = x.shape
    dff = w_gate.shape[2]
    nf = dff // tf
    max_tiles = tile_expert.shape[0]

    def _last(nt):
        return jnp.maximum(nt[0] - 1, 0)

    def row_map(t, f, te, nt):
        return (jnp.minimum(t, _last(nt)), 0)

    def _tf(t, f, nt):
        return jnp.where(t < nt[0], f, nf - 1)

    def _te(t, te, nt):
        return te[jnp.minimum(t, _last(nt))]

    return pl.pallas_call(
        functools.partial(_swiglu_kernel, add_residual=add_residual),
        grid_spec=pltpu.PrefetchScalarGridSpec(
            num_scalar_prefetch=2,
            grid=(max_tiles, nf),
            in_specs=[
                pl.BlockSpec((tm, d), row_map),
                pl.BlockSpec((1, d), lambda t, f, te, nt: (0, 0)),
                pl.BlockSpec((1, d, tf), lambda t, f, te, nt: (_te(t, te, nt), 0, _tf(t, f, nt))),
                pl.BlockSpec((1, d, tf), lambda t, f, te, nt: (_te(t, te, nt), 0, _tf(t, f, nt))),
                pl.BlockSpec((1, tf, d), lambda t, f, te, nt: (_te(t, te, nt), _tf(t, f, nt), 0)),
            ],
            out_specs=pl.BlockSpec((tm, d), lambda t, f, te, nt: (t, 0)),
            scratch_shapes=[pltpu.VMEM((tm, d), BF16), pltpu.VMEM((tm, d), F32)],
        ),
        out_shape=jax.ShapeDtypeStruct((m, d), F32),
        compiler_params=_params("arbitrary", "arbitrary"),
        name="swiglu_residual" if add_residual else "expert_swiglu",
    )(tile_expert, n_tiles, x, g, w_gate, w_up, w_down)


def _ple_tail(x, p_ref, wp_ref, wg_ref, g_e_ref, g_gate_ref, o_ref):
    e = _rms(_dot(p_ref[...].astype(BF16), wp_ref[...]), g_e_ref[...])
    gate = jax.nn.sigmoid(_dot(_rms(x, g_gate_ref[...]).astype(BF16), wg_ref[...]))
    o_ref[...] = x + gate * e


def _ple_kernel(x_ref, p_ref, wp_ref, wg_ref, g_e_ref, g_gate_ref, o_ref, wpb_ref, wgb_ref):
    @pl.when(pl.program_id(0) == 0)
    def _():
        wpb_ref[...] = wp_ref[...].astype(BF16)
        wgb_ref[...] = wg_ref[...].astype(BF16)

    _ple_tail(x_ref[...], p_ref, wpb_ref, wgb_ref, g_e_ref, g_gate_ref, o_ref)


def _combine_ple_kernel(pos_ref, pos_next_ref, x_ref, w_ref, ys_ref, p_ref, wp_ref, wg_ref, g_e_ref,
                        g_gate_ref, o_ref, wpb_ref, wgb_ref, ybuf_ref, sem, *, tc):
    i = pl.program_id(0)
    slot = i % 2

    def issue_tile(rows_ref, dst_slot):
        def issue(t, _):
            for k in range(2):
                pltpu.make_async_copy(ys_ref.at[pl.ds(rows_ref[k, t], 1), :],
                                      ybuf_ref.at[dst_slot, k, pl.ds(t, 1), :],
                                      sem.at[dst_slot]).start(priority=k)
            return 0
        lax.fori_loop(0, tc, issue, 0, unroll=8)

    @pl.when(i == 0)
    def _():
        wpb_ref[...] = wp_ref[...].astype(BF16)
        wgb_ref[...] = wg_ref[...].astype(BF16)
        issue_tile(pos_ref, 0)

    @pl.when(i + 1 < pl.num_programs(0))
    def _():
        issue_tile(pos_next_ref, 1 - slot)

    for k in range(2):
        pltpu.make_async_copy(ys_ref.at[pl.ds(0, tc), :], ybuf_ref.at[slot, k], sem.at[slot]).wait()
    w = w_ref[...]
    x = x_ref[...] + w[:, 0:1] * ybuf_ref[slot, 0] + w[:, 1:2] * ybuf_ref[slot, 1]
    _ple_tail(x, p_ref, wpb_ref, wgb_ref, g_e_ref, g_gate_ref, o_ref)


def _ple(x, p_layer, wp, wg, g_e, g_gate, tm, routed=None):
    s, d = x.shape
    p, layer, batch = p_layer
    pd = p.shape[-1]
    row = lambda i: (i, 0)
    fixed = lambda i: (0, 0)
    tail_specs = [
        pl.BlockSpec((None, None, tm, pd), lambda i: (layer, batch, i, 0)),
        pl.BlockSpec((pd, d), fixed),
        pl.BlockSpec((d, d), fixed),
        pl.BlockSpec((1, d), fixed),
        pl.BlockSpec((1, d), fixed),
    ]
    scratch = [pltpu.VMEM((pd, d), BF16), pltpu.VMEM((d, d), BF16)]
    if routed is None:
        return pl.pallas_call(
            _ple_kernel,
            grid=(s // tm,),
            in_specs=[pl.BlockSpec((tm, d), row)] + tail_specs,
            out_specs=pl.BlockSpec((tm, d), row),
            out_shape=jax.ShapeDtypeStruct((s, d), F32),
            scratch_shapes=scratch,
            compiler_params=_params("arbitrary"),
            name="ple",
        )(x, p, wp, wg, g_e, g_gate)
    pos, w12, ys = routed
    last = s // tm - 1
    return pl.pallas_call(
        functools.partial(_combine_ple_kernel, tc=tm),
        grid=(s // tm,),
        in_specs=[
            pl.BlockSpec((2, tm), lambda i: (0, i), memory_space=pltpu.SMEM),
            pl.BlockSpec((2, tm), lambda i: (0, jnp.minimum(i + 1, last)), memory_space=pltpu.SMEM),
            pl.BlockSpec((tm, d), row),
            pl.BlockSpec((tm, 2), row),
            pl.BlockSpec(memory_space=pl.ANY),
        ] + tail_specs,
        out_specs=pl.BlockSpec((tm, d), row),
        out_shape=jax.ShapeDtypeStruct((s, d), F32),
        scratch_shapes=scratch + [pltpu.VMEM((2, 2, tm, d), F32), pltpu.SemaphoreType.DMA((2,))],
        compiler_params=_params("arbitrary"),
        name="combine_ple",
    )(pos, pos, x, w12, ys, p, wp, wg, g_e, g_gate)


def _route(x, g_ref, wrt_ref, b_ref, tri_ref, carry_ref, info_ref, cnt_ref):
    h = _rms(x, g_ref[...])
    logits = lax.dot_general(wrt_ref[...], h, (((1,), (1,)), ((), ())),
                             preferred_element_type=F32,
                             precision=lax.Precision.HIGHEST) + b_ref[...]
    eidx = lax.broadcasted_iota(jnp.int32, logits.shape, 0)
    m1 = jnp.max(logits, axis=0, keepdims=True)
    e1 = jnp.min(jnp.where(logits == m1, eidx, N_EXPERTS), axis=0, keepdims=True)
    rest = jnp.where(eidx == e1, -jnp.inf, logits)
    m2 = jnp.max(rest, axis=0, keepdims=True)
    e2 = jnp.min(jnp.where(rest == m2, eidx, N_EXPERTS), axis=0, keepdims=True)
    z = jnp.exp(m2 - m1)
    w1 = 1.0 / (1.0 + z)
    w2 = z / (1.0 + z)
    sel1 = eidx == e1
    sel2 = eidx == e2
    sel = (sel1 | sel2).astype(F32)
    rank = carry_ref[:, 0:1] + _dot(sel.astype(BF16), tri_ref[...])
    r1 = jnp.sum(jnp.where(sel1, rank, 0.0), axis=0, keepdims=True)
    r2 = jnp.sum(jnp.where(sel2, rank, 0.0), axis=0, keepdims=True)
    rows = (e1.astype(F32), e2.astype(F32), w1, w2, r1, r2)
    info = jnp.zeros(logits.shape, F32)
    for k, v in enumerate(rows):
        info = jnp.where(eidx == k, v, info)
    info_ref[...] = info
    carry_ref[...] = carry_ref[...] + jnp.sum(sel, axis=1, keepdims=True)
    cnt_ref[...] = carry_ref[...]


def _conv_router_kernel(x_ref, g_ref, w1_ref, b1_ref, wdw_ref, bdw_ref, lng_ref, lnb_ref, w2_ref, b2_ref,
                        rg_ref, wrt_ref, rb_ref, o_ref, info_ref, cnt_ref,
                        ext_ref, c_ref, tri_ref, carry_ref, *, tm, rc):
    i = pl.program_id(0)
    d = x_ref.shape[1]

    @pl.when(i == 0)
    def _():
        ext_ref[0:HALO_ROWS, :] = jnp.zeros((HALO_ROWS, d), F32)
        a = lax.broadcasted_iota(jnp.int32, (tm, tm), 0)
        b = lax.broadcasted_iota(jnp.int32, (tm, tm), 1)
        tri_ref[...] = (a < b).astype(BF16)
        carry_ref[...] = jnp.zeros(carry_ref.shape, F32)

    @pl.when(i > 0)
    def _():
        ext_ref[0:HALO_ROWS, :] = ext_ref[tm:tm + HALO_ROWS, :]

    x = x_ref[...]
    y = _dot(_rms(x, g_ref[...]).astype(BF16), w1_ref[...]) + b1_ref[...]
    ext_ref[HALO_ROWS:, :] = y[:, :d] * jax.nn.sigmoid(y[:, d:])
    first = HALO_ROWS - (CONV_WIDTH - 1)
    win_rows = rc + HALO_ROWS

    def chunk(r, _):
        base = pl.multiple_of(r * rc, rc)
        for lo in range(0, d, LANES):
            cols = slice(lo, lo + LANES)
            win = ext_ref[pl.ds(base, win_rows), cols]
            acc = jnp.zeros((rc, LANES), F32) + bdw_ref[:, cols]
            for b in range(SUBLANES):
                wb = win if b == 0 else pltpu.roll(win, shift=win_rows - b, axis=0)
                for a in range(win_rows // SUBLANES):
                    t = SUBLANES * a + b - first
                    if 0 <= t < CONV_WIDTH:
                        acc = acc + wdw_ref[t:t + 1, cols] * wb[SUBLANES * a:SUBLANES * a + rc, :]
            c_ref[pl.ds(base, rc), cols] = acc
        return 0

    lax.fori_loop(0, tm // rc, chunk, 0)
    c = c_ref[...]
    mu = jnp.mean(c, axis=-1, keepdims=True)
    cc = c - mu
    y = cc * lax.rsqrt(jnp.mean(cc * cc, axis=-1, keepdims=True) + LN_EPS) * lng_ref[...] + lnb_ref[...]
    y = y * jax.nn.sigmoid(y)
    x1 = x + _dot(y.astype(BF16), w2_ref[...]) + b2_ref[...]
    o_ref[...] = x1
    _route(x1, rg_ref, wrt_ref, rb_ref, tri_ref, carry_ref, info_ref, cnt_ref)


def _conv_module_router(x, g, w1, b1, w_dw, b_dw, ln_g, ln_b, w2, b2, rg, w_router_t, rb_col, tm):
    s, d = x.shape
    row = lambda i: (i, 0)
    fixed = lambda i: (0, 0)
    return pl.pallas_call(
        functools.partial(_conv_router_kernel, tm=tm, rc=min(64, tm)),
        grid=(s // tm,),
        in_specs=[
            pl.BlockSpec((tm, d), row),
            pl.BlockSpec((1, d), fixed),
            pl.BlockSpec((d, 2 * d), fixed),
            pl.BlockSpec((1, 2 * d), fixed),
            pl.BlockSpec((CONV_WIDTH, d), fixed),
            pl.BlockSpec((1, d), fixed),
            pl.BlockSpec((1, d), fixed),
            pl.BlockSpec((1, d), fixed),
            pl.BlockSpec((d, d), fixed),
            pl.BlockSpec((1, d), fixed),
            pl.BlockSpec((1, d), fixed),
            pl.BlockSpec((N_EXPERTS, d), fixed),
            pl.BlockSpec((N_EXPERTS, 1), fixed),
        ],
        out_specs=[
            pl.BlockSpec((tm, d), row),
            pl.BlockSpec((N_EXPERTS, tm), lambda i: (0, i)),
            pl.BlockSpec((N_EXPERTS, LANES), fixed),
        ],
        out_shape=[jax.ShapeDtypeStruct((s, d), F32),
                   jax.ShapeDtypeStruct((N_EXPERTS, s), F32),
                   jax.ShapeDtypeStruct((N_EXPERTS, LANES), F32)],
        scratch_shapes=[pltpu.VMEM((tm + HALO_ROWS, d), F32), pltpu.VMEM((tm, d), F32),
                        pltpu.VMEM((tm, tm), BF16), pltpu.VMEM((N_EXPERTS, LANES), F32)],
        compiler_params=_params("arbitrary"),
        name="conv_module_router",
    )(x, g, w1, b1, w_dw, b_dw, ln_g, ln_b, w2, b2, rg, w_router_t, rb_col)


def _dispatch_kernel(pos_ref, x_ref, xs_in_ref, xs_ref, sem, *, td):
    del xs_in_ref

    def row_copy(k, t):
        return pltpu.make_async_copy(x_ref.at[pl.ds(t, 1), :],
                                     xs_ref.at[pl.ds(pos_ref[k, t], 1), :], sem)

    def issue(t, _):
        row_copy(0, t).start(priority=0)
        row_copy(1, t).start(priority=1)
        return 0

    lax.fori_loop(0, td, issue, 0, unroll=8)
    for _ in range(2):
        pltpu.make_async_copy(x_ref, xs_ref.at[pl.ds(0, td), :], sem).wait()


def _dispatch(x, pos, m_pad, td):
    s, d = x.shape
    return pl.pallas_call(
        functools.partial(_dispatch_kernel, td=td),
        grid=(s // td,),
        in_specs=[
            pl.BlockSpec((2, td), lambda i: (0, i), memory_space=pltpu.SMEM),
            pl.BlockSpec((td, d), lambda i: (i, 0)),
            pl.BlockSpec(memory_space=pl.ANY),
        ],
        out_specs=pl.BlockSpec(memory_space=pl.ANY),
        out_shape=jax.ShapeDtypeStruct((m_pad, d), F32),
        scratch_shapes=[pltpu.SemaphoreType.DMA(())],
        input_output_aliases={2: 0},
        compiler_params=_params("arbitrary"),
        name="moe_dispatch",
    )(pos, x, jnp.zeros((m_pad, d), F32))


def _tiles(s):
    t = lambda want: min(want, s)
    return dict(qkv=(t(1024), 1024), attn=t(512), proj=t(1024), ffn=(t(1024), 512), ple=t(1024),
                conv=t(1024), move=t(1024), expert=(t(1024), 512))


def _layer_even(x, p, tiles, mix_g, ffn_g, w_qkv, w_o, g_q, g_k, lam_vecs, g_sub, lambda_init,
                w_gate, w_up, w_down, ple_wp, ple_g, ple_gate_g, ple_wg):
    s, d = x.shape
    q_gain = g_q * (ATTN_SCALE * LOG2_E)
    gain_row = jnp.concatenate([jnp.tile(q_gain.reshape(-1), N_HEADS),
                                jnp.tile(g_k.reshape(-1), N_HEADS)])[None, :]
    qk, vt = _qkv_proj(x, mix_g[None, :], w_qkv, gain_row, *tiles["qkv"])
    o = _diff_attention(qk, vt, q_gain, g_k, lam_vecs, g_sub[:, None], lambda_init, tiles["attn"])
    x = _proj_residual(x, o, w_o, tiles["proj"])
    tm, tf = tiles["ffn"]
    n_tiles = s // tm
    x = _grouped_swiglu(x, ffn_g[None, :], w_gate[None], w_up[None], w_down[None],
                        jnp.zeros((n_tiles,), jnp.int32), jnp.full((1,), n_tiles, jnp.int32),
                        tm, tf, add_residual=True)
    return _ple(x, p, ple_wp, ple_wg, ple_g[None, :], ple_gate_g[None, :], tiles["ple"])


def _layer_odd(x, p, tiles, mix_g, ffn_g, w_pw1, b_pw1, w_dw, b_dw, ln_g, ln_b, w_pw2, b_pw2,
               w_router, b_router, w_gate, w_up, w_down, ple_wp, ple_g, ple_gate_g, ple_wg):
    s, d = x.shape
    x, info, counts = _conv_module_router(
        x, mix_g[None, :], w_pw1.astype(BF16), b_pw1[None, :], w_dw, b_dw[None, :], ln_g[None, :],
        ln_b[None, :], w_pw2.astype(BF16), b_pw2[None, :], ffn_g[None, :], w_router.T,
        b_router[:, None], tiles["conv"])
    tm, tf = tiles["expert"]
    e12 = info[0:2].astype(jnp.int32)
    w12 = info[2:4].T
    r12 = info[4:6].astype(jnp.int32)
    cnt = counts[:, 0].astype(jnp.int32)
    group_tiles = (cnt + tm - 1) // tm
    tile_end = jnp.cumsum(group_tiles)
    offsets = (tile_end - group_tiles) * tm
    expert_ids = jnp.arange(N_EXPERTS, dtype=jnp.int32)
    onehot = e12[:, :, None] == expert_ids[None, None, :]
    pos = jnp.sum(jnp.where(onehot, offsets[None, None, :], 0), axis=-1) + r12
    max_tiles = (2 * s) // tm + N_EXPERTS
    tile_ids = jnp.arange(max_tiles, dtype=jnp.int32)
    tile_expert = jnp.minimum(jnp.sum((tile_end[None, :] <= tile_ids[:, None]).astype(jnp.int32), axis=1),
                              N_EXPERTS - 1)
    n_tiles = tile_end[-1:].astype(jnp.int32)
    xs = _dispatch(x, pos, max_tiles * tm, tiles["move"])
    ys = _grouped_swiglu(xs, ffn_g[None, :], w_gate, w_up, w_down, tile_expert, n_tiles,
                         tm, tf, add_residual=False)
    return _ple(x, p, ple_wp, ple_wg, ple_g[None, :], ple_gate_g[None, :], tiles["move"],
                routed=(pos, w12, ys))


def kernel(x, p, mix_norm_g, ffn_norm_g, attn_w_qkv, attn_w_o, attn_g_q, attn_g_k, attn_lam_q1, attn_lam_k1, attn_lam_q2, attn_lam_k2, attn_g_sub, conv_w_pw1, conv_b_pw1, conv_w_dw, conv_b_dw, conv_ln_g, conv_ln_b, conv_w_pw2, conv_b_pw2, ffn_w_gate, ffn_w_up, ffn_w_down, moe_w_router, moe_b_router, moe_w_gate, moe_w_up, moe_w_down, ple_w_proj, ple_norm_g, ple_gate_norm_g, ple_w_gate):
    depth, batch, s, _ = p.shape
    tiles = _tiles(s)
    outs = []
    for b in range(batch):
        xb = x[b]
        for i in range(depth):
            j = i // 2
            ple_args = (ple_w_proj[i], ple_norm_g[i], ple_gate_norm_g[i], ple_w_gate[i])
            if i % 2 == 0:
                lambda_init = 0.8 - 0.6 * math.exp(-0.3 * i)
                lam_vecs = jnp.stack([attn_lam_q1[j], attn_lam_k1[j], attn_lam_q2[j], attn_lam_k2[j]])
                xb = _layer_even(xb, (p, i, b), tiles, mix_norm_g[i], ffn_norm_g[i], attn_w_qkv[j],
                                 attn_w_o[j], attn_g_q[j], attn_g_k[j], lam_vecs, attn_g_sub[j],
                                 lambda_init, ffn_w_gate[j], ffn_w_up[j], ffn_w_down[j], *ple_args)
            else:
                xb = _layer_odd(xb, (p, i, b), tiles, mix_norm_g[i], ffn_norm_g[i], conv_w_pw1[j],
                                conv_b_pw1[j], conv_w_dw[j], conv_b_dw[j], conv_ln_g[j], conv_ln_b[j],
                                conv_w_pw2[j], conv_b_pw2[j], moe_w_router[j], moe_b_router[j],
                                moe_w_gate[j], moe_w_up[j], moe_w_down[j], *ple_args)
        outs.append(xb)
    return jnp.stack(outs)
```
